```python
import math
import jax
import jax.numpy as jnp
from jax import lax
import numpy as np

D_MODEL = 1024
BATCH = 1
SEQ = 16384
DEPTH = 4

N_MIXERS = 3
N_SSD_LAYERS = (DEPTH + N_MIXERS - 1) // N_MIXERS
N_GDN_LAYERS = (DEPTH + N_MIXERS - 2) // N_MIXERS
N_S5_LAYERS = DEPTH // N_MIXERS
NORM_EPS = 1e-6
CONV_WIDTH = 5
D_FF = ((8 * D_MODEL + 3 * 256 - 1) // (3 * 256)) * 256

SSD_D_INNER = 2 * D_MODEL
SSD_HEAD_DIM = 64
SSD_HEADS = SSD_D_INNER // SSD_HEAD_DIM
SSD_GROUPS = 8
SSD_STATE = 128
SSD_CHUNK = 128
SSD_GN = SSD_GROUPS * SSD_STATE
SSD_CONV_CH = SSD_D_INNER + 2 * SSD_GN
SSD_IN = 2 * SSD_D_INNER + 2 * SSD_GN + 2 * SSD_HEADS

GDN_HEAD_DIM = 128
GDN_QK_HEADS = D_MODEL // GDN_HEAD_DIM
GDN_V_HEADS = 2 * GDN_QK_HEADS
GDN_KEY_DIM = GDN_QK_HEADS * GDN_HEAD_DIM
GDN_VALUE_DIM = GDN_V_HEADS * GDN_HEAD_DIM
GDN_CHUNK = 64
GDN_CONV_CH = 2 * GDN_KEY_DIM + GDN_VALUE_DIM
GDN_IN = GDN_CONV_CH + GDN_VALUE_DIM + 4 * GDN_V_HEADS

S5_GROUP = 16
S5_GROUPS = D_MODEL // S5_GROUP
S5_STATE = 64

kernel_name = 'hybrid_bidir_ssd_gdn_s5_encoder'


def rms_norm(x, w):
    xf = x.astype(jnp.float32)
    y = xf * lax.rsqrt(jnp.mean(xf * xf, axis=-1, keepdims=True) + NORM_EPS)
    return (y * w.astype(jnp.float32)).astype(x.dtype)


def l2_normalize(x):
    return x * lax.rsqrt(jnp.sum(x * x, axis=-1, keepdims=True) + NORM_EPS)


def _flip(t):
    return jnp.flip(t, axis=1)


def centred_depthwise_conv(x, w, b):
    k, ch = w.shape
    pad = k // 2
    y = lax.conv_general_dilated(x, w[:, None, :].astype(x.dtype), window_strides=(1,),
                                 padding=[(pad, pad)], dimension_numbers=('NWC', 'WIO', 'NWC'),
                                 feature_group_count=ch)
    return y + b


def _exp_segsum(a_cs):
    l = a_cs.shape[-1]
    diff = a_cs[..., :, None] - a_cs[..., None, :]
    mask = jnp.tril(jnp.ones((l, l), dtype=bool))
    return jnp.exp(jnp.where(mask, diff, -jnp.inf))


def ssd_chunked(x, dt, a, b_in, c_out):
    f32 = jnp.float32
    bt, L, h, p = x.shape
    g, n = b_in.shape[-2:]
    r = h // g
    q = SSD_CHUNK
    nc = L // q
    x, dt, b_in, c_out = x.astype(f32), dt.astype(f32), b_in.astype(f32), c_out.astype(f32)
    xdt = (x * dt[..., None]).reshape(bt, nc, q, g, r, p)
    da = jnp.moveaxis((dt * a).reshape(bt, nc, q, g, r), 2, -1)
    da_cs = jnp.cumsum(da, axis=-1)
    bc = b_in.reshape(bt, nc, q, g, n)
    cc = c_out.reshape(bt, nc, q, g, n)
    lmat = _exp_segsum(da_cs)
    cb = jnp.einsum('bclgn,bcsgn->bcgls', cc, bc)
    y_diag = jnp.einsum('bcgls,bcgrls,bcsgrp->bclgrp', cb, lmat, xdt)
    decay_to_end = jnp.exp(da_cs[..., -1:] - da_cs)
    chunk_states = jnp.einsum('bclgn,bcgrl,bclgrp->bcgrpn', bc, decay_to_end, xdt)
    chunk_decay = jnp.exp(da_cs[..., -1])

    def step(state, inp):
        s_c, d_c = inp
        return state * d_c[..., None, None] + s_c, state

    init = jnp.zeros_like(chunk_states[:, 0])
    _, prev = lax.scan(step, init, (jnp.moveaxis(chunk_states, 1, 0), jnp.moveaxis(chunk_decay, 1, 0)))
    prev = jnp.moveaxis(prev, 0, 1)
    y_off = jnp.einsum('bclgn,bcgrpn,bcgrl->bclgrp', cc, prev, jnp.exp(da_cs))
    return (y_diag + y_off).reshape(bt, L, h, p)


def gated_group_rms_norm(y, z, w, groups):
    yz = y * jax.nn.silu(z)
    shp = yz.shape
    yz = yz.reshape(shp[:-1] + (groups, shp[-1] // groups))
    return rms_norm(yz, w.reshape(groups, -1)).reshape(shp)


def mamba2_mixer(h, w_in, conv_w, conv_b, dt_bias, a_log, d_skip, norm_w, w_out):
    f32 = jnp.float32
    bt, L, _ = h.shape
    proj = h @ w_in
    z = proj[..., :SSD_D_INNER]
    xbc = proj[..., SSD_D_INNER:SSD_D_INNER + SSD_CONV_CH]
    dt_raw = proj[..., SSD_D_INNER + SSD_CONV_CH:]
    xbc = jax.nn.silu(centred_depthwise_conv(xbc, conv_w, conv_b))
    xs = xbc[..., :SSD_D_INNER].reshape(bt, L, SSD_HEADS, SSD_HEAD_DIM)
    b_in = xbc[..., SSD_D_INNER:SSD_D_INNER + SSD_GN].reshape(bt, L, SSD_GROUPS, SSD_STATE)
    c_out = xbc[..., SSD_D_INNER + SSD_GN:].reshape(bt, L, SSD_GROUPS, SSD_STATE)
    dt = jax.nn.softplus(dt_raw.astype(f32).reshape(bt, L, 2, SSD_HEADS) + dt_bias.astype(f32))
    a = -jnp.exp(a_log.astype(f32))
    y_fwd = ssd_chunked(xs, dt[:, :, 0], a[0], b_in, c_out)
    y_bwd = _flip(ssd_chunked(_flip(xs), _flip(dt[:, :, 1]), a[1], _flip(b_in), _flip(c_out)))
    y = y_fwd + y_bwd + xs.astype(f32) * d_skip.astype(f32)[:, None]
    y = y.reshape(bt, L, SSD_D_INNER).astype(h.dtype)
    y = gated_group_rms_norm(y, z, norm_w, SSD_GROUPS)
    return y @ w_out


def gated_delta_chunked(q, k, v, g, beta):
    bt, L, h, dk = q.shape
    dv = v.shape[-1]
    c = GDN_CHUNK
    nc = L // c

    def chunks(t):
        return jnp.swapaxes(t.reshape((bt, nc, c) + t.shape[2:]), 2, 3)

    q, k, v, g, beta = chunks(q), chunks(k), chunks(v), chunks(g), chunks(beta)
    g_cs = jnp.cumsum(g, axis=-1)
    k_beta = k * beta[..., None]
    v_beta = v * beta[..., None]
    decay = _exp_segsum(g_cs)
    strict = jnp.tril(jnp.ones((c, c), dtype=bool), -1)
    a_mat = jnp.where(strict, jnp.einsum('bnhid,bnhjd->bnhij', k_beta, k) * decay, 0.0)
    rhs = jnp.concatenate([v_beta, k_beta * jnp.exp(g_cs)[..., None]], axis=-1)
    sol = lax.linalg.triangular_solve(a_mat, rhs, left_side=True, lower=True, unit_diagonal=True)
    u, w = sol[..., :dv], sol[..., dv:]
    qk = jnp.einsum('bnhid,bnhjd->bnhij', q, k) * decay

    def step(s, inp):
        q_i, k_i, u_i, w_i, qk_i, gcs_i = inp
        v_new = u_i - jnp.einsum('bhcd,bhde->bhce', w_i, s)
        o_i = (jnp.einsum('bhcd,bhde->bhce', q_i * jnp.exp(gcs_i)[..., None], s)
               + jnp.einsum('bhij,bhje->bhie', qk_i, v_new))
        g_last = gcs_i[..., -1]
        k_dec = k_i * jnp.exp(g_last[..., None] - gcs_i)[..., None]
        s = s * jnp.exp(g_last)[..., None, None] + jnp.einsum('bhcd,bhce->bhde', k_dec, v_new)
        return s, o_i

    init = jnp.zeros((bt, h, dk, dv), dtype=q.dtype)
    mv = lambda t: jnp.moveaxis(t, 1, 0)
    _, o = lax.scan(step, init, (mv(q), mv(k), mv(u), mv(w), mv(qk), mv(g_cs)))
    o = jnp.swapaxes(jnp.moveaxis(o, 0, 1), 2, 3)
    return o.reshape(bt, L, h, dv)


def gdn_mixer(h, w_in, conv_w, conv_b, dt_bias, a_log, norm_w, w_out):
    f32 = jnp.float32
    bt, L, _ = h.shape
    proj = h @ w_in
    qkv = jax.nn.silu(centred_depthwise_conv(proj[..., :GDN_CONV_CH], conv_w, conv_b)).astype(f32)
    o0 = GDN_CONV_CH + GDN_VALUE_DIM
    z = proj[..., GDN_CONV_CH:o0]
    a_raw = proj[..., o0:o0 + 2 * GDN_V_HEADS]
    b_raw = proj[..., o0 + 2 * GDN_V_HEADS:]
    rep = GDN_V_HEADS // GDN_QK_HEADS
    q = l2_normalize(qkv[..., :GDN_KEY_DIM].reshape(bt, L, GDN_QK_HEADS, GDN_HEAD_DIM)) * (GDN_HEAD_DIM ** -0.5)
    k = l2_normalize(qkv[..., GDN_KEY_DIM:2 * GDN_KEY_DIM].reshape(bt, L, GDN_QK_HEADS, GDN_HEAD_DIM))
    q = jnp.repeat(q, rep, axis=2)
    k = jnp.repeat(k, rep, axis=2)
    v = qkv[..., 2 * GDN_KEY_DIM:].reshape(bt, L, GDN_V_HEADS, GDN_HEAD_DIM)
    g = -jnp.exp(a_log.astype(f32)) * jax.nn.softplus(
        a_raw.astype(f32).reshape(bt, L, 2, GDN_V_HEADS) + dt_bias.astype(f32))
    beta = jax.nn.sigmoid(b_raw.astype(f32).reshape(bt, L, 2, GDN_V_HEADS))
    o_fwd = gated_delta_chunked(q, k, v, g[:, :, 0], beta[:, :, 0])
    o_bwd = _flip(gated_delta_chunked(_flip(q), _flip(k), _flip(v), _flip(g[:, :, 1]), _flip(beta[:, :, 1])))
    o = rms_norm(o_fwd + o_bwd, norm_w) * jax.nn.silu(z.astype(f32).reshape(bt, L, GDN_V_HEADS, GDN_HEAD_DIM))
    return o.reshape(bt, L, GDN_VALUE_DIM).astype(h.dtype) @ w_out


def _cmul(ar, ai, br, bi):
    return ar * br - ai * bi, ar * bi + ai * br


def _s5_combine(e1, e2):
    a1r, a1i, b1r, b1i = e1
    a2r, a2i, b2r, b2i = e2
    ar, ai = _cmul(a2r, a2i, a1r, a1i)
    br, bi = _cmul(a2r, a2i, b1r, b1i)
    return ar, ai, br + b2r, bi + b2i


def s5_direction(u, lam_re, lam_im, log_step, b_re, b_im, c_re, c_im):
    L = u.shape[0]
    step = jnp.exp(log_step)[:, None]
    mag = jnp.exp(lam_re * step)
    ang = lam_im * step
    lbar_r, lbar_i = mag * jnp.cos(ang), mag * jnp.sin(ang)
    inv_den = 1.0 / (lam_re * lam_re + lam_im * lam_im)
    zr, zi = _cmul(lbar_r - 1.0, lbar_i, lam_re * inv_den, -lam_im * inv_den)
    bb_r, bb_i = _cmul(zr[..., None], zi[..., None], b_re, b_im)
    xr = jnp.einsum('lbgc,gnc->lbgn', u, bb_r)
    xi = jnp.einsum('lbgc,gnc->lbgn', u, bb_i)
    ar = jnp.broadcast_to(lbar_r, (L, 1) + lbar_r.shape)
    ai = jnp.broadcast_to(lbar_i, (L, 1) + lbar_i.shape)
    _, _, sr, si = lax.associative_scan(_s5_combine, (ar, ai, xr, xi), axis=0)
    return jnp.einsum('lbgn,gcn->lbgc', sr, c_re) - jnp.einsum('lbgn,gcn->lbgc', si, c_im)


def s5_mixer(h, lam_re, lam_im, log_step, b_re, b_im, c_re, c_im, d_skip, w_glu, b_glu):
    f32 = jnp.float32
    bt, L, d = h.shape
    lam_re, lam_im, log_step = lam_re.astype(f32), lam_im.astype(f32), log_step.astype(f32)
    b_re, b_im, c_re, c_im = b_re.astype(f32), b_im.astype(f32), c_re.astype(f32), c_im.astype(f32)
    u = jnp.swapaxes(h.astype(f32).reshape(bt, L, S5_GROUPS, S5_GROUP), 0, 1)
    y_fwd = s5_direction(u, lam_re[0], lam_im[0], log_step[0], b_re, b_im, c_re[0], c_im[0])
    y_bwd = jnp.flip(s5_direction(jnp.flip(u, axis=0), lam_re[1], lam_im[1], log_step[1],
                                  b_re, b_im, c_re[1], c_im[1]), axis=0)
    y = jnp.swapaxes(y_fwd + y_bwd, 0, 1).reshape(bt, L, d).astype(h.dtype) + d_skip * h
    y = jax.nn.gelu(y)
    val, gate = jnp.split(y @ w_glu + b_glu, 2, axis=-1)
    return val * jax.nn.sigmoid(gate)


def swiglu_ffn(h, w_gate_up, w_down):
    gate, up = jnp.split(h @ w_gate_up, 2, axis=-1)
    return (jax.nn.silu(gate) * up) @ w_down


def setup_inputs(seed: int = 0) -> dict:
    key = jax.random.key(seed)
    keys = jax.random.split(key, 64)
    counter = [0]
    f32 = jnp.float32

    def nk():
        kk = keys[counter[0]]
        counter[0] += 1
        return kk

    def normal(shape, scale):
        return jax.random.normal(nk(), shape, f32) * scale

    def gain(shape):
        return 1.0 + 0.01 * jax.random.normal(nk(), shape, f32)

    def small(shape):
        return 0.01 * jax.random.normal(nk(), shape, f32)

    def dt_bias(shape):
        dt = jnp.exp(jax.random.uniform(nk(), shape, f32, math.log(1e-3), math.log(1e-1)))
        return dt + jnp.log(-jnp.expm1(-dt))

    def a_log(shape):
        return jnp.log(jax.random.uniform(nk(), shape, f32, 1.0, 16.0))

    na, nb, nc = N_SSD_LAYERS, N_GDN_LAYERS, N_S5_LAYERS
    n_idx = jnp.arange(S5_STATE, dtype=f32)
    lam_im0 = jnp.broadcast_to(math.pi * n_idx, (nc, 2, S5_GROUPS, S5_STATE))
    return {
        'x': jax.random.normal(nk(), (BATCH, SEQ, D_MODEL), f32),
        'norm_w': gain((DEPTH, 4, D_MODEL)),
        'ssd_w_in': normal((na, D_MODEL, SSD_IN), D_MODEL ** -0.5),
        'ssd_conv_w': normal((na, CONV_WIDTH, SSD_CONV_CH), CONV_WIDTH ** -0.5),
        'ssd_conv_b': small((na, SSD_CONV_CH)),
        'ssd_dt_bias': dt_bias((na, 2, SSD_HEADS)),
        'ssd_a_log': a_log((na, 2, SSD_HEADS)),
        'ssd_d': gain((na, SSD_HEADS)),
        'ssd_norm_w': gain((na, SSD_D_INNER)),
        'ssd_w_out': normal((na, SSD_D_INNER, D_MODEL), SSD_D_INNER ** -0.5),
        'gdn_w_in': normal((nb, D_MODEL, GDN_IN), D_MODEL ** -0.5),
        'gdn_conv_w': normal((nb, CONV_WIDTH, GDN_CONV_CH), CONV_WIDTH ** -0.5),
        'gdn_conv_b': small((nb, GDN_CONV_CH)),
        'gdn_dt_bias': dt_bias((nb, 2, GDN_V_HEADS)),
        'gdn_a_log': a_log((nb, 2, GDN_V_HEADS)),
        'gdn_norm_w': gain((nb, GDN_HEAD_DIM)),
        'gdn_w_out': normal((nb, GDN_VALUE_DIM, D_MODEL), GDN_VALUE_DIM ** -0.5),
        's5_lam_re': -0.5 + small((nc, 2, S5_GROUPS, S5_STATE)),
        's5_lam_im': lam_im0 + small((nc, 2, S5_GROUPS, S5_STATE)),
        's5_log_step': jax.random.uniform(nk(), (nc, 2, S5_GROUPS), f32, math.log(1e-3), math.log(1e-1)),
        's5_b_re': normal((nc, S5_GROUPS, S5_STATE, S5_GROUP), (2 * S5_GROUP) ** -0.5),
        's5_b_im': normal((nc, S5_GROUPS, S5_STATE, S5_GROUP), (2 * S5_GROUP) ** -0.5),
        's5_c_re': normal((nc, 2, S5_GROUPS, S5_GROUP, S5_STATE), S5_STATE ** -0.5),
        's5_c_im': normal((nc, 2, S5_GROUPS, S5_GROUP, S5_STATE), S5_STATE ** -0.5),
        's5_d': normal((nc, D_MODEL), 1.0),
        's5_w_glu': normal((nc, D_MODEL, 2 * D_MODEL), D_MODEL ** -0.5),
        's5_b_glu': small((nc, 2 * D_MODEL)),
        'ffn_w_gate_up': normal((DEPTH, D_MODEL, 2 * D_FF), D_MODEL ** -0.5),
        'ffn_w_down': normal((DEPTH, D_FF, D_MODEL), D_FF ** -0.5),
    }


def reference(x, norm_w, ssd_w_in, ssd_conv_w, ssd_conv_b, ssd_dt_bias, ssd_a_log, ssd_d,
              ssd_norm_w, ssd_w_out, gdn_w_in, gdn_conv_w, gdn_conv_b, gdn_dt_bias, gdn_a_log,
              gdn_norm_w, gdn_w_out, s5_lam_re, s5_lam_im, s5_log_step, s5_b_re, s5_b_im,
              s5_c_re, s5_c_im, s5_d, s5_w_glu, s5_b_glu, ffn_w_gate_up, ffn_w_down):
    h = x
    for i in range(DEPTH):
        kind, j = i % N_MIXERS, i // N_MIXERS
        hn = rms_norm(h, norm_w[i, 0])
        if kind == 0:
            m = mamba2_mixer(hn, ssd_w_in[j], ssd_conv_w[j], ssd_conv_b[j], ssd_dt_bias[j],
                             ssd_a_log[j], ssd_d[j], ssd_norm_w[j], ssd_w_out[j])
        elif kind == 1:
            m = gdn_mixer(hn, gdn_w_in[j], gdn_conv_w[j], gdn_conv_b[j], gdn_dt_bias[j],
                          gdn_a_log[j], gdn_norm_w[j], gdn_w_out[j])
        else:
            m = s5_mixer(hn, s5_lam_re[j], s5_lam_im[j], s5_log_step[j], s5_b_re[j], s5_b_im[j],
                         s5_c_re[j], s5_c_im[j], s5_d[j], s5_w_glu[j], s5_b_glu[j])
        h = h + rms_norm(m, norm_w[i, 1])
        f = swiglu_ffn(rms_norm(h, norm_w[i, 2]), ffn_w_gate_up[i], ffn_w_down[i])
        h = h + rms_norm(f, norm_w[i, 3])
    return h
```

```python
import functools

import jax
import jax.numpy as jnp
from jax import lax
from jax.experimental import pallas as pl
from jax.experimental.pallas import tpu as pltpu

F32 = jnp.float32
BF16 = jnp.bfloat16
HIGHEST = lax.Precision.HIGHEST

D_MODEL = 1024
NORM_EPS = 1e-6
CONV_TAPS = 5
CONV_CH = 4096
Z_CH = 2048
SMALL_CH = 128
HALO = 8

SSD_HEADS = 32
SSD_HEAD_DIM = 64
SSD_GROUPS = 8
SSD_STATE = 128
SSD_CHUNK = 128
SSD_D_INNER = 2048

GDN_HEAD_DIM = 128
GDN_QK_HEADS = 8
GDN_CHUNK = 64
GDN_VALUE_DIM = 2048

S5_GROUPS = 64
S5_GROUP = 16
S5_STATE = 64
S5_LANES = S5_GROUPS * S5_STATE
S5_KB = 4
S5_TILE = 8

D_FF = 2816

ROW_TILE = 512
GDN_ROWS = 512
S5_ROWS = 256
VMEM_LIMIT_BYTES = 56 * 1024 * 1024

NT_DIMS = (((1,), (1,)), ((), ()))
TN_DIMS = (((0,), (0,)), ((), ()))


def _params(*semantics):
    return pltpu.CompilerParams(dimension_semantics=semantics, vmem_limit_bytes=VMEM_LIMIT_BYTES)


def _resident(shape):
    nd = len(shape)
    return pl.BlockSpec(shape, lambda *_: (0,) * nd, pipeline_mode=pl.Buffered(1))


def _rms(x, w):
    return x * lax.rsqrt(jnp.mean(x * x, axis=-1, keepdims=True) + NORM_EPS) * w


def _silu(x):
    return x * jax.nn.sigmoid(x)


def _dot(a, b):
    return jnp.dot(a, b, preferred_element_type=F32)


def _inproj_kernel(h_ref, hp_ref, hx_ref, nw_ref, wc_ref, wz_ref, ws_ref, cw_ref, cb_ref,
                   act_ref, z_ref, s_ref, *, tm, nblk, cchunk):
    i = pl.program_id(0)
    nw = nw_ref[...]
    cur = _rms(h_ref[...], nw)
    prev = _rms(hp_ref[...], nw) * jnp.where(i > 0, 1.0, 0.0)
    nxt = _rms(hx_ref[...], nw) * jnp.where(i < nblk - 1, 1.0, 0.0)
    curb = cur.astype(BF16)
    ext = jnp.concatenate([prev, cur, nxt], axis=0).astype(BF16)
    z_ref[...] = _dot(curb, wz_ref[...]).astype(z_ref.dtype)
    s_ref[...] = _dot(curb, ws_ref[...])
    pad = CONV_TAPS // 2
    for c in range(CONV_CH // cchunk):
        sl = slice(c * cchunk, (c + 1) * cchunk)
        p = _dot(ext, wc_ref[:, sl])
        acc = cb_ref[:, sl] + cw_ref[0:1, sl] * p[HALO - pad:HALO - pad + tm]
        for k in range(1, CONV_TAPS):
            off = HALO - pad + k
            acc = acc + cw_ref[k:k + 1, sl] * p[off:off + tm]
        act_ref[:, sl] = _silu(acc).astype(act_ref.dtype)


def _inproj(h, nw, wc, wz, ws, conv_w, conv_b):
    L = h.shape[0]
    tm = min(ROW_TILE, L)
    nblk = L // tm
    per = tm // HALO
    last = L // HALO - 1
    kern = functools.partial(_inproj_kernel, tm=tm, nblk=nblk, cchunk=512)
    return pl.pallas_call(
        kern,
        grid=(nblk,),
        in_specs=[
            pl.BlockSpec((tm, D_MODEL), lambda i: (i, 0)),
            pl.BlockSpec((HALO, D_MODEL), lambda i: (jnp.maximum(i * per - 1, 0), 0)),
            pl.BlockSpec((HALO, D_MODEL), lambda i: (jnp.minimum((i + 1) * per, last), 0)),
            _resident((1, D_MODEL)),
            _resident((D_MODEL, CONV_CH)),
            _resident((D_MODEL, Z_CH)),
            _resident((D_MODEL, SMALL_CH)),
            _resident((8, CONV_CH)),
            _resident((1, CONV_CH)),
        ],
        out_specs=[
            pl.BlockSpec((tm, CONV_CH), lambda i: (i, 0)),
            pl.BlockSpec((tm, Z_CH), lambda i: (i, 0)),
            pl.BlockSpec((tm, SMALL_CH), lambda i: (i, 0)),
        ],
        out_shape=[
            jax.ShapeDtypeStruct((L, CONV_CH), BF16),
            jax.ShapeDtypeStruct((L, Z_CH), BF16),
            jax.ShapeDtypeStruct((L, SMALL_CH), F32),
        ],
        compiler_params=_params("arbitrary"),
        name="inproj_conv",
    )(h, h, h, nw, wc, wz, ws, conv_w, conv_b)


def _split_in_weights(w_in, conv_w, conv_b, conv_first):
    if conv_first:
        wc, wz = w_in[:, :CONV_CH], w_in[:, CONV_CH:CONV_CH + Z_CH]
    else:
        wz, wc = w_in[:, :Z_CH], w_in[:, Z_CH:Z_CH + CONV_CH]
    ws = w_in[:, CONV_CH + Z_CH:]
    ws = jnp.pad(ws, ((0, 0), (0, SMALL_CH - ws.shape[1])))
    cw = jnp.pad(conv_w, ((0, 8 - CONV_TAPS), (0, 0)))
    return wc.astype(BF16), wz.astype(BF16), ws.astype(BF16), cw, conv_b.reshape(1, CONV_CH)


def _ssd_core_kernel(xf_ref, xb_ref, dtf_ref, dtb_ref, bias_ref, alog_ref, dskip_ref,
                     yf_ref, yb_ref, sf_ref, sb_ref, *, q):
    @pl.when(pl.program_id(0) == 0)
    def _():
        sf_ref[...] = jnp.zeros_like(sf_ref)
        sb_ref[...] = jnp.zeros_like(sb_ref)

    row = lax.broadcasted_iota(jnp.int32, (q, q), 0)
    col = lax.broadcasted_iota(jnp.int32, (q, q), 1)
    lower = row >= col
    upper = row <= col
    lower_f = lower.astype(F32)
    upper_f = upper.astype(F32)
    a = -jnp.exp(alog_ref[...])
    bias = bias_ref[...]
    p = SSD_HEAD_DIM
    per_group = SSD_HEADS // SSD_GROUPS
    gw = per_group * p
    b_off = SSD_D_INNER
    c_off = SSD_D_INNER + SSD_GROUPS * SSD_STATE

    dirs = ((xf_ref, dtf_ref, yf_ref, sf_ref), (xb_ref, dtb_ref, yb_ref, sb_ref))
    for d, (x_ref, dt_ref, y_ref, s_ref) in enumerate(dirs):
        dt = jax.nn.softplus(dt_ref[...] + bias)
        da = dt * a
        tri_c, tri_r, mask = (lower_f, upper_f, lower) if d == 0 else (upper_f, lower_f, upper)
        cs = jnp.dot(tri_c, da, precision=HIGHEST, preferred_element_type=F32)
        cs_t = jnp.dot(da.T, tri_r, precision=HIGHEST, preferred_element_type=F32)
        tot = cs[q - 1:q, :] if d == 0 else cs[0:1, :]
        e_cs = jnp.exp(cs)
        dec = jnp.exp(tot - cs)
        cdec = jnp.exp(tot)
        for g in range(SSD_GROUPS):
            bmat = x_ref[:, b_off + g * SSD_STATE:b_off + (g + 1) * SSD_STATE]
            cmat = x_ref[:, c_off + g * SSD_STATE:c_off + (g + 1) * SSD_STATE]
            cb = lax.dot_general(cmat, bmat, NT_DIMS, preferred_element_type=F32)
            s_old = s_ref[g]
            yoff = _dot(cmat, s_old.astype(BF16))
            y_parts, xdec_parts, scale_parts = [], [], []
            for r in range(per_group):
                hh = per_group * g + r
                ch = hh + SSD_HEADS * d
                xh = x_ref[:, hh * p:(hh + 1) * p].astype(F32)
                diff = cs[:, ch:ch + 1] - cs_t[ch:ch + 1, :]
                lm = jnp.exp(jnp.where(mask, diff, -jnp.inf))
                m = (cb * lm).astype(BF16)
                xdt = xh * dt[:, ch:ch + 1]
                y = _dot(m, xdt.astype(BF16)) + yoff[:, r * p:(r + 1) * p] * e_cs[:, ch:ch + 1]
                if d == 0:
                    y = y + xh * dskip_ref[:, hh * p:(hh + 1) * p]
                y_parts.append(y)
                xdec_parts.append(xdt * dec[:, ch:ch + 1])
                scale_parts.append(jnp.broadcast_to(cdec[:, ch:ch + 1], (1, p)))
            y_ref[:, g * gw:(g + 1) * gw] = jnp.concatenate(y_parts, axis=1).astype(y_ref.dtype)
            xdec = jnp.concatenate(xdec_parts, axis=1).astype(BF16)
            new = lax.dot_general(bmat, xdec, TN_DIMS, preferred_element_type=F32)
            s_ref[g] = s_old * jnp.concatenate(scale_parts, axis=1) + new


def _ssd_core(act, small, dt_bias, a_log, d_skip):
    L = act.shape[0]
    q = SSD_CHUNK
    nc = L // q
    pad = SMALL_CH - 2 * SSD_HEADS
    bias = jnp.pad(dt_bias.reshape(1, 2 * SSD_HEADS), ((0, 0), (0, pad)))
    alog = jnp.pad(a_log.reshape(1, 2 * SSD_HEADS), ((0, 0), (0, pad)))
    dskip = jnp.repeat(d_skip, SSD_HEAD_DIM).reshape(1, SSD_D_INNER)
    gw = SSD_D_INNER // SSD_GROUPS
    fwd = lambda c: (c, 0)
    bwd = lambda c: (nc - 1 - c, 0)
    return pl.pallas_call(
        functools.partial(_ssd_core_kernel, q=q),
        grid=(nc,),
        in_specs=[
            pl.BlockSpec((q, CONV_CH), fwd),
            pl.BlockSpec((q, CONV_CH), bwd),
            pl.BlockSpec((q, SMALL_CH), fwd),
            pl.BlockSpec((q, SMALL_CH), bwd),
            _resident((1, SMALL_CH)),
            _resident((1, SMALL_CH)),
            _resident((1, SSD_D_INNER)),
        ],
        out_specs=[pl.BlockSpec((q, SSD_D_INNER), fwd), pl.BlockSpec((q, SSD_D_INNER), bwd)],
        out_shape=[jax.ShapeDtypeStruct((L, SSD_D_INNER), BF16)] * 2,
        scratch_shapes=[pltpu.VMEM((SSD_GROUPS, SSD_STATE, gw), F32)] * 2,
        compiler_params=_params("arbitrary"),
        name="ssd_core",
    )(act, act, small, small, bias, alog, dskip)


def _ssd_out_kernel(yf_ref, yb_ref, z_ref, h_ref, gnw_ref, wo_ref, pnw_ref, o_ref):
    y = yf_ref[...].astype(F32) + yb_ref[...].astype(F32)
    yz = y * _silu(z_ref[...].astype(F32))
    gw = SSD_D_INNER // SSD_GROUPS
    parts = []
    for g in range(SSD_GROUPS):
        blk = yz[:, g * gw:(g + 1) * gw]
        parts.append(blk * lax.rsqrt(jnp.mean(blk * blk, axis=-1, keepdims=True) + NORM_EPS))
    yn = (jnp.concatenate(parts, axis=1) * gnw_ref[...]).astype(BF16)
    m = _dot(yn, wo_ref[...])
    o_ref[...] = h_ref[...] + _rms(m, pnw_ref[...])


def _mixer_out_call(kern, name, h, streams, consts):
    L = h.shape[0]
    tm = min(ROW_TILE, L)
    tiled = lambda a: pl.BlockSpec((tm, a.shape[1]), lambda i: (i, 0))
    return pl.pallas_call(
        kern,
        grid=(L // tm,),
        in_specs=[tiled(a) for a in streams] + [tiled(h)] + [_resident(c.shape) for c in consts],
        out_specs=tiled(h),
        out_shape=jax.ShapeDtypeStruct(h.shape, F32),
        compiler_params=_params("arbitrary"),
        name=name,
    )(*streams, h, *consts)


def _ssd_layer(h, nw_pre, nw_post, w_in, conv_w, conv_b, dt_bias, a_log, d_skip, norm_w, w_out):
    wc, wz, ws, cw, cb = _split_in_weights(w_in, conv_w, conv_b, conv_first=False)
    act, z, small = _inproj(h, nw_pre.reshape(1, -1), wc, wz, ws, cw, cb)
    yf, yb = _ssd_core(act, small, dt_bias, a_log, d_skip)
    consts = (norm_w.reshape(1, -1), w_out.astype(BF16), nw_post.reshape(1, -1))
    return _mixer_out_call(_ssd_out_kernel, "ssd_out", h, (yf, yb, z), consts)


def _gdn_core_kernel(qf_ref, kf_ref, vf_ref, qb_ref, kb_ref, vb_ref,
                     gcf_ref, grf_ref, gcb_ref, grb_ref, pc_ref, pr_ref,
                     of_ref, ob_ref, s_ref, *, rb, ck):
    @pl.when(pl.program_id(1) == 0)
    def _():
        s_ref[...] = jnp.zeros_like(s_ref)

    n = rb // ck
    hd = GDN_HEAD_DIM
    row = lax.broadcasted_iota(jnp.int32, (ck, ck), 0)
    col = lax.broadcasted_iota(jnp.int32, (ck, ck), 1)
    lower, upper = row >= col, row <= col
    lower_s, upper_s = row > col, row < col
    lower_f, upper_f = lower.astype(F32), upper.astype(F32)
    eye = (row == col).astype(F32)
    alog_c, bias_c = pc_ref[0, 0:1, :], pc_ref[0, 1:2, :]
    alog_r, bias_r = pr_ref[0, :, 0:1], pr_ref[0, :, 1:2]
    is_a_c = lax.broadcasted_iota(jnp.int32, (ck, 8), 1) < 4
    is_a_r = lax.broadcasted_iota(jnp.int32, (8, ck), 0) < 4

    def gates(raw, alog, bias, is_a):
        return jnp.where(is_a, -jnp.exp(alog) * jax.nn.softplus(raw + bias), jax.nn.sigmoid(raw))

    def chunk(ci, carry):
        for d in range(2):
            q_ref, k_ref, v_ref, gc_ref, gr_ref, o_ref = (
                (qf_ref, kf_ref, vf_ref, gcf_ref, grf_ref, of_ref) if d == 0 else
                (qb_ref, kb_ref, vb_ref, gcb_ref, grb_ref, ob_ref))
            cc = ci if d == 0 else n - 1 - ci
            rows = pl.ds(pl.multiple_of(cc * ck, ck), ck)
            qx = q_ref[rows, :].astype(F32)
            kx = k_ref[rows, :].astype(F32)
            qn = qx * lax.rsqrt(jnp.sum(qx * qx, axis=-1, keepdims=True) + NORM_EPS) * (hd ** -0.5)
            kn = kx * lax.rsqrt(jnp.sum(kx * kx, axis=-1, keepdims=True) + NORM_EPS)
            knb = kn.astype(BF16)
            gram = lax.dot_general(jnp.concatenate([kn, qn], axis=0).astype(BF16), knb, NT_DIMS,
                                   preferred_element_type=F32)
            kk, qk = gram[:ck], gram[ck:]
            gb_c = gates(gc_ref[0, rows, :], alog_c, bias_c, is_a_c)
            gb_r = gates(gr_ref[0, cc], alog_r, bias_r, is_a_r)
            tri_c, tri_r, mask, strict = ((lower_f, upper_f, lower, lower_s) if d == 0 else
                                          (upper_f, lower_f, upper, upper_s))
            cs_c = jnp.dot(tri_c, gb_c, precision=HIGHEST, preferred_element_type=F32)
            cs_r = jnp.dot(gb_r, tri_r, precision=HIGHEST, preferred_element_type=F32)
            tot_all = cs_c[ck - 1:ck, :] if d == 0 else cs_c[0:1, :]
            for j in range(2):
                ic = 2 * d + j
                gcs_c, gcs_r = cs_c[:, ic:ic + 1], cs_r[ic:ic + 1, :]
                beta_c = gb_c[:, 4 + ic:5 + ic]
                decay = jnp.exp(jnp.where(mask, gcs_c - gcs_r, -jnp.inf))
                a_mat = jnp.where(strict, beta_c * kk * decay, 0.0)
                pw = -a_mat
                t_inv = eye + pw
                for _ in range(5):
                    pwb = pw.astype(BF16)
                    pw = _dot(pwb, pwb)
                    t_inv = t_inv + _dot(t_inv.astype(BF16), pw.astype(BF16))
                e_gcs = jnp.exp(gcs_c)
                vx = v_ref[rows, j * hd:(j + 1) * hd].astype(F32)
                rhs = jnp.concatenate([vx * beta_c, kn * (beta_c * e_gcs)], axis=1).astype(BF16)
                sol = _dot(t_inv.astype(BF16), rhs)
                u, w = sol[:, :hd], sol[:, hd:]
                s_old = s_ref[d, j]
                wq = jnp.concatenate([w, qn * e_gcs], axis=0).astype(BF16)
                both = _dot(wq, s_old.astype(BF16))
                v_new = u - both[:ck]
                v_newb = v_new.astype(BF16)
                o = both[ck:] + _dot((qk * decay).astype(BF16), v_newb)
                o_ref[rows, j * hd:(j + 1) * hd] = o.astype(o_ref.dtype)
                tot = tot_all[:, ic:ic + 1]
                k_dec = (kn * jnp.exp(tot - gcs_c)).astype(BF16)
                s_ref[d, j] = s_old * jnp.exp(tot) + lax.dot_general(
                    k_dec, v_newb, TN_DIMS, preferred_element_type=F32)
        return carry

    lax.fori_loop(0, n, chunk, 0)


def _gdn_core(act, small, dt_bias, a_log):
    L = act.shape[0]
    ck = GDN_CHUNK
    rb = min(GDN_ROWS, L)
    nb = L // rb
    nq = GDN_QK_HEADS
    hd = GDN_HEAD_DIM
    gates = small[:, :64].reshape(L, 4, nq, 2)
    g_col = gates.transpose(2, 0, 1, 3).reshape(nq, L, 8)
    g_row = gates.reshape(L // ck, ck, 4, nq, 2).transpose(3, 0, 2, 4, 1).reshape(nq, L // ck, 8, ck)
    zeros = jnp.zeros((nq, 4), F32)
    per_head = lambda t: jnp.concatenate([t.reshape(2, nq, 2).transpose(1, 0, 2).reshape(nq, 4), zeros], 1)
    p_col = jnp.stack([per_head(a_log), per_head(dt_bias)], axis=1)
    p_row = p_col.transpose(0, 2, 1)
    kcol = GDN_QK_HEADS
    vcol = 2 * GDN_QK_HEADS * hd // (2 * hd)
    fwd_q = lambda hq, b: (b, hq)
    fwd_k = lambda hq, b: (b, kcol + hq)
    fwd_v = lambda hq, b: (b, vcol + hq)
    bwd_q = lambda hq, b: (nb - 1 - b, hq)
    bwd_k = lambda hq, b: (nb - 1 - b, kcol + hq)
    bwd_v = lambda hq, b: (nb - 1 - b, vcol + hq)
    n = rb // ck
    return pl.pallas_call(
        functools.partial(_gdn_core_kernel, rb=rb, ck=ck),
        grid=(nq, nb),
        in_specs=[
            pl.BlockSpec((rb, hd), fwd_q),
            pl.BlockSpec((rb, hd), fwd_k),
            pl.BlockSpec((rb, 2 * hd), fwd_v),
            pl.BlockSpec((rb, hd), bwd_q),
            pl.BlockSpec((rb, hd), bwd_k),
            pl.BlockSpec((rb, 2 * hd), bwd_v),
            pl.BlockSpec((1, rb, 8), lambda hq, b: (hq, b, 0)),
            pl.BlockSpec((1, n, 8, ck), lambda hq, b: (hq, b, 0, 0)),
            pl.BlockSpec((1, rb, 8), lambda hq, b: (hq, nb - 1 - b, 0)),
            pl.BlockSpec((1, n, 8, ck), lambda hq, b: (hq, nb - 1 - b, 0, 0)),
            pl.BlockSpec((1, 2, 8), lambda hq, b: (hq, 0, 0)),
            pl.BlockSpec((1, 8, 2), lambda hq, b: (hq, 0, 0)),
        ],
        out_specs=[
            pl.BlockSpec((rb, 2 * hd), lambda hq, b: (b, hq)),
            pl.BlockSpec((rb, 2 * hd), lambda hq, b: (nb - 1 - b, hq)),
        ],
        out_shape=[jax.ShapeDtypeStruct((L, GDN_VALUE_DIM), BF16)] * 2,
        scratch_shapes=[pltpu.VMEM((2, 2, hd, hd), F32)],
        compiler_params=_params("arbitrary", "arbitrary"),
        name="gdn_core",
    )(act, act, act, act, act, act, g_col, g_row, g_col, g_row, p_col, p_row)


def _gdn_out_kernel(of_ref, ob_ref, z_ref, h_ref, hnw_ref, wo_ref, pnw_ref, o_ref):
    o = of_ref[...].astype(F32) + ob_ref[...].astype(F32)
    z = z_ref[...].astype(F32)
    hd = GDN_HEAD_DIM
    hnw = hnw_ref[...]
    parts = []
    for v in range(GDN_VALUE_DIM // hd):
        blk = o[:, v * hd:(v + 1) * hd]
        parts.append(_rms(blk, hnw) * _silu(z[:, v * hd:(v + 1) * hd]))
    m = _dot(jnp.concatenate(parts, axis=1).astype(BF16), wo_ref[...])
    o_ref[...] = h_ref[...] + _rms(m, pnw_ref[...])


def _gdn_layer(h, nw_pre, nw_post, w_in, conv_w, conv_b, dt_bias, a_log, norm_w, w_out):
    wc, wz, ws, cw, cb = _split_in_weights(w_in, conv_w, conv_b, conv_first=True)
    act, z, small = _inproj(h, nw_pre.reshape(1, -1), wc, wz, ws, cw, cb)
    of, ob = _gdn_core(act, small, dt_bias, a_log)
    consts = (norm_w.reshape(1, -1), w_out.astype(BF16), nw_post.reshape(1, -1))
    return _mixer_out_call(_gdn_out_kernel, "gdn_out", h, (of, ob, z), consts)


def _s5_prep_kernel(lre_ref, lim_ref, lst_ref, bre_ref, bim_ref, lre4_ref, lim4_ref, lst4_ref,
                    bbr_ref, bbi_ref, pw_ref):
    lre, lim = lre_ref[...], lim_ref[...]
    step = jnp.exp(lst_ref[...])
    mag, ang = jnp.exp(lre * step), lim * step
    lbr, lbi = mag * jnp.cos(ang), mag * jnp.sin(ang)
    inv_den = 1.0 / (lre * lre + lim * lim)
    ar, ai, br, bi = lbr - 1.0, lbi, lre * inv_den, -lim * inv_den
    zr, zi = ar * br - ai * bi, ar * bi + ai * br
    b_re, b_im = bre_ref[...], bim_ref[...]
    bbr_ref[...] = zr * b_re - zi * b_im
    bbi_ref[...] = zr * b_im + zi * b_re
    r = lax.broadcasted_iota(jnp.int32, (S5_TILE, S5_LANES), 0).astype(F32)
    for d in range(2):
        k = r + 1.0 if d == 0 else float(S5_TILE) - r
        st = jnp.exp(lst4_ref[d])
        m, a = jnp.exp(k * (lre4_ref[d] * st)), k * (lim4_ref[d] * st)
        pw_ref[d, 0] = m * jnp.cos(a)
        pw_ref[d, 1] = m * jnp.sin(a)


def _s5_prep(lam_re, lam_im, log_step, b_re, b_im):
    g, n, c = S5_GROUPS, S5_STATE, S5_GROUP
    expand = lambda t: jnp.repeat(t, c, axis=-1)
    lst = jnp.broadcast_to(log_step[:, :, None], (2, g, n))
    flat = lambda t: t.reshape(2, 1, g * n)
    full = lambda s: pl.BlockSpec(s, lambda: (0,) * len(s))
    shapes = [(2, g, n * c)] * 3 + [(1, g, n * c)] * 2 + [(2, 1, g * n)] * 3
    return pl.pallas_call(
        _s5_prep_kernel,
        in_specs=[full(s) for s in shapes],
        out_specs=[full((2, g, n * c)), full((2, g, n * c)), full((2, 2, S5_TILE, g * n))],
        out_shape=[
            jax.ShapeDtypeStruct((2, g, n * c), F32),
            jax.ShapeDtypeStruct((2, g, n * c), F32),
            jax.ShapeDtypeStruct((2, 2, S5_TILE, g * n), F32),
        ],
        name="s5_prep",
    )(expand(lam_re), expand(lam_im), expand(lst), b_re.reshape(1, g, n * c), b_im.reshape(1, g, n * c),
      flat(lam_re), flat(lam_im), flat(lst))


def _s5_core_kernel(hf_ref, hb_ref, nw_ref, win_ref, wout_ref, pw_ref, yf_ref, yb_ref,
                    xr_ref, xi_ref, car_ref, *, rb):
    @pl.when(pl.program_id(0) == 0)
    def _():
        car_ref[...] = jnp.zeros_like(car_ref)

    nw = nw_ref[...]
    nt = rb // S5_TILE
    w = S5_LANES // S5_KB
    cin = D_MODEL // S5_KB
    rowi = lax.broadcasted_iota(jnp.int32, (S5_TILE, w), 0)
    tile = lambda v: jnp.broadcast_to(v, (S5_TILE, w))
    for d, (h_ref, y_ref) in enumerate(((hf_ref, yf_ref), (hb_ref, yb_ref))):
        hn = _rms(h_ref[...], nw).astype(BF16)
        for kb in range(S5_KB):
            x = _dot(hn[:, kb * cin:(kb + 1) * cin], win_ref[d, kb])
            xr_ref[...] = x[:, :w]
            xi_ref[...] = x[:, w:]
            p_r = pw_ref[d, 0, :, kb * w:(kb + 1) * w]
            p_i = pw_ref[d, 1, :, kb * w:(kb + 1) * w]
            consts = []
            for k in (1, 2, 4):
                src = k - 1 if d == 0 else S5_TILE - k
                valid = rowi >= k if d == 0 else rowi <= S5_TILE - 1 - k
                consts.append((jnp.where(valid, tile(p_r[src:src + 1]), 0.0),
                               jnp.where(valid, tile(p_i[src:src + 1]), 0.0)))
            last = S5_TILE - 1 if d == 0 else 0

            def body(t, carry, d=d, consts=consts, p_r=p_r, p_i=p_i, last=last):
                cr, ci = carry
                tt = t if d == 0 else nt - 1 - t
                rows = pl.ds(pl.multiple_of(tt * S5_TILE, S5_TILE), S5_TILE)
                sr, si = xr_ref[rows, :], xi_ref[rows, :]
                for k, (a_r, a_i) in zip((1, 2, 4), consts):
                    sh = k if d == 0 else S5_TILE - k
                    tr, ti = pltpu.roll(sr, sh, 0), pltpu.roll(si, sh, 0)
                    sr, si = sr + a_r * tr - a_i * ti, si + a_r * ti + a_i * tr
                sr, si = sr + p_r * cr - p_i * ci, si + p_r * ci + p_i * cr
                xr_ref[rows, :] = sr
                xi_ref[rows, :] = si
                return tile(sr[last:last + 1]), tile(si[last:last + 1])

            cr, ci = lax.fori_loop(0, nt, body, (car_ref[d, kb, 0], car_ref[d, kb, 1]))
            car_ref[d, kb, 0] = cr
            car_ref[d, kb, 1] = ci
            s_cat = jnp.concatenate([xr_ref[...], xi_ref[...]], axis=1).astype(BF16)
            y_ref[:, kb * cin:(kb + 1) * cin] = _dot(s_cat, wout_ref[d, kb]).astype(y_ref.dtype)


def _s5_block_diag(t):
    per = S5_GROUPS // S5_KB
    a, b = t.shape[2], t.shape[3]
    t = t.reshape(2, S5_KB, per, a, b)
    eye = jnp.eye(per, dtype=t.dtype)
    return jnp.einsum('dkgab,gh->dkgahb', t, eye).reshape(2, S5_KB, per * a, per * b)


def _s5_core(h, nw, bb_r, bb_i, c_re, c_im, pw):
    L = h.shape[0]
    rb = min(S5_ROWS, L)
    nb = L // rb
    g, n, c = S5_GROUPS, S5_STATE, S5_GROUP
    to_in = lambda t: _s5_block_diag(t.reshape(2, g, n, c).transpose(0, 1, 3, 2))
    w_in = jnp.concatenate([to_in(bb_r), to_in(bb_i)], axis=-1).astype(BF16)
    to_out = lambda t: _s5_block_diag(t.transpose(0, 1, 3, 2))
    w_out = jnp.concatenate([to_out(c_re), to_out(-c_im)], axis=-2).astype(BF16)
    w = S5_LANES // S5_KB
    return pl.pallas_call(
        functools.partial(_s5_core_kernel, rb=rb),
        grid=(nb,),
        in_specs=[
            pl.BlockSpec((rb, D_MODEL), lambda b: (b, 0)),
            pl.BlockSpec((rb, D_MODEL), lambda b: (nb - 1 - b, 0)),
            _resident((1, D_MODEL)),
            _resident(w_in.shape),
            _resident(w_out.shape),
            _resident(pw.shape),
        ],
        out_specs=[pl.BlockSpec((rb, D_MODEL), lambda b: (b, 0)),
                   pl.BlockSpec((rb, D_MODEL), lambda b: (nb - 1 - b, 0))],
        out_shape=[jax.ShapeDtypeStruct((L, D_MODEL), BF16)] * 2,
        scratch_shapes=[pltpu.VMEM((rb, w), F32), pltpu.VMEM((rb, w), F32),
                        pltpu.VMEM((2, S5_KB, 2, S5_TILE, w), F32)],
        compiler_params=_params("arbitrary"),
        name="s5_core",
    )(h, h, nw, w_in, w_out, pw)


def _s5_out_kernel(yf_ref, yb_ref, h_ref, nw_ref, dsk_ref, wg_ref, bg_ref, pnw_ref, o_ref):
    h = h_ref[...]
    hn = _rms(h, nw_ref[...])
    y = yf_ref[...].astype(F32) + yb_ref[...].astype(F32) + dsk_ref[...] * hn
    y = jax.nn.gelu(y)
    vg = _dot(y.astype(BF16), wg_ref[...]) + bg_ref[...]
    m = vg[:, :D_MODEL] * jax.nn.sigmoid(vg[:, D_MODEL:])
    o_ref[...] = h + _rms(m, pnw_ref[...])


def _s5_layer(h, nw_pre, nw_post, lam_re, lam_im, log_step, b_re, b_im, c_re, c_im, d_skip, w_glu, b_glu):
    nw = nw_pre.reshape(1, -1)
    bb_r, bb_i, pw = _s5_prep(lam_re, lam_im, log_step, b_re, b_im)
    yf, yb = _s5_core(h, nw, bb_r, bb_i, c_re, c_im, pw)
    consts = (nw, d_skip.reshape(1, -1), w_glu.astype(BF16), b_glu.reshape(1, -1), nw_post.reshape(1, -1))
    return _mixer_out_call(_s5_out_kernel, "s5_out", h, (yf, yb), consts)


def _ffn_kernel(h_ref, nw_ref, wgu_ref, wd_ref, pnw_ref, o_ref, act_ref, *, fchunk):
    h = h_ref[...]
    hn = _rms(h, nw_ref[...]).astype(BF16)
    for c in range(D_FF // fchunk):
        gate = _dot(hn, wgu_ref[:, c * fchunk:(c + 1) * fchunk])
        up = _dot(hn, wgu_ref[:, D_FF + c * fchunk:D_FF + (c + 1) * fchunk])
        act_ref[:, c * fchunk:(c + 1) * fchunk] = (_silu(gate) * up).astype(BF16)
    f = _dot(act_ref[...], wd_ref[...])
    o_ref[...] = h + _rms(f, pnw_ref[...])


def _ffn(h, nw_pre, nw_post, w_gate_up, w_down):
    L = h.shape[0]
    tm = min(ROW_TILE, L)
    tiled = pl.BlockSpec((tm, D_MODEL), lambda i: (i, 0))
    return pl.pallas_call(
        functools.partial(_ffn_kernel, fchunk=256),
        grid=(L // tm,),
        in_specs=[tiled, _resident((1, D_MODEL)), _resident((D_MODEL, 2 * D_FF)),
                  _resident((D_FF, D_MODEL)), _resident((1, D_MODEL))],
        out_specs=tiled,
        out_shape=jax.ShapeDtypeStruct(h.shape, F32),
        scratch_shapes=[pltpu.VMEM((tm, D_FF), BF16)],
        compiler_params=_params("arbitrary"),
        name="ffn",
    )(h, nw_pre.reshape(1, -1), w_gate_up.astype(BF16), w_down.astype(BF16), nw_post.reshape(1, -1))


def kernel(x, norm_w, ssd_w_in, ssd_conv_w, ssd_conv_b, ssd_dt_bias, ssd_a_log, ssd_d, ssd_norm_w, ssd_w_out, gdn_w_in, gdn_conv_w, gdn_conv_b, gdn_dt_bias, gdn_a_log, gdn_norm_w, gdn_w_out, s5_lam_re, s5_lam_im, s5_log_step, s5_b_re, s5_b_im, s5_c_re, s5_c_im, s5_d, s5_w_glu, s5_b_glu, ffn_w_gate_up, ffn_w_down):
    bt, L, d = x.shape
    assert bt == 1 and d == D_MODEL
    h = x.reshape(L, d)
    for i in range(norm_w.shape[0]):
        kind, j = i % 3, i // 3
        if kind == 0:
            h = _ssd_layer(h, norm_w[i, 0], norm_w[i, 1], ssd_w_in[j], ssd_conv_w[j], ssd_conv_b[j],
                           ssd_dt_bias[j], ssd_a_log[j], ssd_d[j], ssd_norm_w[j], ssd_w_out[j])
        elif kind == 1:
            h = _gdn_layer(h, norm_w[i, 0], norm_w[i, 1], gdn_w_in[j], gdn_conv_w[j], gdn_conv_b[j],
                           gdn_dt_bias[j], gdn_a_log[j], gdn_norm_w[j], gdn_w_out[j])
        else:
            h = _s5_layer(h, norm_w[i, 0], norm_w[i, 1], s5_lam_re[j], s5_lam_im[j], s5_log_step[j],
                          s5_b_re[j], s5_b_im[j], s5_c_re[j], s5_c_im[j], s5_d[j], s5_w_glu[j], s5_b_glu[j])
        h = _ffn(h, norm_w[i, 2], norm_w[i, 3], ffn_w_gate_up[i], ffn_w_down[i])
    return h.reshape(bt, L, d)
```

```python
import functools

import jax
import jax.numpy as jnp
from jax import lax
from jax.experimental import pallas as pl
from jax.experimental.pallas import tpu as pltpu

F32 = jnp.float32
BF16 = jnp.bfloat16
HIGHEST = lax.Precision.HIGHEST

D_MODEL = 1024
NORM_EPS = 1e-6
CONV_TAPS = 5
CONV_CH = 4096
Z_CH = 2048
SMALL_CH = 128
HALO = 8

SSD_HEADS = 32
SSD_HEAD_DIM = 64
SSD_GROUPS = 8
SSD_STATE = 128
SSD_CHUNK = 128
SSD_D_INNER = 2048

GDN_HEAD_DIM = 128
GDN_QK_HEADS = 8
GDN_V_HEADS = 16
GDN_CHUNK = 64
GDN_VALUE_DIM = 2048

S5_GROUPS = 64
S5_GROUP = 16
S5_STATE = 64
S5_LANES = S5_GROUPS * S5_STATE
S5_KB = 4
S5_TILE = 8

D_FF = 2816

ROW_TILE = 512
GDN_ROWS = 256
S5_ROWS = 256
VMEM_LIMIT_BYTES = 56 * 1024 * 1024

NT_DIMS = (((1,), (1,)), ((), ()))
TN_DIMS = (((0,), (0,)), ((), ()))


def _params(*semantics):
    return pltpu.CompilerParams(dimension_semantics=semantics, vmem_limit_bytes=VMEM_LIMIT_BYTES)


def _resident(shape):
    nd = len(shape)
    return pl.BlockSpec(shape, lambda *_: (0,) * nd, pipeline_mode=pl.Buffered(1))


def _rms(x, w):
    return x * lax.rsqrt(jnp.mean(x * x, axis=-1, keepdims=True) + NORM_EPS) * w


def _silu(x):
    return x * jax.nn.sigmoid(x)


def _dot(a, b):
    return jnp.dot(a, b, preferred_element_type=F32)


def _inproj_kernel(h_ref, hp_ref, hx_ref, nw_ref, wc_ref, wz_ref, ws_ref, cw_ref, cb_ref,
                   act_ref, z_ref, s_ref, *, tm, nblk, cchunk):
    i = pl.program_id(0)
    nw = nw_ref[...]
    cur = _rms(h_ref[...], nw)
    prev = _rms(hp_ref[...], nw) * jnp.where(i > 0, 1.0, 0.0)
    nxt = _rms(hx_ref[...], nw) * jnp.where(i < nblk - 1, 1.0, 0.0)
    curb = cur.astype(BF16)
    ext = jnp.concatenate([prev, cur, nxt], axis=0).astype(BF16)
    z_ref[...] = _dot(curb, wz_ref[...]).astype(z_ref.dtype)
    s_ref[...] = _dot(curb, ws_ref[...])
    pad = CONV_TAPS // 2
    for c in range(CONV_CH // cchunk):
        sl = slice(c * cchunk, (c + 1) * cchunk)
        p = _dot(ext, wc_ref[:, sl])
        acc = cb_ref[:, sl] + cw_ref[0:1, sl] * p[HALO - pad:HALO - pad + tm]
        for k in range(1, CONV_TAPS):
            off = HALO - pad + k
            acc = acc + cw_ref[k:k + 1, sl] * p[off:off + tm]
        act_ref[:, sl] = _silu(acc).astype(act_ref.dtype)


def _inproj(h, nw, wc, wz, ws, conv_w, conv_b):
    L = h.shape[0]
    tm = min(ROW_TILE, L)
    nblk = L // tm
    per = tm // HALO
    last = L // HALO - 1
    kern = functools.partial(_inproj_kernel, tm=tm, nblk=nblk, cchunk=512)
    return pl.pallas_call(
        kern,
        grid=(nblk,),
        in_specs=[
            pl.BlockSpec((tm, D_MODEL), lambda i: (i, 0)),
            pl.BlockSpec((HALO, D_MODEL), lambda i: (jnp.maximum(i * per - 1, 0), 0)),
            pl.BlockSpec((HALO, D_MODEL), lambda i: (jnp.minimum((i + 1) * per, last), 0)),
            _resident((1, D_MODEL)),
            _resident((D_MODEL, CONV_CH)),
            _resident((D_MODEL, Z_CH)),
            _resident((D_MODEL, SMALL_CH)),
            _resident((8, CONV_CH)),
            _resident((1, CONV_CH)),
        ],
        out_specs=[
            pl.BlockSpec((tm, CONV_CH), lambda i: (i, 0)),
            pl.BlockSpec((tm, Z_CH), lambda i: (i, 0)),
            pl.BlockSpec((tm, SMALL_CH), lambda i: (i, 0)),
        ],
        out_shape=[
            jax.ShapeDtypeStruct((L, CONV_CH), BF16),
            jax.ShapeDtypeStruct((L, Z_CH), BF16),
            jax.ShapeDtypeStruct((L, SMALL_CH), F32),
        ],
        compiler_params=_params("arbitrary"),
        name="inproj_conv",
    )(h, h, h, nw, wc, wz, ws, conv_w, conv_b)


def _split_in_weights(w_in, conv_w, conv_b, conv_first):
    if conv_first:
        wc, wz = w_in[:, :CONV_CH], w_in[:, CONV_CH:CONV_CH + Z_CH]
    else:
        wz, wc = w_in[:, :Z_CH], w_in[:, Z_CH:Z_CH + CONV_CH]
    ws = w_in[:, CONV_CH + Z_CH:]
    ws = jnp.pad(ws, ((0, 0), (0, SMALL_CH - ws.shape[1])))
    cw = jnp.pad(conv_w, ((0, 8 - CONV_TAPS), (0, 0)))
    return wc.astype(BF16), wz.astype(BF16), ws.astype(BF16), cw, conv_b.reshape(1, CONV_CH)


def _ssd_core_kernel(xf_ref, xb_ref, dtf_ref, dtb_ref, bias_ref, alog_ref, dskip_ref, exb_ref, exf_ref,
                     yf_ref, yb_ref, sf_ref, sb_ref, *, q):
    @pl.when(pl.program_id(0) == 0)
    def _():
        sf_ref[...] = jnp.zeros_like(sf_ref)
        sb_ref[...] = jnp.zeros_like(sb_ref)

    row = lax.broadcasted_iota(jnp.int32, (q, q), 0)
    col = lax.broadcasted_iota(jnp.int32, (q, q), 1)
    lower = row >= col
    upper = row <= col
    lower_f = lower.astype(F32)
    upper_f = upper.astype(F32)
    a = -jnp.exp(alog_ref[...])
    bias = bias_ref[...]
    p = SSD_HEAD_DIM
    per_group = SSD_HEADS // SSD_GROUPS
    gw = per_group * p
    pw = 2 * p
    b_off = SSD_D_INNER
    c_off = SSD_D_INNER + SSD_GROUPS * SSD_STATE
    left = lax.broadcasted_iota(jnp.int32, (q, pw), 1) < p

    dirs = ((xf_ref, dtf_ref, yf_ref, sf_ref), (xb_ref, dtb_ref, yb_ref, sb_ref))
    pre = []
    for d, (x_ref, dt_ref, y_ref, s_ref) in enumerate(dirs):
        dt = jax.nn.softplus(dt_ref[...] + bias)
        da = dt * a
        tri_c, tri_r = (lower_f, upper_f) if d == 0 else (upper_f, lower_f)
        cs = jnp.dot(tri_c, da, precision=HIGHEST, preferred_element_type=F32)
        cs_t = jnp.dot(da.T, tri_r, precision=HIGHEST, preferred_element_type=F32)
        tot = cs[q - 1:q, :] if d == 0 else cs[0:1, :]
        exb = exb_ref[d]
        dt_x = _dot(dt.astype(BF16), exb)
        e_x = _dot(jnp.exp(cs).astype(BF16), exb)
        w_x = _dot((dt * jnp.exp(tot - cs)).astype(BF16), exb)
        tot_x = jnp.dot(jnp.broadcast_to(tot, (8, SMALL_CH)), exf_ref[d], precision=HIGHEST,
                        preferred_element_type=F32)[0:1]
        pre.append((cs, cs_t, dt_x, e_x, w_x, jnp.exp(tot_x)))

    jobs = [(d, g) for d in range(2) for g in range(SSD_GROUPS)]
    bmats = [dirs[d][0][:, b_off + g * SSD_STATE:b_off + (g + 1) * SSD_STATE] for d, g in jobs]
    cmats = [dirs[d][0][:, c_off + g * SSD_STATE:c_off + (g + 1) * SSD_STATE] for d, g in jobs]
    cbs = [lax.dot_general(c, b, NT_DIMS, preferred_element_type=F32) for c, b in zip(cmats, bmats)]
    yoffs = [_dot(c, dirs[d][3][g].astype(BF16)) for (d, g), c in zip(jobs, cmats)]
    xgs = [dirs[d][0][:, g * gw:(g + 1) * gw].astype(F32) for d, g in jobs]

    for (d, g), cb, yoff, xg in zip(jobs, cbs, yoffs, xgs):
        cs, cs_t, dt_x, e_x, _, _ = pre[d]
        mask = lower if d == 0 else upper
        y_ref = dirs[d][2]
        for pr in range(per_group // 2):
            h0 = per_group * g + 2 * pr
            lms = []
            for hh in (h0, h0 + 1):
                ch = hh + SSD_HEADS * d
                lms.append(cb * jnp.exp(jnp.where(mask, cs[:, ch:ch + 1] - cs_t[ch:ch + 1, :], -jnp.inf)))
            m_pair = jnp.concatenate(lms, axis=1).astype(BF16)
            sl = slice(h0 * p, h0 * p + pw)
            lo = slice(2 * pr * p, 2 * pr * p + pw)
            xp = xg[:, lo]
            xdt = xp * dt_x[:, sl]
            rhs = jnp.concatenate([jnp.where(left, xdt, 0.0), jnp.where(left, 0.0, xdt)], axis=0).astype(BF16)
            y = _dot(m_pair, rhs) + yoff[:, lo] * e_x[:, sl]
            if d == 0:
                y = y + xp * dskip_ref[:, sl]
            y_ref[:, sl] = y.astype(y_ref.dtype)

    for (d, g), bmat, xg in zip(jobs, bmats, xgs):
        _, _, _, _, w_x, cdec_x = pre[d]
        s_ref = dirs[d][3]
        gs = slice(g * gw, (g + 1) * gw)
        new = lax.dot_general(bmat, (xg * w_x[:, gs]).astype(BF16), TN_DIMS, preferred_element_type=F32)
        s_ref[g] = s_ref[g] * cdec_x[:, gs] + new


def _ssd_core(act, small, dt_bias, a_log, d_skip):
    L = act.shape[0]
    q = SSD_CHUNK
    nc = L // q
    pad = SMALL_CH - 2 * SSD_HEADS
    bias = jnp.pad(dt_bias.reshape(1, 2 * SSD_HEADS), ((0, 0), (0, pad)))
    alog = jnp.pad(a_log.reshape(1, 2 * SSD_HEADS), ((0, 0), (0, pad)))
    dskip = jnp.repeat(d_skip, SSD_HEAD_DIM).reshape(1, SSD_D_INNER)
    head_of_channel = jnp.arange(SSD_D_INNER) // SSD_HEAD_DIM
    col_id = jnp.arange(SMALL_CH)
    expand = jnp.stack([col_id[:, None] == head_of_channel[None, :] + SSD_HEADS * d for d in range(2)])
    gw = SSD_D_INNER // SSD_GROUPS
    fwd = lambda c: (c, 0)
    bwd = lambda c: (nc - 1 - c, 0)
    return pl.pallas_call(
        functools.partial(_ssd_core_kernel, q=q),
        grid=(nc,),
        in_specs=[
            pl.BlockSpec((q, CONV_CH), fwd),
            pl.BlockSpec((q, CONV_CH), bwd),
            pl.BlockSpec((q, SMALL_CH), fwd),
            pl.BlockSpec((q, SMALL_CH), bwd),
            _resident((1, SMALL_CH)),
            _resident((1, SMALL_CH)),
            _resident((1, SSD_D_INNER)),
            _resident((2, SMALL_CH, SSD_D_INNER)),
            _resident((2, SMALL_CH, SSD_D_INNER)),
        ],
        out_specs=[pl.BlockSpec((q, SSD_D_INNER), fwd), pl.BlockSpec((q, SSD_D_INNER), bwd)],
        out_shape=[jax.ShapeDtypeStruct((L, SSD_D_INNER), BF16)] * 2,
        scratch_shapes=[pltpu.VMEM((SSD_GROUPS, SSD_STATE, gw), F32)] * 2,
        compiler_params=_params("arbitrary"),
        name="ssd_core",
    )(act, act, small, small, bias, alog, dskip, expand.astype(BF16), expand.astype(F32))


def _ssd_out_kernel(yf_ref, yb_ref, z_ref, h_ref, gnw_ref, wo_ref, pnw_ref, o_ref):
    y = yf_ref[...].astype(F32) + yb_ref[...].astype(F32)
    yz = y * _silu(z_ref[...].astype(F32))
    gw = SSD_D_INNER // SSD_GROUPS
    parts = []
    for g in range(SSD_GROUPS):
        blk = yz[:, g * gw:(g + 1) * gw]
        parts.append(blk * lax.rsqrt(jnp.mean(blk * blk, axis=-1, keepdims=True) + NORM_EPS))
    yn = (jnp.concatenate(parts, axis=1) * gnw_ref[...]).astype(BF16)
    m = _dot(yn, wo_ref[...])
    o_ref[...] = h_ref[...] + _rms(m, pnw_ref[...])


def _mixer_out_call(kern, name, h, streams, consts):
    L = h.shape[0]
    tm = min(ROW_TILE, L)
    tiled = lambda a: pl.BlockSpec((tm, a.shape[1]), lambda i: (i, 0))
    return pl.pallas_call(
        kern,
        grid=(L // tm,),
        in_specs=[tiled(a) for a in streams] + [tiled(h)] + [_resident(c.shape) for c in consts],
        out_specs=tiled(h),
        out_shape=jax.ShapeDtypeStruct(h.shape, F32),
        compiler_params=_params("arbitrary"),
        name=name,
    )(*streams, h, *consts)


def _ssd_layer(h, nw_pre, nw_post, w_in, conv_w, conv_b, dt_bias, a_log, d_skip, norm_w, w_out):
    wc, wz, ws, cw, cb = _split_in_weights(w_in, conv_w, conv_b, conv_first=False)
    act, z, small = _inproj(h, nw_pre.reshape(1, -1), wc, wz, ws, cw, cb)
    yf, yb = _ssd_core(act, small, dt_bias, a_log, d_skip)
    consts = (norm_w.reshape(1, -1), w_out.astype(BF16), nw_post.reshape(1, -1))
    return _mixer_out_call(_ssd_out_kernel, "ssd_out", h, (yf, yb, z), consts)


def _gdn_core_kernel(xf_ref, xb_ref, gcf_ref, gcb_ref, grf_ref, grb_ref, pc_ref, pr_ref,
                     of_ref, ob_ref, s_ref, u_ref, wq_ref, kd_ref, qkm_ref, et_ref, *, rb, ck):
    @pl.when(pl.program_id(0) == 0)
    def _():
        s_ref[...] = jnp.zeros_like(s_ref)

    n = rb // ck
    hd = GDN_HEAD_DIM
    nq, nv = GDN_QK_HEADS, GDN_V_HEADS
    k_off, v_off = nq * hd, 2 * nq * hd
    row = lax.broadcasted_iota(jnp.int32, (ck, ck), 0)
    col = lax.broadcasted_iota(jnp.int32, (ck, ck), 1)
    lower, upper = row >= col, row <= col
    lower_s, upper_s = row > col, row < col
    lower_f, upper_f = lower.astype(F32), upper.astype(F32)
    eye = (row == col).astype(F32)
    alog_c, bias_c = pc_ref[0:1, :], pc_ref[1:2, :]
    alog_r, bias_r = pr_ref[:, 0:1], pr_ref[:, 1:2]
    is_a_c = lax.broadcasted_iota(jnp.int32, (ck, SMALL_CH), 1) < 2 * nv
    is_a_r = lax.broadcasted_iota(jnp.int32, (4 * nv, ck), 0) < 2 * nv

    def gates(raw, alog, bias, is_a):
        return jnp.where(is_a, -jnp.exp(alog) * jax.nn.softplus(raw + bias), jax.nn.sigmoid(raw))

    def solve_chunk(ci, carry):
        units = []
        for d in range(2):
            x_ref, gc_ref, gr_ref = (xf_ref, gcf_ref, grf_ref) if d == 0 else (xb_ref, gcb_ref, grb_ref)
            cc = ci if d == 0 else n - 1 - ci
            rows = pl.ds(pl.multiple_of(cc * ck, ck), ck)
            tri_c, tri_r, mask, strict = ((lower_f, upper_f, lower, lower_s) if d == 0 else
                                          (upper_f, lower_f, upper, upper_s))
            gb_c = gates(gc_ref[rows, :], alog_c, bias_c, is_a_c)
            gb_r = gates(gr_ref[cc], alog_r, bias_r, is_a_r)
            cs_c = jnp.dot(tri_c, gb_c, precision=HIGHEST, preferred_element_type=F32)
            cs_r = jnp.dot(gb_r, tri_r, precision=HIGHEST, preferred_element_type=F32)
            tot = cs_c[ck - 1:ck, :] if d == 0 else cs_c[0:1, :]
            e_cs = jnp.exp(cs_c)
            k_fac = jnp.exp(tot - cs_c)
            et_ref[d, cc] = jnp.exp(tot)
            beta_al = pltpu.roll(gb_c, SMALL_CH - 2 * nv, axis=1)
            for hq in range(nq):
                qx = x_ref[rows, hq * hd:(hq + 1) * hd].astype(F32)
                kx = x_ref[rows, k_off + hq * hd:k_off + (hq + 1) * hd].astype(F32)
                qn = qx * lax.rsqrt(jnp.sum(qx * qx, axis=-1, keepdims=True) + NORM_EPS) * (hd ** -0.5)
                kn = kx * lax.rsqrt(jnp.sum(kx * kx, axis=-1, keepdims=True) + NORM_EPS)
                gram = lax.dot_general(jnp.concatenate([kn, qn], axis=0).astype(BF16), kn.astype(BF16),
                                       NT_DIMS, preferred_element_type=F32)
                kk, qk = gram[:ck], gram[ck:]
                for j in range(nv // nq):
                    vh = (nv // nq) * hq + j
                    cg = nv * d + vh
                    gcs_c, gcs_r = cs_c[:, cg:cg + 1], cs_r[cg:cg + 1, :]
                    beta_c = beta_al[:, cg:cg + 1]
                    decay = jnp.exp(jnp.where(mask, gcs_c - gcs_r, -jnp.inf))
                    a_mat = jnp.where(strict, beta_c * kk * decay, 0.0)
                    e_g = e_cs[:, cg:cg + 1]
                    vx = x_ref[rows, v_off + vh * hd:v_off + (vh + 1) * hd].astype(F32)
                    rhs = jnp.concatenate([vx * beta_c, kn * (beta_c * e_g)], axis=1).astype(BF16)
                    wq_ref[d, vh, cc, ck:2 * ck, :] = (qn * e_g).astype(BF16)
                    kd_ref[d, vh, cc] = (kn * k_fac[:, cg:cg + 1]).astype(BF16)
                    qkm_ref[d, vh, cc] = (qk * decay).astype(BF16)
                    units.append((d, vh, cc, a_mat, rhs))
        pws = [_dot(a.astype(BF16), a.astype(BF16)) for (_, _, _, a, _) in units]
        tis = [eye - a for (_, _, _, a, _) in units]
        for level in range(5):
            if level < 4:
                res = [_dot(jnp.concatenate([t, p], axis=0).astype(BF16), p.astype(BF16)) for t, p in zip(tis, pws)]
                tis = [t + r[:ck] for t, r in zip(tis, res)]
                pws = [r[ck:] for r in res]
            else:
                tis = [t + _dot(t.astype(BF16), p.astype(BF16)) for t, p in zip(tis, pws)]
        for (d, vh, cc, _, rhs), t in zip(units, tis):
            sol = _dot(t.astype(BF16), rhs)
            u_ref[d, vh, cc] = sol[:, :hd]
            wq_ref[d, vh, cc, 0:ck, :] = sol[:, hd:].astype(BF16)
        return carry

    def scan_chunk(ci, carry):
        units = [(d, vh, ci if d == 0 else n - 1 - ci) for d in range(2) for vh in range(nv)]
        boths = [_dot(wq_ref[d, vh, cc], s_ref[d, vh].astype(BF16)) for d, vh, cc in units]
        v_news = [(u_ref[d, vh, cc] - both[:ck]).astype(BF16) for (d, vh, cc), both in zip(units, boths)]
        for (d, vh, cc), both, v_newb in zip(units, boths, v_news):
            o_ref = of_ref if d == 0 else ob_ref
            rows = pl.ds(pl.multiple_of(cc * ck, ck), ck)
            o = both[ck:] + _dot(qkm_ref[d, vh, cc], v_newb)
            o_ref[rows, vh * hd:(vh + 1) * hd] = o.astype(o_ref.dtype)
        for (d, vh, cc), v_newb in zip(units, v_news):
            scale = et_ref[d, cc][:, nv * d + vh:nv * d + vh + 1]
            s_ref[d, vh] = s_ref[d, vh] * scale + lax.dot_general(
                kd_ref[d, vh, cc], v_newb, TN_DIMS, preferred_element_type=F32)
        return carry

    lax.fori_loop(0, n, solve_chunk, 0)
    lax.fori_loop(0, n, scan_chunk, 0)


def _gdn_core(act, small, dt_bias, a_log):
    L = act.shape[0]
    ck = GDN_CHUNK
    rb = min(GDN_ROWS, L)
    nb = L // rb
    nv = GDN_V_HEADS
    hd = GDN_HEAD_DIM
    n = rb // ck
    g_row = small[:, :4 * nv].reshape(L // ck, ck, 4 * nv).transpose(0, 2, 1)
    p_col = jnp.pad(jnp.stack([a_log.reshape(-1), dt_bias.reshape(-1)]), ((0, 0), (0, SMALL_CH - 2 * nv)))
    p_row = p_col[:, :4 * nv].T
    fwd = lambda b: (b, 0)
    bwd = lambda b: (nb - 1 - b, 0)
    return pl.pallas_call(
        functools.partial(_gdn_core_kernel, rb=rb, ck=ck),
        grid=(nb,),
        in_specs=[
            pl.BlockSpec((rb, CONV_CH), fwd),
            pl.BlockSpec((rb, CONV_CH), bwd),
            pl.BlockSpec((rb, SMALL_CH), fwd),
            pl.BlockSpec((rb, SMALL_CH), bwd),
            pl.BlockSpec((n, 4 * nv, ck), lambda b: (b, 0, 0)),
            pl.BlockSpec((n, 4 * nv, ck), lambda b: (nb - 1 - b, 0, 0)),
            _resident(p_col.shape),
            _resident(p_row.shape),
        ],
        out_specs=[pl.BlockSpec((rb, GDN_VALUE_DIM), fwd), pl.BlockSpec((rb, GDN_VALUE_DIM), bwd)],
        out_shape=[jax.ShapeDtypeStruct((L, GDN_VALUE_DIM), BF16)] * 2,
        scratch_shapes=[
            pltpu.VMEM((2, nv, hd, hd), F32),
            pltpu.VMEM((2, nv, n, ck, hd), F32),
            pltpu.VMEM((2, nv, n, 2 * ck, hd), BF16),
            pltpu.VMEM((2, nv, n, ck, hd), BF16),
            pltpu.VMEM((2, nv, n, ck, ck), BF16),
            pltpu.VMEM((2, n, 1, SMALL_CH), F32),
        ],
        compiler_params=_params("arbitrary"),
        name="gdn_core",
    )(act, act, small, small, g_row, g_row, p_col, p_row)


def _gdn_out_kernel(of_ref, ob_ref, z_ref, h_ref, hnw_ref, wo_ref, pnw_ref, o_ref):
    o = of_ref[...].astype(F32) + ob_ref[...].astype(F32)
    z = z_ref[...].astype(F32)
    hd = GDN_HEAD_DIM
    hnw = hnw_ref[...]
    parts = []
    for v in range(GDN_VALUE_DIM // hd):
        blk = o[:, v * hd:(v + 1) * hd]
        parts.append(_rms(blk, hnw) * _silu(z[:, v * hd:(v + 1) * hd]))
    m = _dot(jnp.concatenate(parts, axis=1).astype(BF16), wo_ref[...])
    o_ref[...] = h_ref[...] + _rms(m, pnw_ref[...])


def _gdn_layer(h, nw_pre, nw_post, w_in, conv_w, conv_b, dt_bias, a_log, norm_w, w_out):
    wc, wz, ws, cw, cb = _split_in_weights(w_in, conv_w, conv_b, conv_first=True)
    act, z, small = _inproj(h, nw_pre.reshape(1, -1), wc, wz, ws, cw, cb)
    of, ob = _gdn_core(act, small, dt_bias, a_log)
    consts = (norm_w.reshape(1, -1), w_out.astype(BF16), nw_post.reshape(1, -1))
    return _mixer_out_call(_gdn_out_kernel, "gdn_out", h, (of, ob, z), consts)


def _s5_prep_kernel(lre_ref, lim_ref, lst_ref, bre_ref, bim_ref, lre4_ref, lim4_ref, lst4_ref,
                    bbr_ref, bbi_ref, pw_ref):
    lre, lim = lre_ref[...], lim_ref[...]
    step = jnp.exp(lst_ref[...])
    mag, ang = jnp.exp(lre * step), lim * step
    lbr, lbi = mag * jnp.cos(ang), mag * jnp.sin(ang)
    inv_den = 1.0 / (lre * lre + lim * lim)
    ar, ai, br, bi = lbr - 1.0, lbi, lre * inv_den, -lim * inv_den
    zr, zi = ar * br - ai * bi, ar * bi + ai * br
    b_re, b_im = bre_ref[...], bim_ref[...]
    bbr_ref[...] = zr * b_re - zi * b_im
    bbi_ref[...] = zr * b_im + zi * b_re
    r = lax.broadcasted_iota(jnp.int32, (S5_TILE, S5_LANES), 0).astype(F32)
    for d in range(2):
        k = r + 1.0 if d == 0 else float(S5_TILE) - r
        st = jnp.exp(lst4_ref[d])
        m, a = jnp.exp(k * (lre4_ref[d] * st)), k * (lim4_ref[d] * st)
        pw_ref[d, 0] = m * jnp.cos(a)
        pw_ref[d, 1] = m * jnp.sin(a)


def _s5_prep(lam_re, lam_im, log_step, b_re, b_im):
    g, n, c = S5_GROUPS, S5_STATE, S5_GROUP
    expand = lambda t: jnp.repeat(t, c, axis=-1)
    lst = jnp.broadcast_to(log_step[:, :, None], (2, g, n))
    flat = lambda t: t.reshape(2, 1, g * n)
    full = lambda s: pl.BlockSpec(s, lambda: (0,) * len(s))
    shapes = [(2, g, n * c)] * 3 + [(1, g, n * c)] * 2 + [(2, 1, g * n)] * 3
    return pl.pallas_call(
        _s5_prep_kernel,
        in_specs=[full(s) for s in shapes],
        out_specs=[full((2, g, n * c)), full((2, g, n * c)), full((2, 2, S5_TILE, g * n))],
        out_shape=[
            jax.ShapeDtypeStruct((2, g, n * c), F32),
            jax.ShapeDtypeStruct((2, g, n * c), F32),
            jax.ShapeDtypeStruct((2, 2, S5_TILE, g * n), F32),
        ],
        name="s5_prep",
    )(expand(lam_re), expand(lam_im), expand(lst), b_re.reshape(1, g, n * c), b_im.reshape(1, g, n * c),
      flat(lam_re), flat(lam_im), flat(lst))


def _s5_core_kernel(hf_ref, hb_ref, nw_ref, win_ref, wout_ref, pw_ref, yf_ref, yb_ref,
                    xr_ref, xi_ref, car_ref, *, rb):
    @pl.when(pl.program_id(0) == 0)
    def _():
        car_ref[...] = jnp.zeros_like(car_ref)

    nw = nw_ref[...]
    nt = rb // S5_TILE
    w = S5_LANES // S5_KB
    cin = D_MODEL // S5_KB
    rowi = lax.broadcasted_iota(jnp.int32, (S5_TILE, w), 0)
    tile = lambda v: jnp.broadcast_to(v, (S5_TILE, w))
    for d, (h_ref, y_ref) in enumerate(((hf_ref, yf_ref), (hb_ref, yb_ref))):
        hn = _rms(h_ref[...], nw).astype(BF16)
        for kb in range(S5_KB):
            x = _dot(hn[:, kb * cin:(kb + 1) * cin], win_ref[d, kb])
            xr_ref[...] = x[:, :w]
            xi_ref[...] = x[:, w:]
            p_r = pw_ref[d, 0, :, kb * w:(kb + 1) * w]
            p_i = pw_ref[d, 1, :, kb * w:(kb + 1) * w]
            consts = []
            for k in (1, 2, 4):
                src = k - 1 if d == 0 else S5_TILE - k
                valid = rowi >= k if d == 0 else rowi <= S5_TILE - 1 - k
                consts.append((jnp.where(valid, tile(p_r[src:src + 1]), 0.0),
                               jnp.where(valid, tile(p_i[src:src + 1]), 0.0)))
            last = S5_TILE - 1 if d == 0 else 0

            def body(t, carry, d=d, consts=consts, p_r=p_r, p_i=p_i, last=last):
                cr, ci = carry
                tt = t if d == 0 else nt - 1 - t
                rows = pl.ds(pl.multiple_of(tt * S5_TILE, S5_TILE), S5_TILE)
                sr, si = xr_ref[rows, :], xi_ref[rows, :]
                for k, (a_r, a_i) in zip((1, 2, 4), consts):
                    sh = k if d == 0 else S5_TILE - k
                    tr, ti = pltpu.roll(sr, sh, 0), pltpu.roll(si, sh, 0)
                    sr, si = sr + a_r * tr - a_i * ti, si + a_r * ti + a_i * tr
                sr, si = sr + p_r * cr - p_i * ci, si + p_r * ci + p_i * cr
                xr_ref[rows, :] = sr
                xi_ref[rows, :] = si
                return tile(sr[last:last + 1]), tile(si[last:last + 1])

            cr, ci = lax.fori_loop(0, nt, body, (car_ref[d, kb, 0], car_ref[d, kb, 1]))
            car_ref[d, kb, 0] = cr
            car_ref[d, kb, 1] = ci
            s_cat = jnp.concatenate([xr_ref[...], xi_ref[...]], axis=1).astype(BF16)
            y_ref[:, kb * cin:(kb + 1) * cin] = _dot(s_cat, wout_ref[d, kb]).astype(y_ref.dtype)


def _s5_block_diag(t):
    per = S5_GROUPS // S5_KB
    a, b = t.shape[2], t.shape[3]
    t = t.reshape(2, S5_KB, per, a, b)
    eye = jnp.eye(per, dtype=t.dtype)
    return jnp.einsum('dkgab,gh->dkgahb', t, eye).reshape(2, S5_KB, per * a, per * b)


def _s5_core(h, nw, bb_r, bb_i, c_re, c_im, pw):
    L = h.shape[0]
    rb = min(S5_ROWS, L)
    nb = L // rb
    g, n, c = S5_GROUPS, S5_STATE, S5_GROUP
    to_in = lambda t: _s5_block_diag(t.reshape(2, g, n, c).transpose(0, 1, 3, 2))
    w_in = jnp.concatenate([to_in(bb_r), to_in(bb_i)], axis=-1).astype(BF16)
    to_out = lambda t: _s5_block_diag(t.transpose(0, 1, 3, 2))
    w_out = jnp.concatenate([to_out(c_re), to_out(-c_im)], axis=-2).astype(BF16)
    w = S5_LANES // S5_KB
    return pl.pallas_call(
        functools.partial(_s5_core_kernel, rb=rb),
        grid=(nb,),
        in_specs=[
            pl.BlockSpec((rb, D_MODEL), lambda b: (b, 0)),
            pl.BlockSpec((rb, D_MODEL), lambda b: (nb - 1 - b, 0)),
            _resident((1, D_MODEL)),
            _resident(w_in.shape),
            _resident(w_out.shape),
            _resident(pw.shape),
        ],
        out_specs=[pl.BlockSpec((rb, D_MODEL), lambda b: (b, 0)),
                   pl.BlockSpec((rb, D_MODEL), lambda b: (nb - 1 - b, 0))],
        out_shape=[jax.ShapeDtypeStruct((L, D_MODEL), BF16)] * 2,
        scratch_shapes=[pltpu.VMEM((rb, w), F32), pltpu.VMEM((rb, w), F32),
                        pltpu.VMEM((2, S5_KB, 2, S5_TILE, w), F32)],
        compiler_params=_params("arbitrary"),
        name="s5_core",
    )(h, h, nw, w_in, w_out, pw)


def _s5_out_kernel(yf_ref, yb_ref, h_ref, nw_ref, dsk_ref, wg_ref, bg_ref, pnw_ref, o_ref):
    h = h_ref[...]
    hn = _rms(h, nw_ref[...])
    y = yf_ref[...].astype(F32) + yb_ref[...].astype(F32) + dsk_ref[...] * hn
    y = jax.nn.gelu(y)
    vg = _dot(y.astype(BF16), wg_ref[...]) + bg_ref[...]
    m = vg[:, :D_MODEL] * jax.nn.sigmoid(vg[:, D_MODEL:])
    o_ref[...] = h + _rms(m, pnw_ref[...])


def _s5_layer(h, nw_pre, nw_post, lam_re, lam_im, log_step, b_re, b_im, c_re, c_im, d_skip, w_glu, b_glu):
    nw = nw_pre.reshape(1, -1)
    bb_r, bb_i, pw = _s5_prep(lam_re, lam_im, log_step, b_re, b_im)
    yf, yb = _s5_core(h, nw, bb_r, bb_i, c_re, c_im, pw)
    consts = (nw, d_skip.reshape(1, -1), w_glu.astype(BF16), b_glu.reshape(1, -1), nw_post.reshape(1, -1))
    return _mixer_out_call(_s5_out_kernel, "s5_out", h, (yf, yb), consts)


def _ffn_kernel(h_ref, nw_ref, wgu_ref, wd_ref, pnw_ref, o_ref, act_ref, *, fchunk):
    h = h_ref[...]
    hn = _rms(h, nw_ref[...]).astype(BF16)
    for c in range(D_FF // fchunk):
        gate = _dot(hn, wgu_ref[:, c * fchunk:(c + 1) * fchunk])
        up = _dot(hn, wgu_ref[:, D_FF + c * fchunk:D_FF + (c + 1) * fchunk])
        act_ref[:, c * fchunk:(c + 1) * fchunk] = (_silu(gate) * up).astype(BF16)
    f = _dot(act_ref[...], wd_ref[...])
    o_ref[...] = h + _rms(f, pnw_ref[...])


def _ffn(h, nw_pre, nw_post, w_gate_up, w_down):
    L = h.shape[0]
    tm = min(ROW_TILE, L)
    tiled = pl.BlockSpec((tm, D_MODEL), lambda i: (i, 0))
    return pl.pallas_call(
        functools.partial(_ffn_kernel, fchunk=256),
        grid=(L // tm,),
        in_specs=[tiled, _resident((1, D_MODEL)), _resident((D_MODEL, 2 * D_FF)),
                  _resident((D_FF, D_MODEL)), _resident((1, D_MODEL))],
        out_specs=tiled,
        out_shape=jax.ShapeDtypeStruct(h.shape, F32),
        scratch_shapes=[pltpu.VMEM((tm, D_FF), BF16)],
        compiler_params=_params("arbitrary"),
        name="ffn",
    )(h, nw_pre.reshape(1, -1), w_gate_up.astype(BF16), w_down.astype(BF16), nw_post.reshape(1, -1))


def kernel(x, norm_w, ssd_w_in, ssd_conv_w, ssd_conv_b, ssd_dt_bias, ssd_a_log, ssd_d, ssd_norm_w, ssd_w_out, gdn_w_in, gdn_conv_w, gdn_conv_b, gdn_dt_bias, gdn_a_log, gdn_norm_w, gdn_w_out, s5_lam_re, s5_lam_im, s5_log_step, s5_b_re, s5_b_im, s5_c_re, s5_c_im, s5_d, s5_w_glu, s5_b_glu, ffn_w_gate_up, ffn_w_down):
    bt, L, d = x.shape
    assert bt == 1 and d == D_MODEL
    h = x.reshape(L, d)
    for i in range(norm_w.shape[0]):
        kind, j = i % 3, i // 3
        if kind == 0:
            h = _ssd_layer(h, norm_w[i, 0], norm_w[i, 1], ssd_w_in[j], ssd_conv_w[j], ssd_conv_b[j],
                           ssd_dt_bias[j], ssd_a_log[j], ssd_d[j], ssd_norm_w[j], ssd_w_out[j])
        elif kind == 1:
            h = _gdn_layer(h, norm_w[i, 0], norm_w[i, 1], gdn_w_in[j], gdn_conv_w[j], gdn_conv_b[j],
                           gdn_dt_bias[j], gdn_a_log[j], gdn_norm_w[j], gdn_w_out[j])
        else:
            h = _s5_layer(h, norm_w[i, 0], norm_w[i, 1], s5_lam_re[j], s5_lam_im[j], s5_log_step[j],
                          s5_b_re[j], s5_b_im[j], s5_c_re[j], s5_c_im[j], s5_d[j], s5_w_glu[j], s5_b_glu[j])
        h = _ffn(h, norm_w[i, 2], norm_w[i, 3], ffn_w_gate_up[i], ffn_w_down[i])
    return h.reshape(bt, L, d)
```

```python
import functools

import jax
import jax.numpy as jnp
from jax import lax
from jax.experimental import pallas as pl
from jax.experimental.pallas import tpu as pltpu

F32 = jnp.float32
BF16 = jnp.bfloat16
HIGHEST = lax.Precision.HIGHEST

D_MODEL = 1024
NORM_EPS = 1e-6
CONV_TAPS = 5
CONV_CH = 4096
Z_CH = 2048
SMALL_CH = 128
HALO = 8

SSD_HEADS = 32
SSD_HEAD_DIM = 64
SSD_GROUPS = 8
SSD_STATE = 128
SSD_CHUNK = 128
SSD_D_INNER = 2048

GDN_HEAD_DIM = 128
GDN_QK_HEADS = 8
GDN_V_HEADS = 16
GDN_KEY_DIM = GDN_QK_HEADS * GDN_HEAD_DIM
GDN_CHUNK = 64
GDN_VALUE_DIM = 2048

S5_GROUPS = 64
S5_GROUP = 16
S5_STATE = 64
S5_LANES = S5_GROUPS * S5_STATE
S5_KB = 4
S5_TILE = 8

D_FF = 2816

ROW_TILE = 512
GDN_ROWS = 256
S5_ROWS = 512
VMEM_LIMIT_BYTES = 56 * 1024 * 1024

NT_DIMS = (((1,), (1,)), ((), ()))
TN_DIMS = (((0,), (0,)), ((), ()))


def _params(*semantics):
    return pltpu.CompilerParams(dimension_semantics=semantics, vmem_limit_bytes=VMEM_LIMIT_BYTES)


def _resident(shape):
    nd = len(shape)
    return pl.BlockSpec(shape, lambda *_: (0,) * nd, pipeline_mode=pl.Buffered(1))


def _rms(x, w):
    return x * lax.rsqrt(jnp.mean(x * x, axis=-1, keepdims=True) + NORM_EPS) * w


def _silu(x):
    return x * jax.nn.sigmoid(x)


def _dot(a, b):
    return jnp.dot(a, b, preferred_element_type=F32)


def _inproj_kernel(h_ref, hp_ref, hx_ref, nw_ref, wc_ref, wz_ref, ws_ref, cw_ref, cb_ref, sb_ref, sa_ref,
                   act_ref, z_ref, s_ref, p_ref, *, tm, nblk, cchunk, gdn):
    i = pl.program_id(0)
    nw = nw_ref[...]
    cur = _rms(h_ref[...], nw)
    prev = _rms(hp_ref[...], nw) * jnp.where(i > 0, 1.0, 0.0)
    nxt = _rms(hx_ref[...], nw) * jnp.where(i < nblk - 1, 1.0, 0.0)
    curb = cur.astype(BF16)
    ext = jnp.concatenate([prev, cur, nxt], axis=0).astype(BF16)
    z_ref[...] = _dot(curb, wz_ref[...]).astype(z_ref.dtype)
    raw = _dot(curb, ws_ref[...]) + sb_ref[...]
    if gdn:
        is_decay = lax.broadcasted_iota(jnp.int32, raw.shape, 1) < 2 * GDN_V_HEADS
        s_ref[...] = jnp.where(is_decay, -jnp.exp(sa_ref[...]) * jax.nn.softplus(raw), jax.nn.sigmoid(raw))
    else:
        s_ref[...] = jax.nn.softplus(raw)
    pad = CONV_TAPS // 2
    hd = GDN_HEAD_DIM
    for c in range(CONV_CH // cchunk):
        sl = slice(c * cchunk, (c + 1) * cchunk)
        p_ref[...] = _dot(ext, wc_ref[:, sl])
        acc = cb_ref[:, sl]
        for k in range(CONV_TAPS):
            acc = acc + cw_ref[k:k + 1, sl] * p_ref[HALO - pad + k:HALO - pad + k + tm, :]
        y = _silu(acc)
        if gdn and c * cchunk < 2 * GDN_KEY_DIM:
            parts = []
            for s in range(cchunk // hd):
                blk = y[:, s * hd:(s + 1) * hd]
                inv = lax.rsqrt(jnp.sum(blk * blk, axis=-1, keepdims=True) + NORM_EPS)
                if c * cchunk + s * hd < GDN_KEY_DIM:
                    inv = inv * (hd ** -0.5)
                parts.append(blk * inv)
            y = jnp.concatenate(parts, axis=1)
        act_ref[:, sl] = y.astype(act_ref.dtype)


def _inproj(h, nw, wc, wz, ws, conv_w, conv_b, s_bias, s_alog, gdn):
    L = h.shape[0]
    tm = min(ROW_TILE, L)
    nblk = L // tm
    per = tm // HALO
    last = L // HALO - 1
    kern = functools.partial(_inproj_kernel, tm=tm, nblk=nblk, cchunk=512, gdn=gdn)
    return pl.pallas_call(
        kern,
        grid=(nblk,),
        in_specs=[
            pl.BlockSpec((tm, D_MODEL), lambda i: (i, 0)),
            pl.BlockSpec((HALO, D_MODEL), lambda i: (jnp.maximum(i * per - 1, 0), 0)),
            pl.BlockSpec((HALO, D_MODEL), lambda i: (jnp.minimum((i + 1) * per, last), 0)),
            _resident((1, D_MODEL)),
            _resident((D_MODEL, CONV_CH)),
            _resident((D_MODEL, Z_CH)),
            _resident((D_MODEL, SMALL_CH)),
            _resident((8, CONV_CH)),
            _resident((1, CONV_CH)),
            _resident((1, SMALL_CH)),
            _resident((1, SMALL_CH)),
        ],
        out_specs=[
            pl.BlockSpec((tm, CONV_CH), lambda i: (i, 0)),
            pl.BlockSpec((tm, Z_CH), lambda i: (i, 0)),
            pl.BlockSpec((tm, SMALL_CH), lambda i: (i, 0)),
        ],
        out_shape=[
            jax.ShapeDtypeStruct((L, CONV_CH), BF16),
            jax.ShapeDtypeStruct((L, Z_CH), BF16),
            jax.ShapeDtypeStruct((L, SMALL_CH), F32),
        ],
        scratch_shapes=[pltpu.VMEM((tm + 2 * HALO, 512), F32)],
        compiler_params=_params("arbitrary"),
        name="inproj_conv",
    )(h, h, h, nw, wc, wz, ws, conv_w, conv_b, s_bias, s_alog)


def _pad_small(t):
    t = t.reshape(1, -1)
    return jnp.pad(t, ((0, 0), (0, SMALL_CH - t.shape[1])))


def _split_in_weights(w_in, conv_w, conv_b, conv_first):
    if conv_first:
        wc, wz = w_in[:, :CONV_CH], w_in[:, CONV_CH:CONV_CH + Z_CH]
    else:
        wz, wc = w_in[:, :Z_CH], w_in[:, Z_CH:Z_CH + CONV_CH]
    ws = w_in[:, CONV_CH + Z_CH:]
    ws = jnp.pad(ws, ((0, 0), (0, SMALL_CH - ws.shape[1])))
    cw = jnp.pad(conv_w, ((0, 8 - CONV_TAPS), (0, 0)))
    return wc.astype(BF16), wz.astype(BF16), ws.astype(BF16), cw, conv_b.reshape(1, CONV_CH)


def _ssd_core_kernel(xf_ref, xb_ref, dtf_ref, dtb_ref, alog_ref, dskip_ref, exb_ref, exf_ref,
                     yf_ref, yb_ref, sf_ref, sb_ref, *, q):
    @pl.when(pl.program_id(0) == 0)
    def _():
        sf_ref[...] = jnp.zeros_like(sf_ref)
        sb_ref[...] = jnp.zeros_like(sb_ref)

    row = lax.broadcasted_iota(jnp.int32, (q, q), 0)
    col = lax.broadcasted_iota(jnp.int32, (q, q), 1)
    lower = row >= col
    upper = row <= col
    lower_f = lower.astype(F32)
    upper_f = upper.astype(F32)
    a = -jnp.exp(alog_ref[...])
    p = SSD_HEAD_DIM
    per_group = SSD_HEADS // SSD_GROUPS
    gw = per_group * p
    pw = 2 * p
    b_off = SSD_D_INNER
    c_off = SSD_D_INNER + SSD_GROUPS * SSD_STATE
    left = lax.broadcasted_iota(jnp.int32, (q, pw), 1) < p

    dirs = ((xf_ref, dtf_ref, yf_ref, sf_ref), (xb_ref, dtb_ref, yb_ref, sb_ref))
    pre = []
    for d, (x_ref, dt_ref, y_ref, s_ref) in enumerate(dirs):
        dt = dt_ref[...]
        da = dt * a
        tri_c, tri_r = (lower_f, upper_f) if d == 0 else (upper_f, lower_f)
        cs = jnp.dot(tri_c, da, precision=HIGHEST, preferred_element_type=F32)
        cs_t = jnp.dot(da.T, tri_r, precision=HIGHEST, preferred_element_type=F32)
        tot = cs[q - 1:q, :] if d == 0 else cs[0:1, :]
        exb = exb_ref[d]
        dt_x = _dot(dt.astype(BF16), exb)
        e_x = _dot(jnp.exp(cs).astype(BF16), exb)
        w_x = _dot((dt * jnp.exp(tot - cs)).astype(BF16), exb)
        tot_x = jnp.dot(jnp.broadcast_to(tot, (8, SMALL_CH)), exf_ref[d], precision=HIGHEST,
                        preferred_element_type=F32)[0:1]
        pre.append((cs, cs_t, dt_x, e_x, w_x, jnp.exp(tot_x)))

    jobs = [(d, g) for d in range(2) for g in range(SSD_GROUPS)]
    bmats = [dirs[d][0][:, b_off + g * SSD_STATE:b_off + (g + 1) * SSD_STATE] for d, g in jobs]
    cmats = [dirs[d][0][:, c_off + g * SSD_STATE:c_off + (g + 1) * SSD_STATE] for d, g in jobs]
    cbs = [lax.dot_general(c, b, NT_DIMS, preferred_element_type=F32) for c, b in zip(cmats, bmats)]
    yoffs = [_dot(c, dirs[d][3][g].astype(BF16)) for (d, g), c in zip(jobs, cmats)]
    xgs = [dirs[d][0][:, g * gw:(g + 1) * gw].astype(F32) for d, g in jobs]

    for (d, g), cb, yoff, xg in zip(jobs, cbs, yoffs, xgs):
        cs, cs_t, dt_x, e_x, _, _ = pre[d]
        mask = lower if d == 0 else upper
        y_ref = dirs[d][2]
        for pr in range(per_group // 2):
            h0 = per_group * g + 2 * pr
            lms = []
            for hh in (h0, h0 + 1):
                ch = hh + SSD_HEADS * d
                lms.append(cb * jnp.exp(jnp.where(mask, cs[:, ch:ch + 1] - cs_t[ch:ch + 1, :], -jnp.inf)))
            m_pair = jnp.concatenate(lms, axis=1).astype(BF16)
            sl = slice(h0 * p, h0 * p + pw)
            lo = slice(2 * pr * p, 2 * pr * p + pw)
            xp = xg[:, lo]
            xdt = xp * dt_x[:, sl]
            rhs = jnp.concatenate([jnp.where(left, xdt, 0.0), jnp.where(left, 0.0, xdt)], axis=0).astype(BF16)
            y = _dot(m_pair, rhs) + yoff[:, lo] * e_x[:, sl]
            if d == 0:
                y = y + xp * dskip_ref[:, sl]
            y_ref[:, sl] = y.astype(y_ref.dtype)

    for (d, g), bmat, xg in zip(jobs, bmats, xgs):
        _, _, _, _, w_x, cdec_x = pre[d]
        s_ref = dirs[d][3]
        gs = slice(g * gw, (g + 1) * gw)
        new = lax.dot_general(bmat, (xg * w_x[:, gs]).astype(BF16), TN_DIMS, preferred_element_type=F32)
        s_ref[g] = s_ref[g] * cdec_x[:, gs] + new


def _ssd_core(act, small, a_log, d_skip):
    L = act.shape[0]
    q = SSD_CHUNK
    nc = L // q
    alog = _pad_small(a_log)
    dskip = jnp.repeat(d_skip, SSD_HEAD_DIM).reshape(1, SSD_D_INNER)
    head_of_channel = jnp.arange(SSD_D_INNER) // SSD_HEAD_DIM
    col_id = jnp.arange(SMALL_CH)
    expand = jnp.stack([col_id[:, None] == head_of_channel[None, :] + SSD_HEADS * d for d in range(2)])
    gw = SSD_D_INNER // SSD_GROUPS
    fwd = lambda c: (c, 0)
    bwd = lambda c: (nc - 1 - c, 0)
    return pl.pallas_call(
        functools.partial(_ssd_core_kernel, q=q),
        grid=(nc,),
        in_specs=[
            pl.BlockSpec((q, CONV_CH), fwd),
            pl.BlockSpec((q, CONV_CH), bwd),
            pl.BlockSpec((q, SMALL_CH), fwd),
            pl.BlockSpec((q, SMALL_CH), bwd),
            _resident((1, SMALL_CH)),
            _resident((1, SSD_D_INNER)),
            _resident((2, SMALL_CH, SSD_D_INNER)),
            _resident((2, SMALL_CH, SSD_D_INNER)),
        ],
        out_specs=[pl.BlockSpec((q, SSD_D_INNER), fwd), pl.BlockSpec((q, SSD_D_INNER), bwd)],
        out_shape=[jax.ShapeDtypeStruct((L, SSD_D_INNER), BF16)] * 2,
        scratch_shapes=[pltpu.VMEM((SSD_GROUPS, SSD_STATE, gw), F32)] * 2,
        compiler_params=_params("arbitrary"),
        name="ssd_core",
    )(act, act, small, small, alog, dskip, expand.astype(BF16), expand.astype(F32))


def _ssd_out_kernel(yf_ref, yb_ref, z_ref, h_ref, gnw_ref, wo_ref, pnw_ref, o_ref):
    y = yf_ref[...].astype(F32) + yb_ref[...].astype(F32)
    yz = y * _silu(z_ref[...].astype(F32))
    gw = SSD_D_INNER // SSD_GROUPS
    parts = []
    for g in range(SSD_GROUPS):
        blk = yz[:, g * gw:(g + 1) * gw]
        parts.append(blk * lax.rsqrt(jnp.mean(blk * blk, axis=-1, keepdims=True) + NORM_EPS))
    yn = (jnp.concatenate(parts, axis=1) * gnw_ref[...]).astype(BF16)
    m = _dot(yn, wo_ref[...])
    o_ref[...] = h_ref[...] + _rms(m, pnw_ref[...])


def _mixer_out_call(kern, name, h, streams, consts):
    L = h.shape[0]
    tm = min(ROW_TILE, L)
    tiled = lambda a: pl.BlockSpec((tm, a.shape[1]), lambda i: (i, 0))
    return pl.pallas_call(
        kern,
        grid=(L // tm,),
        in_specs=[tiled(a) for a in streams] + [tiled(h)] + [_resident(c.shape) for c in consts],
        out_specs=tiled(h),
        out_shape=jax.ShapeDtypeStruct(h.shape, F32),
        compiler_params=_params("arbitrary"),
        name=name,
    )(*streams, h, *consts)


def _ssd_layer(h, nw_pre, nw_post, w_in, conv_w, conv_b, dt_bias, a_log, d_skip, norm_w, w_out):
    wc, wz, ws, cw, cb = _split_in_weights(w_in, conv_w, conv_b, conv_first=False)
    act, z, small = _inproj(h, nw_pre.reshape(1, -1), wc, wz, ws, cw, cb, _pad_small(dt_bias), _pad_small(a_log),
                            gdn=False)
    yf, yb = _ssd_core(act, small, a_log, d_skip)
    consts = (norm_w.reshape(1, -1), w_out.astype(BF16), nw_post.reshape(1, -1))
    return _mixer_out_call(_ssd_out_kernel, "ssd_out", h, (yf, yb, z), consts)


def _gdn_core_kernel(xf_ref, xb_ref, gcf_ref, gcb_ref, grf_ref, grb_ref, ex_ref, exh_ref,
                     of_ref, ob_ref, s_ref, u_ref, wq_ref, kd_ref, qkm_ref, et_ref, *, rb, ck):
    @pl.when(pl.program_id(0) == 0)
    def _():
        s_ref[...] = jnp.zeros_like(s_ref)

    n = rb // ck
    hd = GDN_HEAD_DIM
    nq, nv = GDN_QK_HEADS, GDN_V_HEADS
    k_off, v_off = nq * hd, 2 * nq * hd
    row = lax.broadcasted_iota(jnp.int32, (ck, ck), 0)
    col = lax.broadcasted_iota(jnp.int32, (ck, ck), 1)
    lower, upper = row >= col, row <= col
    lower_s, upper_s = row > col, row < col
    lower_f, upper_f = lower.astype(F32), upper.astype(F32)
    row2 = lax.broadcasted_iota(jnp.int32, (ck, 2 * ck), 0)
    lane2 = lax.broadcasted_iota(jnp.int32, (ck, 2 * ck), 1)
    left = lane2 < ck
    col2 = jnp.where(left, lane2, lane2 - ck)
    pair = lambda t: jnp.concatenate([t, t], axis=1)
    eye2 = (row2 == col2).astype(F32)

    def block_diag(p2):
        return jnp.concatenate([jnp.where(left, p2, 0.0), jnp.where(left, 0.0, p2)], axis=0).astype(BF16)

    def solve_chunk(ci, carry):
        pairs = []
        for d in range(2):
            x_ref, gc_ref, gr_ref = (xf_ref, gcf_ref, grf_ref) if d == 0 else (xb_ref, gcb_ref, grb_ref)
            cc = ci if d == 0 else n - 1 - ci
            rows = pl.ds(pl.multiple_of(cc * ck, ck), ck)
            tri_c, tri_r = (lower_f, upper_f) if d == 0 else (upper_f, lower_f)
            mask2, strict2 = (row2 >= col2, row2 > col2) if d == 0 else (row2 <= col2, row2 < col2)
            gb_c = gc_ref[rows, :]
            gb_r = gr_ref[cc]
            cs_c = jnp.dot(tri_c, gb_c, precision=HIGHEST, preferred_element_type=F32)
            cs_r = jnp.dot(gb_r, tri_r, precision=HIGHEST, preferred_element_type=F32)
            tot = cs_c[ck - 1:ck, :] if d == 0 else cs_c[0:1, :]
            et_ref[d, cc] = jnp.exp(tot)
            beta_al = pltpu.roll(gb_c, SMALL_CH - 2 * nv, axis=1)
            ex, ex_half = ex_ref[d], exh_ref[d]
            beta_b = beta_al.astype(BF16)
            beta_h = _dot(beta_b, ex_half)
            stacked = jnp.concatenate([beta_b, jnp.exp(cs_c).astype(BF16), jnp.exp(tot - cs_c).astype(BF16)], axis=0)
            stacked_x = _dot(stacked, ex)
            beta_x = stacked_x[:ck]
            e_x = stacked_x[ck:2 * ck]
            kf_x = stacked_x[2 * ck:]
            for hq in range(nq):
                qn = x_ref[rows, hq * hd:(hq + 1) * hd]
                kn = x_ref[rows, k_off + hq * hd:k_off + (hq + 1) * hd]
                gram = lax.dot_general(jnp.concatenate([kn, qn], axis=0), kn, NT_DIMS,
                                       preferred_element_type=F32)
                kk2, qk2 = pair(gram[:ck]), pair(gram[ck:])
                c0 = nv * d + 2 * hq
                gcs_c2 = jnp.where(left, cs_c[:, c0:c0 + 1], cs_c[:, c0 + 1:c0 + 2])
                gcs_r2 = jnp.concatenate([cs_r[c0:c0 + 1, :], cs_r[c0 + 1:c0 + 2, :]], axis=1)
                beta2 = beta_h[:, hq * 2 * ck:(hq + 1) * 2 * ck]
                decay2 = jnp.exp(jnp.where(mask2, gcs_c2 - gcs_r2, -jnp.inf))
                a2 = jnp.where(strict2, beta2 * kk2 * decay2, 0.0)
                qkm_ref[d, hq, cc] = (qk2 * decay2).astype(BF16)
                qf, kf = qn.astype(F32), kn.astype(F32)
                rhs = []
                for j in range(2):
                    vh, cg = 2 * hq + j, c0 + j
                    hs = slice(vh * hd, (vh + 1) * hd)
                    beta_c, e_g = beta_x[:, hs], e_x[:, hs]
                    vx = x_ref[rows, v_off + vh * hd:v_off + (vh + 1) * hd].astype(F32)
                    rhs.append(jnp.concatenate([vx * beta_c, kf * (beta_c * e_g)], axis=1).astype(BF16))
                    wq_ref[d, vh, cc, ck:2 * ck, :] = (qf * e_g).astype(BF16)
                    kd_ref[d, vh, cc] = (kf * kf_x[:, hs]).astype(BF16)
                zero = jnp.zeros_like(rhs[0])
                rhs2 = jnp.concatenate([jnp.concatenate([rhs[0], zero], axis=1),
                                        jnp.concatenate([zero, rhs[1]], axis=1)], axis=0)
                pairs.append((d, hq, cc, a2, rhs2))
        pws = [_dot(a2.astype(BF16), block_diag(a2)) for (_, _, _, a2, _) in pairs]
        tis = [eye2 - a2 for (_, _, _, a2, _) in pairs]
        for level in range(5):
            if level < 4:
                res = [_dot(jnp.concatenate([t, p], axis=0).astype(BF16), block_diag(p)) for t, p in zip(tis, pws)]
                tis = [t + r[:ck] for t, r in zip(tis, res)]
                pws = [r[ck:] for r in res]
            else:
                tis = [t + _dot(t.astype(BF16), block_diag(p)) for t, p in zip(tis, pws)]
        for (d, hq, cc, _, rhs2), t in zip(pairs, tis):
            sol = _dot(t.astype(BF16), rhs2)
            for j in range(2):
                vh = 2 * hq + j
                u_ref[d, vh, cc] = sol[:, 2 * j * hd:(2 * j + 1) * hd]
                wq_ref[d, vh, cc, 0:ck, :] = sol[:, (2 * j + 1) * hd:(2 * j + 2) * hd].astype(BF16)
        return carry

    def scan_chunk(ci, carry):
        units = [(d, vh, ci if d == 0 else n - 1 - ci) for d in range(2) for vh in range(nv)]
        boths = [_dot(wq_ref[d, vh, cc], s_ref[d, vh].astype(BF16)) for d, vh, cc in units]
        v_news = [(u_ref[d, vh, cc] - both[:ck]).astype(BF16) for (d, vh, cc), both in zip(units, boths)]
        for i in range(0, len(units), 2):
            d, vh, cc = units[i]
            o_ref = of_ref if d == 0 else ob_ref
            rows = pl.ds(pl.multiple_of(cc * ck, ck), ck)
            zero = jnp.zeros_like(v_news[i])
            v2 = jnp.concatenate([jnp.concatenate([v_news[i], zero], axis=1),
                                  jnp.concatenate([zero, v_news[i + 1]], axis=1)], axis=0)
            o2 = jnp.concatenate([boths[i][ck:], boths[i + 1][ck:]], axis=1) + _dot(qkm_ref[d, vh // 2, cc], v2)
            o_ref[rows, vh * hd:(vh + 2) * hd] = o2.astype(o_ref.dtype)
        for (d, vh, cc), v_newb in zip(units, v_news):
            scale = et_ref[d, cc][:, nv * d + vh:nv * d + vh + 1]
            s_ref[d, vh] = s_ref[d, vh] * scale + lax.dot_general(
                kd_ref[d, vh, cc], v_newb, TN_DIMS, preferred_element_type=F32)
        return carry

    lax.fori_loop(0, n, solve_chunk, 0)
    lax.fori_loop(0, n, scan_chunk, 0)


def _gdn_core(act, small):
    L = act.shape[0]
    ck = GDN_CHUNK
    rb = min(GDN_ROWS, L)
    nb = L // rb
    nv = GDN_V_HEADS
    hd = GDN_HEAD_DIM
    n = rb // ck
    g_row = small[:, :4 * nv].reshape(L // ck, ck, 4 * nv).transpose(0, 2, 1)
    col_id = jnp.arange(SMALL_CH)[:, None]
    expand = lambda w: jnp.stack([col_id == jnp.arange(nv * w)[None, :] // w + nv * d
                                  for d in range(2)]).astype(BF16)
    ex, ex_half = expand(hd), expand(ck)
    fwd = lambda b: (b, 0)
    bwd = lambda b: (nb - 1 - b, 0)
    return pl.pallas_call(
        functools.partial(_gdn_core_kernel, rb=rb, ck=ck),
        grid=(nb,),
        in_specs=[
            pl.BlockSpec((rb, CONV_CH), fwd),
            pl.BlockSpec((rb, CONV_CH), bwd),
            pl.BlockSpec((rb, SMALL_CH), fwd),
            pl.BlockSpec((rb, SMALL_CH), bwd),
            pl.BlockSpec((n, 4 * nv, ck), lambda b: (b, 0, 0)),
            pl.BlockSpec((n, 4 * nv, ck), lambda b: (nb - 1 - b, 0, 0)),
            _resident(ex.shape),
            _resident(ex_half.shape),
        ],
        out_specs=[pl.BlockSpec((rb, GDN_VALUE_DIM), fwd), pl.BlockSpec((rb, GDN_VALUE_DIM), bwd)],
        out_shape=[jax.ShapeDtypeStruct((L, GDN_VALUE_DIM), BF16)] * 2,
        scratch_shapes=[
            pltpu.VMEM((2, nv, hd, hd), F32),
            pltpu.VMEM((2, nv, n, ck, hd), F32),
            pltpu.VMEM((2, nv, n, 2 * ck, hd), BF16),
            pltpu.VMEM((2, nv, n, ck, hd), BF16),
            pltpu.VMEM((2, nv // 2, n, ck, 2 * ck), BF16),
            pltpu.VMEM((2, n, 1, SMALL_CH), F32),
        ],
        compiler_params=_params("arbitrary"),
        name="gdn_core",
    )(act, act, small, small, g_row, g_row, ex, ex_half)


def _gdn_out_kernel(of_ref, ob_ref, z_ref, h_ref, hnw_ref, wo_ref, pnw_ref, o_ref):
    o = of_ref[...].astype(F32) + ob_ref[...].astype(F32)
    z = z_ref[...].astype(F32)
    hd = GDN_HEAD_DIM
    hnw = hnw_ref[...]
    parts = []
    for v in range(GDN_VALUE_DIM // hd):
        blk = o[:, v * hd:(v + 1) * hd]
        parts.append(_rms(blk, hnw) * _silu(z[:, v * hd:(v + 1) * hd]))
    m = _dot(jnp.concatenate(parts, axis=1).astype(BF16), wo_ref[...])
    o_ref[...] = h_ref[...] + _rms(m, pnw_ref[...])


def _gdn_layer(h, nw_pre, nw_post, w_in, conv_w, conv_b, dt_bias, a_log, norm_w, w_out):
    wc, wz, ws, cw, cb = _split_in_weights(w_in, conv_w, conv_b, conv_first=True)
    act, z, small = _inproj(h, nw_pre.reshape(1, -1), wc, wz, ws, cw, cb, _pad_small(dt_bias), _pad_small(a_log),
                            gdn=True)
    of, ob = _gdn_core(act, small)
    consts = (norm_w.reshape(1, -1), w_out.astype(BF16), nw_post.reshape(1, -1))
    return _mixer_out_call(_gdn_out_kernel, "gdn_out", h, (of, ob, z), consts)


def _s5_prep_kernel(lre_ref, lim_ref, lst_ref, bre_ref, bim_ref, lre4_ref, lim4_ref, lst4_ref,
                    bbr_ref, bbi_ref, pw_ref):
    lre, lim = lre_ref[...], lim_ref[...]
    step = jnp.exp(lst_ref[...])
    mag, ang = jnp.exp(lre * step), lim * step
    lbr, lbi = mag * jnp.cos(ang), mag * jnp.sin(ang)
    inv_den = 1.0 / (lre * lre + lim * lim)
    ar, ai, br, bi = lbr - 1.0, lbi, lre * inv_den, -lim * inv_den
    zr, zi = ar * br - ai * bi, ar * bi + ai * br
    b_re, b_im = bre_ref[...], bim_ref[...]
    bbr_ref[...] = zr * b_re - zi * b_im
    bbi_ref[...] = zr * b_im + zi * b_re
    r = lax.broadcasted_iota(jnp.int32, (2 * S5_TILE, S5_LANES), 0)
    coarse = r < S5_TILE
    rr = jnp.where(coarse, r, r - S5_TILE).astype(F32)
    mult = jnp.where(coarse, float(S5_TILE), 1.0)
    for d in range(2):
        k = mult * (rr + 1.0 if d == 0 else float(S5_TILE) - rr)
        st = jnp.exp(lst4_ref[d])
        m, a = jnp.exp(k * (lre4_ref[d] * st)), k * (lim4_ref[d] * st)
        pw_ref[d, 0] = m * jnp.cos(a)
        pw_ref[d, 1] = m * jnp.sin(a)


def _s5_prep(lam_re, lam_im, log_step, b_re, b_im):
    g, n, c = S5_GROUPS, S5_STATE, S5_GROUP
    expand = lambda t: jnp.repeat(t, c, axis=-1)
    lst = jnp.broadcast_to(log_step[:, :, None], (2, g, n))
    flat = lambda t: t.reshape(2, 1, g * n)
    full = lambda s: pl.BlockSpec(s, lambda: (0,) * len(s))
    shapes = [(2, g, n * c)] * 3 + [(1, g, n * c)] * 2 + [(2, 1, g * n)] * 3
    return pl.pallas_call(
        _s5_prep_kernel,
        in_specs=[full(s) for s in shapes],
        out_specs=[full((2, g, n * c)), full((2, g, n * c)), full((2, 2, 2 * S5_TILE, g * n))],
        out_shape=[
            jax.ShapeDtypeStruct((2, g, n * c), F32),
            jax.ShapeDtypeStruct((2, g, n * c), F32),
            jax.ShapeDtypeStruct((2, 2, 2 * S5_TILE, g * n), F32),
        ],
        name="s5_prep",
    )(expand(lam_re), expand(lam_im), expand(lst), b_re.reshape(1, g, n * c), b_im.reshape(1, g, n * c),
      flat(lam_re), flat(lam_im), flat(lst))


def _s5_core_kernel(hf_ref, hb_ref, nw_ref, win_ref, wout_ref, pw_ref, yf_ref, yb_ref, car_ref, *, rt):
    @pl.when(pl.program_id(0) == 0)
    def _():
        car_ref[...] = jnp.zeros_like(car_ref)

    nw = nw_ref[...]
    ts = S5_TILE
    nt = rt // ts
    w = S5_LANES // S5_KB
    cin = D_MODEL // S5_KB
    rowi = lax.broadcasted_iota(jnp.int32, (ts, w), 0)
    tile = lambda v: jnp.broadcast_to(v, (ts, w))

    def cmul_add(a_r, a_i, s_r, s_i, x_r, x_i):
        return a_r * s_r - a_i * s_i + x_r, a_r * s_i + a_i * s_r + x_i

    jobs = [(d, kb) for d in range(2) for kb in range(S5_KB)]
    refs = ((hf_ref, yf_ref), (hb_ref, yb_ref))
    hn = [[_rms(refs[d][0][:, t * D_MODEL:(t + 1) * D_MODEL], nw).astype(BF16) for t in range(ts)]
          for d in range(2)]

    def drive(d, kb):
        lhs = jnp.concatenate([hn[d][t][:, kb * cin:(kb + 1) * cin] for t in range(ts)], axis=0)
        return _dot(lhs, win_ref[d, kb])

    x_next = drive(*jobs[0])
    for idx, (d, kb) in enumerate(jobs):
        x = x_next
        if idx + 1 < len(jobs):
            x_next = drive(*jobs[idx + 1])
        y_ref = refs[d][1]
        xr = [x[t * rt:(t + 1) * rt, :w] for t in range(ts)]
        xi = [x[t * rt:(t + 1) * rt, w:] for t in range(ts)]
        p_r = pw_ref[d, 0, 0:ts, kb * w:(kb + 1) * w]
        p_i = pw_ref[d, 1, 0:ts, kb * w:(kb + 1) * w]
        one = ts if d == 0 else 2 * ts - 1
        a_r = pw_ref[d, 0, one:one + 1, kb * w:(kb + 1) * w]
        a_i = pw_ref[d, 1, one:one + 1, kb * w:(kb + 1) * w]
        order = list(range(ts)) if d == 0 else list(range(ts - 1, -1, -1))

        e_r, e_i = xr[order[0]], xi[order[0]]
        for t in order[1:]:
            e_r, e_i = cmul_add(a_r, a_i, e_r, e_i, xr[t], xi[t])

        consts = []
        for k in (1, 2, 4):
            src = k - 1 if d == 0 else ts - k
            valid = rowi >= k if d == 0 else rowi <= ts - 1 - k
            consts.append((jnp.where(valid, tile(p_r[src:src + 1]), 0.0),
                           jnp.where(valid, tile(p_i[src:src + 1]), 0.0)))
        last, edge, back = (ts - 1, 0, 1) if d == 0 else (0, ts - 1, ts - 1)
        c_r, c_i = car_ref[d, kb, 0], car_ref[d, kb, 1]
        in_r, in_i = [None] * nt, [None] * nt
        for jj in range(nt):
            j = jj if d == 0 else nt - 1 - jj
            s_r, s_i = e_r[j * ts:(j + 1) * ts], e_i[j * ts:(j + 1) * ts]
            for k, (k_r, k_i) in zip((1, 2, 4), consts):
                sh = k if d == 0 else ts - k
                t_r, t_i = pltpu.roll(s_r, sh, 0), pltpu.roll(s_i, sh, 0)
                s_r, s_i = s_r + k_r * t_r - k_i * t_i, s_i + k_r * t_i + k_i * t_r
            s_r, s_i = s_r + p_r * c_r - p_i * c_i, s_i + p_r * c_i + p_i * c_r
            in_r[j] = jnp.where(rowi == edge, c_r, pltpu.roll(s_r, back, 0))
            in_i[j] = jnp.where(rowi == edge, c_i, pltpu.roll(s_i, back, 0))
            c_r, c_i = tile(s_r[last:last + 1]), tile(s_i[last:last + 1])
        car_ref[d, kb, 0] = c_r
        car_ref[d, kb, 1] = c_i

        s_r, s_i = jnp.concatenate(in_r, axis=0), jnp.concatenate(in_i, axis=0)
        states = [None] * ts
        for t in order:
            s_r, s_i = cmul_add(a_r, a_i, s_r, s_i, xr[t], xi[t])
            states[t] = jnp.concatenate([s_r, s_i], axis=1).astype(BF16)
        yk = _dot(jnp.concatenate(states, axis=0), wout_ref[d, kb])
        for t in range(ts):
            y_ref[:, t * D_MODEL + kb * cin:t * D_MODEL + (kb + 1) * cin] = (
                yk[t * rt:(t + 1) * rt].astype(y_ref.dtype))


def _s5_block_diag(t):
    per = S5_GROUPS // S5_KB
    a, b = t.shape[2], t.shape[3]
    t = t.reshape(2, S5_KB, per, a, b)
    eye = jnp.eye(per, dtype=t.dtype)
    return jnp.einsum('dkgab,gh->dkgahb', t, eye).reshape(2, S5_KB, per * a, per * b)


def _s5_core(h, nw, bb_r, bb_i, c_re, c_im, pw):
    L = h.shape[0]
    rb = min(S5_ROWS, L)
    nb = L // rb
    g, n, c = S5_GROUPS, S5_STATE, S5_GROUP
    to_in = lambda t: _s5_block_diag(t.reshape(2, g, n, c).transpose(0, 1, 3, 2))
    w_in = jnp.concatenate([to_in(bb_r), to_in(bb_i)], axis=-1).astype(BF16)
    to_out = lambda t: _s5_block_diag(t.transpose(0, 1, 3, 2))
    w_out = jnp.concatenate([to_out(c_re), to_out(-c_im)], axis=-2).astype(BF16)
    w = S5_LANES // S5_KB
    ts = S5_TILE
    rt = rb // ts
    hv = h.reshape(L // ts, ts * D_MODEL)
    fwd = lambda b: (b, 0)
    bwd = lambda b: (nb - 1 - b, 0)
    yf, yb = pl.pallas_call(
        functools.partial(_s5_core_kernel, rt=rt),
        grid=(nb,),
        in_specs=[
            pl.BlockSpec((rt, ts * D_MODEL), fwd),
            pl.BlockSpec((rt, ts * D_MODEL), bwd),
            _resident((1, D_MODEL)),
            _resident(w_in.shape),
            _resident(w_out.shape),
            _resident(pw.shape),
        ],
        out_specs=[pl.BlockSpec((rt, ts * D_MODEL), fwd), pl.BlockSpec((rt, ts * D_MODEL), bwd)],
        out_shape=[jax.ShapeDtypeStruct((L // ts, ts * D_MODEL), BF16)] * 2,
        scratch_shapes=[pltpu.VMEM((2, S5_KB, 2, ts, w), F32)],
        compiler_params=_params("arbitrary"),
        name="s5_core",
    )(hv, hv, nw, w_in, w_out, pw)
    return yf.reshape(L, D_MODEL), yb.reshape(L, D_MODEL)


def _s5_out_kernel(yf_ref, yb_ref, h_ref, nw_ref, dsk_ref, wg_ref, bg_ref, pnw_ref, o_ref):
    h = h_ref[...]
    hn = _rms(h, nw_ref[...])
    y = yf_ref[...].astype(F32) + yb_ref[...].astype(F32) + dsk_ref[...] * hn
    y = jax.nn.gelu(y)
    vg = _dot(y.astype(BF16), wg_ref[...]) + bg_ref[...]
    m = vg[:, :D_MODEL] * jax.nn.sigmoid(vg[:, D_MODEL:])
    o_ref[...] = h + _rms(m, pnw_ref[...])


def _s5_layer(h, nw_pre, nw_post, lam_re, lam_im, log_step, b_re, b_im, c_re, c_im, d_skip, w_glu, b_glu):
    nw = nw_pre.reshape(1, -1)
    bb_r, bb_i, pw = _s5_prep(lam_re, lam_im, log_step, b_re, b_im)
    yf, yb = _s5_core(h, nw, bb_r, bb_i, c_re, c_im, pw)
    consts = (nw, d_skip.reshape(1, -1), w_glu.astype(BF16), b_glu.reshape(1, -1), nw_post.reshape(1, -1))
    return _mixer_out_call(_s5_out_kernel, "s5_out", h, (yf, yb), consts)


def _ffn_kernel(h_ref, nw_ref, wgu_ref, wd_ref, pnw_ref, o_ref, act_ref, *, fchunk):
    h = h_ref[...]
    hn = _rms(h, nw_ref[...]).astype(BF16)
    for c in range(D_FF // fchunk):
        gate = _dot(hn, wgu_ref[:, c * fchunk:(c + 1) * fchunk])
        up = _dot(hn, wgu_ref[:, D_FF + c * fchunk:D_FF + (c + 1) * fchunk])
        act_ref[:, c * fchunk:(c + 1) * fchunk] = (_silu(gate) * up).astype(BF16)
    f = _dot(act_ref[...], wd_ref[...])
    o_ref[...] = h + _rms(f, pnw_ref[...])


def _ffn(h, nw_pre, nw_post, w_gate_up, w_down):
    L = h.shape[0]
    tm = min(ROW_TILE, L)
    tiled = pl.BlockSpec((tm, D_MODEL), lambda i: (i, 0))
    return pl.pallas_call(
        functools.partial(_ffn_kernel, fchunk=256),
        grid=(L // tm,),
        in_specs=[tiled, _resident((1, D_MODEL)), _resident((D_MODEL, 2 * D_FF)),
                  _resident((D_FF, D_MODEL)), _resident((1, D_MODEL))],
        out_specs=tiled,
        out_shape=jax.ShapeDtypeStruct(h.shape, F32),
        scratch_shapes=[pltpu.VMEM((tm, D_FF), BF16)],
        compiler_params=_params("arbitrary"),
        name="ffn",
    )(h, nw_pre.reshape(1, -1), w_gate_up.astype(BF16), w_down.astype(BF16), nw_post.reshape(1, -1))


def kernel(x, norm_w, ssd_w_in, ssd_conv_w, ssd_conv_b, ssd_dt_bias, ssd_a_log, ssd_d, ssd_norm_w, ssd_w_out, gdn_w_in, gdn_conv_w, gdn_conv_b, gdn_dt_bias, gdn_a_log, gdn_norm_w, gdn_w_out, s5_lam_re, s5_lam_im, s5_log_step, s5_b_re, s5_b_im, s5_c_re, s5_c_im, s5_d, s5_w_glu, s5_b_glu, ffn_w_gate_up, ffn_w_down):
    bt, L, d = x.shape
    assert bt == 1 and d == D_MODEL
    h = x.reshape(L, d)
    for i in range(norm_w.shape[0]):
        kind, j = i % 3, i // 3
        if kind == 0:
            h = _ssd_layer(h, norm_w[i, 0], norm_w[i, 1], ssd_w_in[j], ssd_conv_w[j], ssd_conv_b[j],
                           ssd_dt_bias[j], ssd_a_log[j], ssd_d[j], ssd_norm_w[j], ssd_w_out[j])
        elif kind == 1:
            h = _gdn_layer(h, norm_w[i, 0], norm_w[i, 1], gdn_w_in[j], gdn_conv_w[j], gdn_conv_b[j],
                           gdn_dt_bias[j], gdn_a_log[j], gdn_norm_w[j], gdn_w_out[j])
        else:
            h = _s5_layer(h, norm_w[i, 0], norm_w[i, 1], s5_lam_re[j], s5_lam_im[j], s5_log_step[j],
                          s5_b_re[j], s5_b_im[j], s5_c_re[j], s5_c_im[j], s5_d[j], s5_w_glu[j], s5_b_glu[j])
        h = _ffn(h, norm_w[i, 2], norm_w[i, 3], ffn_w_gate_up[i], ffn_w_down[i])
    return h.reshape(bt, L, d)
```

```python
import functools

import jax
import jax.numpy as jnp
from jax import lax
from jax.experimental import pallas as pl
from jax.experimental.pallas import tpu as pltpu

F32 = jnp.float32
BF16 = jnp.bfloat16
HIGHEST = lax.Precision.HIGHEST

D_MODEL = 1024
NORM_EPS = 1e-6
CONV_TAPS = 5
CONV_CH = 4096
Z_CH = 2048
SMALL_CH = 128
HALO = 8

SSD_HEADS = 32
SSD_HEAD_DIM = 64
SSD_GROUPS = 8
SSD_STATE = 128
SSD_CHUNK = 128
SSD_D_INNER = 2048

GDN_HEAD_DIM = 128
GDN_QK_HEADS = 8
GDN_V_HEADS = 16
GDN_KEY_DIM = GDN_QK_HEADS * GDN_HEAD_DIM
GDN_CHUNK = 64
GDN_VALUE_DIM = 2048

S5_GROUPS = 64
S5_GROUP = 16
S5_STATE = 64
S5_LANES = S5_GROUPS * S5_STATE
S5_KB = 4
S5_TILE = 8

D_FF = 2816

ROW_TILE = 512
GDN_ROWS = 256
S5_ROWS = 512
VMEM_LIMIT_BYTES = 56 * 1024 * 1024

NT_DIMS = (((1,), (1,)), ((), ()))
TN_DIMS = (((0,), (0,)), ((), ()))


def _params(*semantics):
    return pltpu.CompilerParams(dimension_semantics=semantics, vmem_limit_bytes=VMEM_LIMIT_BYTES)


def _resident(shape):
    nd = len(shape)
    return pl.BlockSpec(shape, lambda *_: (0,) * nd, pipeline_mode=pl.Buffered(1))


def _rms(x, w):
    return x * lax.rsqrt(jnp.mean(x * x, axis=-1, keepdims=True) + NORM_EPS) * w


def _silu(x):
    return x * jax.nn.sigmoid(x)


def _dot(a, b):
    return jnp.dot(a, b, preferred_element_type=F32)


def _inproj_kernel(h_ref, hp_ref, hx_ref, nw_ref, wc_ref, wz_ref, ws_ref, cw_ref, cb_ref, sb_ref, sa_ref,
                   act_ref, z_ref, s_ref, hs_ref, ys_ref, *, tm, nblk, cchunk, gdn):
    i = pl.program_id(0)
    nw = nw_ref[...]
    curb = _rms(h_ref[...], nw).astype(BF16)
    z_ref[...] = _dot(curb, wz_ref[...]).astype(z_ref.dtype)
    raw = _dot(curb, ws_ref[...]) + sb_ref[...]
    if gdn:
        is_decay = lax.broadcasted_iota(jnp.int32, raw.shape, 1) < 2 * GDN_V_HEADS
        s_ref[...] = jnp.where(is_decay, -jnp.exp(sa_ref[...]) * jax.nn.softplus(raw), jax.nn.sigmoid(raw))
    else:
        s_ref[...] = jax.nn.softplus(raw)
    pad = CONV_TAPS // 2
    hd = GDN_HEAD_DIM
    lanes = 128
    rows = tm + 2 * HALO
    seg = rows // 8
    for s in range(D_MODEL // lanes):
        ls = slice(s * lanes, (s + 1) * lanes)
        hs_ref[s, 0:HALO, :] = hp_ref[:, ls]
        hs_ref[s, HALO:HALO + tm, :] = h_ref[:, ls]
        hs_ref[s, HALO + tm:rows, :] = hx_ref[:, ls]
    ext = jnp.concatenate(
        [jnp.concatenate([hs_ref[s, pl.ds(j, 8, stride=seg), :] for s in range(D_MODEL // lanes)], axis=1)
         for j in range(seg)], axis=0)
    r = lax.broadcasted_iota(jnp.int32, (rows, 1), 0)
    orig = (r % 8) * seg + r // 8
    inside = jnp.logical_and(jnp.logical_or(orig >= HALO, i > 0),
                             jnp.logical_or(orig < HALO + tm, i < nblk - 1))
    ext = jnp.where(inside, _rms(ext, nw), 0.0).astype(BF16)
    for c in range(CONV_CH // cchunk):
        sl = slice(c * cchunk, (c + 1) * cchunk)
        p = _dot(ext, wc_ref[:, sl])
        head = [pltpu.roll(p[(seg - pad + m) * 8:(seg - pad + m + 1) * 8], 1, 0) for m in range(pad)]
        tail = [pltpu.roll(p[m * 8:(m + 1) * 8], 7, 0) for m in range(pad)]
        p_ext = jnp.concatenate(head + [p] + tail, axis=0)
        acc = cb_ref[:, sl]
        for k in range(CONV_TAPS):
            acc = acc + cw_ref[k:k + 1, sl] * p_ext[8 * k:8 * k + rows]
        y = _silu(acc)
        if gdn and c * cchunk < 2 * GDN_KEY_DIM:
            parts = []
            for s in range(cchunk // hd):
                blk = y[:, s * hd:(s + 1) * hd]
                inv = lax.rsqrt(jnp.sum(blk * blk, axis=-1, keepdims=True) + NORM_EPS)
                if c * cchunk + s * hd < GDN_KEY_DIM:
                    inv = inv * (hd ** -0.5)
                parts.append(blk * inv)
            y = jnp.concatenate(parts, axis=1)
        for s in range(cchunk // lanes):
            for j in range(seg):
                ys_ref[s, pl.ds(j, 8, stride=seg), :] = y[8 * j:8 * j + 8, s * lanes:(s + 1) * lanes]
        for s in range(cchunk // lanes):
            act_ref[:, c * cchunk + s * lanes:c * cchunk + (s + 1) * lanes] = (
                ys_ref[s, HALO:HALO + tm, :].astype(act_ref.dtype))


def _inproj(h, nw, wc, wz, ws, conv_w, conv_b, s_bias, s_alog, gdn):
    L = h.shape[0]
    tm = min(ROW_TILE, L)
    nblk = L // tm
    per = tm // HALO
    last = L // HALO - 1
    kern = functools.partial(_inproj_kernel, tm=tm, nblk=nblk, cchunk=512, gdn=gdn)
    return pl.pallas_call(
        kern,
        grid=(nblk,),
        in_specs=[
            pl.BlockSpec((tm, D_MODEL), lambda i: (i, 0)),
            pl.BlockSpec((HALO, D_MODEL), lambda i: (jnp.maximum(i * per - 1, 0), 0)),
            pl.BlockSpec((HALO, D_MODEL), lambda i: (jnp.minimum((i + 1) * per, last), 0)),
            _resident((1, D_MODEL)),
            _resident((D_MODEL, CONV_CH)),
            _resident((D_MODEL, Z_CH)),
            _resident((D_MODEL, SMALL_CH)),
            _resident((8, CONV_CH)),
            _resident((1, CONV_CH)),
            _resident((1, SMALL_CH)),
            _resident((1, SMALL_CH)),
        ],
        out_specs=[
            pl.BlockSpec((tm, CONV_CH), lambda i: (i, 0)),
            pl.BlockSpec((tm, Z_CH), lambda i: (i, 0)),
            pl.BlockSpec((tm, SMALL_CH), lambda i: (i, 0)),
        ],
        out_shape=[
            jax.ShapeDtypeStruct((L, CONV_CH), BF16),
            jax.ShapeDtypeStruct((L, Z_CH), BF16),
            jax.ShapeDtypeStruct((L, SMALL_CH), F32),
        ],
        scratch_shapes=[pltpu.VMEM((D_MODEL // 128, tm + 2 * HALO, 128), F32),
                        pltpu.VMEM((512 // 128, tm + 2 * HALO, 128), F32)],
        compiler_params=_params("arbitrary"),
        name="inproj_conv",
    )(h, h, h, nw, wc, wz, ws, conv_w, conv_b, s_bias, s_alog)


def _pad_small(t):
    t = t.reshape(1, -1)
    return jnp.pad(t, ((0, 0), (0, SMALL_CH - t.shape[1])))


def _split_in_weights(w_in, conv_w, conv_b, conv_first):
    if conv_first:
        wc, wz = w_in[:, :CONV_CH], w_in[:, CONV_CH:CONV_CH + Z_CH]
    else:
        wz, wc = w_in[:, :Z_CH], w_in[:, Z_CH:Z_CH + CONV_CH]
    ws = w_in[:, CONV_CH + Z_CH:]
    ws = jnp.pad(ws, ((0, 0), (0, SMALL_CH - ws.shape[1])))
    cw = jnp.pad(conv_w, ((0, 8 - CONV_TAPS), (0, 0)))
    return wc.astype(BF16), wz.astype(BF16), ws.astype(BF16), cw, conv_b.reshape(1, CONV_CH)


def _ssd_core_kernel(xf_ref, xb_ref, dtf_ref, dtb_ref, alog_ref, dskip_ref, exb_ref, exf_ref,
                     yf_ref, yb_ref, sf_ref, sb_ref, *, q):
    @pl.when(pl.program_id(0) == 0)
    def _():
        sf_ref[...] = jnp.zeros_like(sf_ref)
        sb_ref[...] = jnp.zeros_like(sb_ref)

    row = lax.broadcasted_iota(jnp.int32, (q, q), 0)
    col = lax.broadcasted_iota(jnp.int32, (q, q), 1)
    lower = row >= col
    upper = row <= col
    lower_f = lower.astype(F32)
    upper_f = upper.astype(F32)
    a = -jnp.exp(alog_ref[...])
    p = SSD_HEAD_DIM
    per_group = SSD_HEADS // SSD_GROUPS
    gw = per_group * p
    pw = 2 * p
    b_off = SSD_D_INNER
    c_off = SSD_D_INNER + SSD_GROUPS * SSD_STATE
    left = lax.broadcasted_iota(jnp.int32, (q, pw), 1) < p

    dirs = ((xf_ref, dtf_ref, yf_ref, sf_ref), (xb_ref, dtb_ref, yb_ref, sb_ref))
    pre = []
    for d, (x_ref, dt_ref, y_ref, s_ref) in enumerate(dirs):
        dt = dt_ref[...]
        da = dt * a
        tri_c, tri_r = (lower_f, upper_f) if d == 0 else (upper_f, lower_f)
        cs = jnp.dot(tri_c, da, precision=HIGHEST, preferred_element_type=F32)
        cs_t = jnp.dot(da.T, tri_r, precision=HIGHEST, preferred_element_type=F32)
        tot = cs[q - 1:q, :] if d == 0 else cs[0:1, :]
        exb = exb_ref[d]
        dt_x = _dot(dt.astype(BF16), exb)
        e_x = _dot(jnp.exp(cs).astype(BF16), exb)
        w_x = _dot((dt * jnp.exp(tot - cs)).astype(BF16), exb)
        tot_x = jnp.dot(jnp.broadcast_to(tot, (8, SMALL_CH)), exf_ref[d], precision=HIGHEST,
                        preferred_element_type=F32)[0:1]
        pre.append((cs, cs_t, dt_x, e_x, w_x, jnp.exp(tot_x)))

    def front(d, g):
        x_ref, s_ref = dirs[d][0], dirs[d][3]
        bmat = x_ref[:, b_off + g * SSD_STATE:b_off + (g + 1) * SSD_STATE]
        cmat = x_ref[:, c_off + g * SSD_STATE:c_off + (g + 1) * SSD_STATE]
        cb = lax.dot_general(cmat, bmat, NT_DIMS, preferred_element_type=F32)
        yoff = _dot(cmat, s_ref[g].astype(BF16))
        return bmat, cb, yoff

    jobs = [(d, g) for d in range(2) for g in range(SSD_GROUPS)]
    ahead = front(*jobs[0])
    for idx, (d, g) in enumerate(jobs):
        bmat, cb, yoff = ahead
        if idx + 1 < len(jobs):
            ahead = front(*jobs[idx + 1])
        xg = dirs[d][0][:, g * gw:(g + 1) * gw].astype(F32)
        cs, cs_t, dt_x, e_x, w_x, cdec_x = pre[d]
        mask = lower if d == 0 else upper
        y_ref = dirs[d][2]
        for pr in range(per_group // 2):
            h0 = per_group * g + 2 * pr
            lms = []
            for hh in (h0, h0 + 1):
                ch = hh + SSD_HEADS * d
                lms.append(cb * jnp.exp(jnp.where(mask, cs[:, ch:ch + 1] - cs_t[ch:ch + 1, :], -jnp.inf)))
            m_pair = jnp.concatenate(lms, axis=1).astype(BF16)
            sl = slice(h0 * p, h0 * p + pw)
            lo = slice(2 * pr * p, 2 * pr * p + pw)
            xp = xg[:, lo]
            xdt = xp * dt_x[:, sl]
            rhs = jnp.concatenate([jnp.where(left, xdt, 0.0), jnp.where(left, 0.0, xdt)], axis=0).astype(BF16)
            y = _dot(m_pair, rhs) + yoff[:, lo] * e_x[:, sl]
            if d == 0:
                y = y + xp * dskip_ref[:, sl]
            y_ref[:, sl] = y.astype(y_ref.dtype)
        s_ref = dirs[d][3]
        gs = slice(g * gw, (g + 1) * gw)
        new = lax.dot_general(bmat, (xg * w_x[:, gs]).astype(BF16), TN_DIMS, preferred_element_type=F32)
        s_ref[g] = s_ref[g] * cdec_x[:, gs] + new


def _ssd_core(act, small, a_log, d_skip):
    L = act.shape[0]
    q = SSD_CHUNK
    nc = L // q
    alog = _pad_small(a_log)
    dskip = jnp.repeat(d_skip, SSD_HEAD_DIM).reshape(1, SSD_D_INNER)
    head_of_channel = jnp.arange(SSD_D_INNER) // SSD_HEAD_DIM
    col_id = jnp.arange(SMALL_CH)
    expand = jnp.stack([col_id[:, None] == head_of_channel[None, :] + SSD_HEADS * d for d in range(2)])
    gw = SSD_D_INNER // SSD_GROUPS
    fwd = lambda c: (c, 0)
    bwd = lambda c: (nc - 1 - c, 0)
    return pl.pallas_call(
        functools.partial(_ssd_core_kernel, q=q),
        grid=(nc,),
        in_specs=[
            pl.BlockSpec((q, CONV_CH), fwd),
            pl.BlockSpec((q, CONV_CH), bwd),
            pl.BlockSpec((q, SMALL_CH), fwd),
            pl.BlockSpec((q, SMALL_CH), bwd),
            _resident((1, SMALL_CH)),
            _resident((1, SSD_D_INNER)),
            _resident((2, SMALL_CH, SSD_D_INNER)),
            _resident((2, SMALL_CH, SSD_D_INNER)),
        ],
        out_specs=[pl.BlockSpec((q, SSD_D_INNER), fwd), pl.BlockSpec((q, SSD_D_INNER), bwd)],
        out_shape=[jax.ShapeDtypeStruct((L, SSD_D_INNER), BF16)] * 2,
        scratch_shapes=[pltpu.VMEM((SSD_GROUPS, SSD_STATE, gw), F32)] * 2,
        compiler_params=_params("arbitrary"),
        name="ssd_core",
    )(act, act, small, small, alog, dskip, expand.astype(BF16), expand.astype(F32))


def _ssd_out_kernel(yf_ref, yb_ref, z_ref, h_ref, gnw_ref, wo_ref, pnw_ref, o_ref):
    y = yf_ref[...].astype(F32) + yb_ref[...].astype(F32)
    yz = y * _silu(z_ref[...].astype(F32))
    gw = SSD_D_INNER // SSD_GROUPS
    parts = []
    for g in range(SSD_GROUPS):
        blk = yz[:, g * gw:(g + 1) * gw]
        parts.append(blk * lax.rsqrt(jnp.mean(blk * blk, axis=-1, keepdims=True) + NORM_EPS))
    yn = (jnp.concatenate(parts, axis=1) * gnw_ref[...]).astype(BF16)
    m = _dot(yn, wo_ref[...])
    o_ref[...] = h_ref[...] + _rms(m, pnw_ref[...])


def _mixer_out_call(kern, name, h, streams, consts):
    L = h.shape[0]
    tm = min(ROW_TILE, L)
    tiled = lambda a: pl.BlockSpec((tm, a.shape[1]), lambda i: (i, 0))
    return pl.pallas_call(
        kern,
        grid=(L // tm,),
        in_specs=[tiled(a) for a in streams] + [tiled(h)] + [_resident(c.shape) for c in consts],
        out_specs=tiled(h),
        out_shape=jax.ShapeDtypeStruct(h.shape, F32),
        compiler_params=_params("arbitrary"),
        name=name,
    )(*streams, h, *consts)


def _ssd_layer(h, nw_pre, nw_post, w_in, conv_w, conv_b, dt_bias, a_log, d_skip, norm_w, w_out):
    wc, wz, ws, cw, cb = _split_in_weights(w_in, conv_w, conv_b, conv_first=False)
    act, z, small = _inproj(h, nw_pre.reshape(1, -1), wc, wz, ws, cw, cb, _pad_small(dt_bias), _pad_small(a_log),
                            gdn=False)
    yf, yb = _ssd_core(act, small, a_log, d_skip)
    consts = (norm_w.reshape(1, -1), w_out.astype(BF16), nw_post.reshape(1, -1))
    return _mixer_out_call(_ssd_out_kernel, "ssd_out", h, (yf, yb, z), consts)


def _gdn_core_kernel(xf_ref, xb_ref, gcf_ref, gcb_ref, grf_ref, grb_ref, ex_ref, exh_ref,
                     of_ref, ob_ref, s_ref, u_ref, wq_ref, kd_ref, qkm_ref, et_ref, *, rb, ck):
    @pl.when(pl.program_id(0) == 0)
    def _():
        s_ref[...] = jnp.zeros_like(s_ref)

    n = rb // ck
    hd = GDN_HEAD_DIM
    nq, nv = GDN_QK_HEADS, GDN_V_HEADS
    k_off, v_off = nq * hd, 2 * nq * hd
    row = lax.broadcasted_iota(jnp.int32, (ck, ck), 0)
    col = lax.broadcasted_iota(jnp.int32, (ck, ck), 1)
    lower, upper = row >= col, row <= col
    lower_s, upper_s = row > col, row < col
    lower_f, upper_f = lower.astype(F32), upper.astype(F32)
    row2 = lax.broadcasted_iota(jnp.int32, (ck, 2 * ck), 0)
    lane2 = lax.broadcasted_iota(jnp.int32, (ck, 2 * ck), 1)
    left = lane2 < ck
    col2 = jnp.where(left, lane2, lane2 - ck)
    pair = lambda t: jnp.concatenate([t, t], axis=1)
    eye2 = (row2 == col2).astype(F32)

    def block_diag(p2):
        return jnp.concatenate([jnp.where(left, p2, 0.0), jnp.where(left, 0.0, p2)], axis=0).astype(BF16)

    def solve_chunk(ci, carry):
        pairs = []
        for d in range(2):
            x_ref, gc_ref, gr_ref = (xf_ref, gcf_ref, grf_ref) if d == 0 else (xb_ref, gcb_ref, grb_ref)
            cc = ci if d == 0 else n - 1 - ci
            rows = pl.ds(pl.multiple_of(cc * ck, ck), ck)
            tri_c, tri_r = (lower_f, upper_f) if d == 0 else (upper_f, lower_f)
            mask2, strict2 = (row2 >= col2, row2 > col2) if d == 0 else (row2 <= col2, row2 < col2)
            gb_c = gc_ref[rows, :]
            gb_r = gr_ref[cc]
            cs_c = jnp.dot(tri_c, gb_c, precision=HIGHEST, preferred_element_type=F32)
            cs_r = jnp.dot(gb_r, tri_r, precision=HIGHEST, preferred_element_type=F32)
            tot = cs_c[ck - 1:ck, :] if d == 0 else cs_c[0:1, :]
            et_ref[d, cc] = jnp.exp(tot)
            beta_al = pltpu.roll(gb_c, SMALL_CH - 2 * nv, axis=1)
            ex, ex_half = ex_ref[d], exh_ref[d]
            beta_b = beta_al.astype(BF16)
            beta_h = _dot(beta_b, ex_half)
            stacked = jnp.concatenate([beta_b, jnp.exp(cs_c).astype(BF16), jnp.exp(tot - cs_c).astype(BF16)], axis=0)
            stacked_x = _dot(stacked, ex)
            beta_x = stacked_x[:ck]
            e_x = stacked_x[ck:2 * ck]
            kf_x = stacked_x[2 * ck:]
            for hq in range(nq):
                qn = x_ref[rows, hq * hd:(hq + 1) * hd]
                kn = x_ref[rows, k_off + hq * hd:k_off + (hq + 1) * hd]
                gram = lax.dot_general(jnp.concatenate([kn, qn], axis=0), kn, NT_DIMS,
                                       preferred_element_type=F32)
                kk2, qk2 = pair(gram[:ck]), pair(gram[ck:])
                c0 = nv * d + 2 * hq
                gcs_c2 = jnp.where(left, cs_c[:, c0:c0 + 1], cs_c[:, c0 + 1:c0 + 2])
                gcs_r2 = jnp.concatenate([cs_r[c0:c0 + 1, :], cs_r[c0 + 1:c0 + 2, :]], axis=1)
                beta2 = beta_h[:, hq * 2 * ck:(hq + 1) * 2 * ck]
                decay2 = jnp.exp(jnp.where(mask2, gcs_c2 - gcs_r2, -jnp.inf))
                a2 = jnp.where(strict2, beta2 * kk2 * decay2, 0.0)
                qkm_ref[d, hq, cc] = (qk2 * decay2).astype(BF16)
                qf, kf = qn.astype(F32), kn.astype(F32)
                rhs = []
                for j in range(2):
                    vh, cg = 2 * hq + j, c0 + j
                    hs = slice(vh * hd, (vh + 1) * hd)
                    beta_c, e_g = beta_x[:, hs], e_x[:, hs]
                    vx = x_ref[rows, v_off + vh * hd:v_off + (vh + 1) * hd].astype(F32)
                    rhs.append(jnp.concatenate([vx * beta_c, kf * (beta_c * e_g)], axis=1).astype(BF16))
                    wq_ref[d, vh, cc, ck:2 * ck, :] = (qf * e_g).astype(BF16)
                    kd_ref[d, vh, cc] = (kf * kf_x[:, hs]).astype(BF16)
                zero = jnp.zeros_like(rhs[0])
                rhs2 = jnp.concatenate([jnp.concatenate([rhs[0], zero], axis=1),
                                        jnp.concatenate([zero, rhs[1]], axis=1)], axis=0)
                pairs.append((d, hq, cc, a2, rhs2))
        pws = [_dot(a2.astype(BF16), block_diag(a2)) for (_, _, _, a2, _) in pairs]
        tis = [eye2 - a2 for (_, _, _, a2, _) in pairs]
        for level in range(5):
            if level < 4:
                res = [_dot(jnp.concatenate([t, p], axis=0).astype(BF16), block_diag(p)) for t, p in zip(tis, pws)]
                tis = [t + r[:ck] for t, r in zip(tis, res)]
                pws = [r[ck:] for r in res]
            else:
                tis = [t + _dot(t.astype(BF16), block_diag(p)) for t, p in zip(tis, pws)]
        for (d, hq, cc, _, rhs2), t in zip(pairs, tis):
            sol = _dot(t.astype(BF16), rhs2)
            for j in range(2):
                vh = 2 * hq + j
                u_ref[d, vh, cc] = sol[:, 2 * j * hd:(2 * j + 1) * hd]
                wq_ref[d, vh, cc, 0:ck, :] = sol[:, (2 * j + 1) * hd:(2 * j + 2) * hd].astype(BF16)
        return carry

    def scan_chunk(ci, carry):
        units = [(d, vh, ci if d == 0 else n - 1 - ci) for d in range(2) for vh in range(nv)]
        boths = [_dot(wq_ref[d, vh, cc], s_ref[d, vh].astype(BF16)) for d, vh, cc in units]
        v_news = [(u_ref[d, vh, cc] - both[:ck]).astype(BF16) for (d, vh, cc), both in zip(units, boths)]
        for i in range(0, len(units), 2):
            d, vh, cc = units[i]
            o_ref = of_ref if d == 0 else ob_ref
            rows = pl.ds(pl.multiple_of(cc * ck, ck), ck)
            zero = jnp.zeros_like(v_news[i])
            v2 = jnp.concatenate([jnp.concatenate([v_news[i], zero], axis=1),
                                  jnp.concatenate([zero, v_news[i + 1]], axis=1)], axis=0)
            o2 = jnp.concatenate([boths[i][ck:], boths[i + 1][ck:]], axis=1) + _dot(qkm_ref[d, vh // 2, cc], v2)
            o_ref[rows, vh * hd:(vh + 2) * hd] = o2.astype(o_ref.dtype)
        for (d, vh, cc), v_newb in zip(units, v_news):
            scale = et_ref[d, cc][:, nv * d + vh:nv * d + vh + 1]
            s_ref[d, vh] = s_ref[d, vh] * scale + lax.dot_general(
                kd_ref[d, vh, cc], v_newb, TN_DIMS, preferred_element_type=F32)
        return carry

    lax.fori_loop(0, n, solve_chunk, 0)
    lax.fori_loop(0, n, scan_chunk, 0)


def _gdn_core(act, small):
    L = act.shape[0]
    ck = GDN_CHUNK
    rb = min(GDN_ROWS, L)
    nb = L // rb
    nv = GDN_V_HEADS
    hd = GDN_HEAD_DIM
    n = rb // ck
    g_row = small[:, :4 * nv].reshape(L // ck, ck, 4 * nv).transpose(0, 2, 1)
    col_id = jnp.arange(SMALL_CH)[:, None]
    expand = lambda w: jnp.stack([col_id == jnp.arange(nv * w)[None, :] // w + nv * d
                                  for d in range(2)]).astype(BF16)
    ex, ex_half = expand(hd), expand(ck)
    fwd = lambda b: (b, 0)
    bwd = lambda b: (nb - 1 - b, 0)
    return pl.pallas_call(
        functools.partial(_gdn_core_kernel, rb=rb, ck=ck),
        grid=(nb,),
        in_specs=[
            pl.BlockSpec((rb, CONV_CH), fwd),
            pl.BlockSpec((rb, CONV_CH), bwd),
            pl.BlockSpec((rb, SMALL_CH), fwd),
            pl.BlockSpec((rb, SMALL_CH), bwd),
            pl.BlockSpec((n, 4 * nv, ck), lambda b: (b, 0, 0)),
            pl.BlockSpec((n, 4 * nv, ck), lambda b: (nb - 1 - b, 0, 0)),
            _resident(ex.shape),
            _resident(ex_half.shape),
        ],
        out_specs=[pl.BlockSpec((rb, GDN_VALUE_DIM), fwd), pl.BlockSpec((rb, GDN_VALUE_DIM), bwd)],
        out_shape=[jax.ShapeDtypeStruct((L, GDN_VALUE_DIM), BF16)] * 2,
        scratch_shapes=[
            pltpu.VMEM((2, nv, hd, hd), F32),
            pltpu.VMEM((2, nv, n, ck, hd), F32),
            pltpu.VMEM((2, nv, n, 2 * ck, hd), BF16),
            pltpu.VMEM((2, nv, n, ck, hd), BF16),
            pltpu.VMEM((2, nv // 2, n, ck, 2 * ck), BF16),
            pltpu.VMEM((2, n, 1, SMALL_CH), F32),
        ],
        compiler_params=_params("arbitrary"),
        name="gdn_core",
    )(act, act, small, small, g_row, g_row, ex, ex_half)


def _gdn_out_kernel(of_ref, ob_ref, z_ref, h_ref, hnw_ref, wo_ref, pnw_ref, o_ref):
    o = of_ref[...].astype(F32) + ob_ref[...].astype(F32)
    z = z_ref[...].astype(F32)
    hd = GDN_HEAD_DIM
    hnw = hnw_ref[...]
    parts = []
    for v in range(GDN_VALUE_DIM // hd):
        blk = o[:, v * hd:(v + 1) * hd]
        parts.append(_rms(blk, hnw) * _silu(z[:, v * hd:(v + 1) * hd]))
    m = _dot(jnp.concatenate(parts, axis=1).astype(BF16), wo_ref[...])
    o_ref[...] = h_ref[...] + _rms(m, pnw_ref[...])


def _gdn_layer(h, nw_pre, nw_post, w_in, conv_w, conv_b, dt_bias, a_log, norm_w, w_out):
    wc, wz, ws, cw, cb = _split_in_weights(w_in, conv_w, conv_b, conv_first=True)
    act, z, small = _inproj(h, nw_pre.reshape(1, -1), wc, wz, ws, cw, cb, _pad_small(dt_bias), _pad_small(a_log),
                            gdn=True)
    of, ob = _gdn_core(act, small)
    consts = (norm_w.reshape(1, -1), w_out.astype(BF16), nw_post.reshape(1, -1))
    return _mixer_out_call(_gdn_out_kernel, "gdn_out", h, (of, ob, z), consts)


def _s5_prep_kernel(lre_ref, lim_ref, lst_ref, bre_ref, bim_ref, lre4_ref, lim4_ref, lst4_ref,
                    bbr_ref, bbi_ref, pw_ref):
    lre, lim = lre_ref[...], lim_ref[...]
    step = jnp.exp(lst_ref[...])
    mag, ang = jnp.exp(lre * step), lim * step
    lbr, lbi = mag * jnp.cos(ang), mag * jnp.sin(ang)
    inv_den = 1.0 / (lre * lre + lim * lim)
    ar, ai, br, bi = lbr - 1.0, lbi, lre * inv_den, -lim * inv_den
    zr, zi = ar * br - ai * bi, ar * bi + ai * br
    b_re, b_im = bre_ref[...], bim_ref[...]
    bbr_ref[...] = zr * b_re - zi * b_im
    bbi_ref[...] = zr * b_im + zi * b_re
    r = lax.broadcasted_iota(jnp.int32, (2 * S5_TILE, S5_LANES), 0)
    coarse = r < S5_TILE
    rr = jnp.where(coarse, r, r - S5_TILE).astype(F32)
    mult = jnp.where(coarse, float(S5_TILE), 1.0)
    for d in range(2):
        k = mult * (rr + 1.0 if d == 0 else float(S5_TILE) - rr)
        st = jnp.exp(lst4_ref[d])
        m, a = jnp.exp(k * (lre4_ref[d] * st)), k * (lim4_ref[d] * st)
        pw_ref[d, 0] = m * jnp.cos(a)
        pw_ref[d, 1] = m * jnp.sin(a)


def _s5_prep(lam_re, lam_im, log_step, b_re, b_im):
    g, n, c = S5_GROUPS, S5_STATE, S5_GROUP
    expand = lambda t: jnp.repeat(t, c, axis=-1)
    lst = jnp.broadcast_to(log_step[:, :, None], (2, g, n))
    flat = lambda t: t.reshape(2, 1, g * n)
    full = lambda s: pl.BlockSpec(s, lambda: (0,) * len(s))
    shapes = [(2, g, n * c)] * 3 + [(1, g, n * c)] * 2 + [(2, 1, g * n)] * 3
    return pl.pallas_call(
        _s5_prep_kernel,
        in_specs=[full(s) for s in shapes],
        out_specs=[full((2, g, n * c)), full((2, g, n * c)), full((2, 2, 2 * S5_TILE, g * n))],
        out_shape=[
            jax.ShapeDtypeStruct((2, g, n * c), F32),
            jax.ShapeDtypeStruct((2, g, n * c), F32),
            jax.ShapeDtypeStruct((2, 2, 2 * S5_TILE, g * n), F32),
        ],
        name="s5_prep",
    )(expand(lam_re), expand(lam_im), expand(lst), b_re.reshape(1, g, n * c), b_im.reshape(1, g, n * c),
      flat(lam_re), flat(lam_im), flat(lst))


def _s5_core_kernel(hf_ref, hb_ref, nw_ref, win_ref, wout_ref, pw_ref, yf_ref, yb_ref,
                    car_ref, hsc_ref, ysc_ref, *, rt):
    @pl.when(pl.program_id(0) == 0)
    def _():
        car_ref[...] = jnp.zeros_like(car_ref)

    nw = nw_ref[...]
    ts = S5_TILE
    nt = rt // ts
    w = S5_LANES // S5_KB
    cin = D_MODEL // S5_KB
    rowi = lax.broadcasted_iota(jnp.int32, (ts, w), 0)
    tile = lambda v: jnp.broadcast_to(v, (ts, w))

    def cmul_add(a_r, a_i, s_r, s_i, x_r, x_i):
        return a_r * s_r - a_i * s_i + x_r, a_r * s_i + a_i * s_r + x_i

    jobs = [(d, kb) for d in range(2) for kb in range(S5_KB)]
    refs = ((hf_ref, yf_ref), (hb_ref, yb_ref))
    lanes = 128
    nl = D_MODEL // lanes
    hn = []
    for d in range(2):
        for c in range(nl):
            hsc_ref[c] = refs[d][0][:, c * lanes:(c + 1) * lanes]
        hn.append([_rms(jnp.concatenate([hsc_ref[c, pl.ds(t, rt, stride=ts), :] for c in range(nl)], axis=1),
                        nw).astype(BF16) for t in range(ts)])

    def drive(d, kb):
        lhs = jnp.concatenate([hn[d][t][:, kb * cin:(kb + 1) * cin] for t in range(ts)], axis=0)
        return _dot(lhs, win_ref[d, kb])

    x_next = drive(*jobs[0])
    for idx, (d, kb) in enumerate(jobs):
        x = x_next
        if idx + 1 < len(jobs):
            x_next = drive(*jobs[idx + 1])
        y_ref = refs[d][1]
        xr = [x[t * rt:(t + 1) * rt, :w] for t in range(ts)]
        xi = [x[t * rt:(t + 1) * rt, w:] for t in range(ts)]
        p_r = pw_ref[d, 0, 0:ts, kb * w:(kb + 1) * w]
        p_i = pw_ref[d, 1, 0:ts, kb * w:(kb + 1) * w]
        one = ts if d == 0 else 2 * ts - 1
        a_r = pw_ref[d, 0, one:one + 1, kb * w:(kb + 1) * w]
        a_i = pw_ref[d, 1, one:one + 1, kb * w:(kb + 1) * w]
        order = list(range(ts)) if d == 0 else list(range(ts - 1, -1, -1))

        e_r, e_i = xr[order[0]], xi[order[0]]
        for t in order[1:]:
            e_r, e_i = cmul_add(a_r, a_i, e_r, e_i, xr[t], xi[t])

        consts = []
        for k in (1, 2, 4):
            src = k - 1 if d == 0 else ts - k
            valid = rowi >= k if d == 0 else rowi <= ts - 1 - k
            consts.append((jnp.where(valid, tile(p_r[src:src + 1]), 0.0),
                           jnp.where(valid, tile(p_i[src:src + 1]), 0.0)))
        last, edge, back = (ts - 1, 0, 1) if d == 0 else (0, ts - 1, ts - 1)
        c_r, c_i = car_ref[d, kb, 0], car_ref[d, kb, 1]
        in_r, in_i = [None] * nt, [None] * nt
        for jj in range(nt):
            j = jj if d == 0 else nt - 1 - jj
            s_r, s_i = e_r[j * ts:(j + 1) * ts], e_i[j * ts:(j + 1) * ts]
            for k, (k_r, k_i) in zip((1, 2, 4), consts):
                sh = k if d == 0 else ts - k
                t_r, t_i = pltpu.roll(s_r, sh, 0), pltpu.roll(s_i, sh, 0)
                s_r, s_i = s_r + k_r * t_r - k_i * t_i, s_i + k_r * t_i + k_i * t_r
            s_r, s_i = s_r + p_r * c_r - p_i * c_i, s_i + p_r * c_i + p_i * c_r
            in_r[j] = jnp.where(rowi == edge, c_r, pltpu.roll(s_r, back, 0))
            in_i[j] = jnp.where(rowi == edge, c_i, pltpu.roll(s_i, back, 0))
            c_r, c_i = tile(s_r[last:last + 1]), tile(s_i[last:last + 1])
        car_ref[d, kb, 0] = c_r
        car_ref[d, kb, 1] = c_i

        s_r, s_i = jnp.concatenate(in_r, axis=0), jnp.concatenate(in_i, axis=0)
        states = [None] * ts
        for t in order:
            s_r, s_i = cmul_add(a_r, a_i, s_r, s_i, xr[t], xi[t])
            states[t] = jnp.concatenate([s_r, s_i], axis=1).astype(BF16)
        yk = _dot(jnp.concatenate(states, axis=0), wout_ref[d, kb])
        for t in range(ts):
            for c in range(cin // lanes):
                ysc_ref[kb * (cin // lanes) + c, pl.ds(t, rt, stride=ts), :] = (
                    yk[t * rt:(t + 1) * rt, c * lanes:(c + 1) * lanes])
        if kb == S5_KB - 1:
            for c in range(nl):
                y_ref[:, c * lanes:(c + 1) * lanes] = ysc_ref[c].astype(y_ref.dtype)


def _s5_block_diag(t):
    per = S5_GROUPS // S5_KB
    a, b = t.shape[2], t.shape[3]
    t = t.reshape(2, S5_KB, per, a, b)
    eye = jnp.eye(per, dtype=t.dtype)
    return jnp.einsum('dkgab,gh->dkgahb', t, eye).reshape(2, S5_KB, per * a, per * b)


def _s5_core(h, nw, bb_r, bb_i, c_re, c_im, pw):
    L = h.shape[0]
    rb = min(S5_ROWS, L)
    nb = L // rb
    g, n, c = S5_GROUPS, S5_STATE, S5_GROUP
    to_in = lambda t: _s5_block_diag(t.reshape(2, g, n, c).transpose(0, 1, 3, 2))
    w_in = jnp.concatenate([to_in(bb_r), to_in(bb_i)], axis=-1).astype(BF16)
    to_out = lambda t: _s5_block_diag(t.transpose(0, 1, 3, 2))
    w_out = jnp.concatenate([to_out(c_re), to_out(-c_im)], axis=-2).astype(BF16)
    w = S5_LANES // S5_KB
    ts = S5_TILE
    rt = rb // ts
    fwd = lambda b: (b, 0)
    bwd = lambda b: (nb - 1 - b, 0)
    return pl.pallas_call(
        functools.partial(_s5_core_kernel, rt=rt),
        grid=(nb,),
        in_specs=[
            pl.BlockSpec((rb, D_MODEL), fwd),
            pl.BlockSpec((rb, D_MODEL), bwd),
            _resident((1, D_MODEL)),
            _resident(w_in.shape),
            _resident(w_out.shape),
            _resident(pw.shape),
        ],
        out_specs=[pl.BlockSpec((rb, D_MODEL), fwd), pl.BlockSpec((rb, D_MODEL), bwd)],
        out_shape=[jax.ShapeDtypeStruct((L, D_MODEL), BF16)] * 2,
        scratch_shapes=[pltpu.VMEM((2, S5_KB, 2, ts, w), F32),
                        pltpu.VMEM((D_MODEL // 128, rb, 128), F32),
                        pltpu.VMEM((D_MODEL // 128, rb, 128), F32)],
        compiler_params=_params("arbitrary"),
        name="s5_core",
    )(h, h, nw, w_in, w_out, pw)


def _s5_out_kernel(yf_ref, yb_ref, h_ref, nw_ref, dsk_ref, wg_ref, bg_ref, pnw_ref, o_ref):
    h = h_ref[...]
    hn = _rms(h, nw_ref[...])
    y = yf_ref[...].astype(F32) + yb_ref[...].astype(F32) + dsk_ref[...] * hn
    y = jax.nn.gelu(y)
    vg = _dot(y.astype(BF16), wg_ref[...]) + bg_ref[...]
    m = vg[:, :D_MODEL] * jax.nn.sigmoid(vg[:, D_MODEL:])
    o_ref[...] = h + _rms(m, pnw_ref[...])


def _s5_layer(h, nw_pre, nw_post, lam_re, lam_im, log_step, b_re, b_im, c_re, c_im, d_skip, w_glu, b_glu):
    nw = nw_pre.reshape(1, -1)
    bb_r, bb_i, pw = _s5_prep(lam_re, lam_im, log_step, b_re, b_im)
    yf, yb = _s5_core(h, nw, bb_r, bb_i, c_re, c_im, pw)
    consts = (nw, d_skip.reshape(1, -1), w_glu.astype(BF16), b_glu.reshape(1, -1), nw_post.reshape(1, -1))
    return _mixer_out_call(_s5_out_kernel, "s5_out", h, (yf, yb), consts)


def _ffn_kernel(h_ref, nw_ref, wgu_ref, wd_ref, pnw_ref, o_ref, act_ref, *, fchunk):
    h = h_ref[...]
    hn = _rms(h, nw_ref[...]).astype(BF16)
    for c in range(D_FF // fchunk):
        gate = _dot(hn, wgu_ref[:, c * fchunk:(c + 1) * fchunk])
        up = _dot(hn, wgu_ref[:, D_FF + c * fchunk:D_FF + (c + 1) * fchunk])
        act_ref[:, c * fchunk:(c + 1) * fchunk] = (_silu(gate) * up).astype(BF16)
    f = _dot(act_ref[...], wd_ref[...])
    o_ref[...] = h + _rms(f, pnw_ref[...])


def _ffn(h, nw_pre, nw_post, w_gate_up, w_down):
    L = h.shape[0]
    tm = min(ROW_TILE, L)
    tiled = pl.BlockSpec((tm, D_MODEL), lambda i: (i, 0))
    return pl.pallas_call(
        functools.partial(_ffn_kernel, fchunk=256),
        grid=(L // tm,),
        in_specs=[tiled, _resident((1, D_MODEL)), _resident((D_MODEL, 2 * D_FF)),
                  _resident((D_FF, D_MODEL)), _resident((1, D_MODEL))],
        out_specs=tiled,
        out_shape=jax.ShapeDtypeStruct(h.shape, F32),
        scratch_shapes=[pltpu.VMEM((tm, D_FF), BF16)],
        compiler_params=_params("arbitrary"),
        name="ffn",
    )(h, nw_pre.reshape(1, -1), w_gate_up.astype(BF16), w_down.astype(BF16), nw_post.reshape(1, -1))


def kernel(x, norm_w, ssd_w_in, ssd_conv_w, ssd_conv_b, ssd_dt_bias, ssd_a_log, ssd_d, ssd_norm_w, ssd_w_out, gdn_w_in, gdn_conv_w, gdn_conv_b, gdn_dt_bias, gdn_a_log, gdn_norm_w, gdn_w_out, s5_lam_re, s5_lam_im, s5_log_step, s5_b_re, s5_b_im, s5_c_re, s5_c_im, s5_d, s5_w_glu, s5_b_glu, ffn_w_gate_up, ffn_w_down):
    bt, L, d = x.shape
    assert bt == 1 and d == D_MODEL
    h = x.reshape(L, d)
    for i in range(norm_w.shape[0]):
        kind, j = i % 3, i // 3
        if kind == 0:
            h = _ssd_layer(h, norm_w[i, 0], norm_w[i, 1], ssd_w_in[j], ssd_conv_w[j], ssd_conv_b[j],
                           ssd_dt_bias[j], ssd_a_log[j], ssd_d[j], ssd_norm_w[j], ssd_w_out[j])
        elif kind == 1:
            h = _gdn_layer(h, norm_w[i, 0], norm_w[i, 1], gdn_w_in[j], gdn_conv_w[j], gdn_conv_b[j],
                           gdn_dt_bias[j], gdn_a_log[j], gdn_norm_w[j], gdn_w_out[j])
        else:
            h = _s5_layer(h, norm_w[i, 0], norm_w[i, 1], s5_lam_re[j], s5_lam_im[j], s5_log_step[j],
                          s5_b_re[j], s5_b_im[j], s5_c_re[j], s5_c_im[j], s5_d[j], s5_w_glu[j], s5_b_glu[j])
        h = _ffn(h, norm_w[i, 2], norm_w[i, 3], ffn_w_gate_up[i], ffn_w_down[i])
    return h.reshape(bt, L, d)
```

```python
import functools

import jax
import jax.numpy as jnp
from jax import lax
from jax.experimental import pallas as pl
from jax.experimental.pallas import tpu as pltpu

F32 = jnp.float32
BF16 = jnp.bfloat16
HIGHEST = lax.Precision.HIGHEST

D_MODEL = 1024
NORM_EPS = 1e-6
CONV_TAPS = 5
CONV_CH = 4096
Z_CH = 2048
SMALL_CH = 128
HALO = 8

SSD_HEADS = 32
SSD_HEAD_DIM = 64
SSD_GROUPS = 8
SSD_STATE = 128
SSD_CHUNK = 128
SSD_D_INNER = 2048

GDN_HEAD_DIM = 128
GDN_QK_HEADS = 8
GDN_V_HEADS = 16
GDN_KEY_DIM = GDN_QK_HEADS * GDN_HEAD_DIM
GDN_CHUNK = 64
GDN_VALUE_DIM = 2048

S5_GROUPS = 64
S5_GROUP = 16
S5_STATE = 64
S5_LANES = S5_GROUPS * S5_STATE
S5_KB = 4
S5_TILE = 8

D_FF = 2816
FFN_CHUNK = 256

ROW_TILE = 512
GDN_ROWS = 256
S5_ROWS = 512
VMEM_LIMIT_BYTES = 56 * 1024 * 1024

NT_DIMS = (((1,), (1,)), ((), ()))
TN_DIMS = (((0,), (0,)), ((), ()))


def _params(*semantics):
    return pltpu.CompilerParams(dimension_semantics=semantics, vmem_limit_bytes=VMEM_LIMIT_BYTES)


def _resident(shape):
    nd = len(shape)
    return pl.BlockSpec(shape, lambda *_: (0,) * nd, pipeline_mode=pl.Buffered(1))


def _rms(x, w):
    return x * lax.rsqrt(jnp.mean(x * x, axis=-1, keepdims=True) + NORM_EPS) * w


def _silu(x):
    return x * jax.nn.sigmoid(x)


def _dot(a, b):
    return jnp.dot(a, b, preferred_element_type=F32)


def _inproj_kernel(h_ref, hp_ref, hx_ref, nw_ref, wc_ref, wz_ref, ws_ref, cw_ref, cb_ref, sb_ref, sa_ref,
                   act_ref, z_ref, s_ref, hs_ref, ys_ref, *, tm, nblk, cchunk, gdn):
    i = pl.program_id(0)
    nw = nw_ref[...]
    curb = _rms(h_ref[...], nw).astype(BF16)
    z_ref[...] = _dot(curb, wz_ref[...]).astype(z_ref.dtype)
    raw = _dot(curb, ws_ref[...]) + sb_ref[...]
    if gdn:
        is_decay = lax.broadcasted_iota(jnp.int32, raw.shape, 1) < 2 * GDN_V_HEADS
        s_ref[...] = jnp.where(is_decay, -jnp.exp(sa_ref[...]) * jax.nn.softplus(raw), jax.nn.sigmoid(raw))
    else:
        s_ref[...] = jax.nn.softplus(raw)
    pad = CONV_TAPS // 2
    hd = GDN_HEAD_DIM
    lanes = 128
    rows = tm + 2 * HALO
    seg = rows // 8
    for s in range(D_MODEL // lanes):
        ls = slice(s * lanes, (s + 1) * lanes)
        hs_ref[s, 0:HALO, :] = hp_ref[:, ls]
        hs_ref[s, HALO:HALO + tm, :] = h_ref[:, ls]
        hs_ref[s, HALO + tm:rows, :] = hx_ref[:, ls]
    ext = jnp.concatenate(
        [jnp.concatenate([hs_ref[s, pl.ds(j, 8, stride=seg), :] for s in range(D_MODEL // lanes)], axis=1)
         for j in range(seg)], axis=0)
    r = lax.broadcasted_iota(jnp.int32, (rows, 1), 0)
    orig = (r % 8) * seg + r // 8
    inside = jnp.logical_and(jnp.logical_or(orig >= HALO, i > 0),
                             jnp.logical_or(orig < HALO + tm, i < nblk - 1))
    ext = jnp.where(inside, _rms(ext, nw), 0.0).astype(BF16)
    for c in range(CONV_CH // cchunk):
        sl = slice(c * cchunk, (c + 1) * cchunk)
        p = _dot(ext, wc_ref[:, sl])
        head = [pltpu.roll(p[(seg - pad + m) * 8:(seg - pad + m + 1) * 8], 1, 0) for m in range(pad)]
        tail = [pltpu.roll(p[m * 8:(m + 1) * 8], 7, 0) for m in range(pad)]
        p_ext = jnp.concatenate(head + [p] + tail, axis=0)
        acc = cb_ref[:, sl]
        for k in range(CONV_TAPS):
            acc = acc + cw_ref[k:k + 1, sl] * p_ext[8 * k:8 * k + rows]
        y = _silu(acc)
        if gdn and c * cchunk < 2 * GDN_KEY_DIM:
            parts = []
            for s in range(cchunk // hd):
                blk = y[:, s * hd:(s + 1) * hd]
                inv = lax.rsqrt(jnp.sum(blk * blk, axis=-1, keepdims=True) + NORM_EPS)
                if c * cchunk + s * hd < GDN_KEY_DIM:
                    inv = inv * (hd ** -0.5)
                parts.append(blk * inv)
            y = jnp.concatenate(parts, axis=1)
        for s in range(cchunk // lanes):
            for j in range(seg):
                ys_ref[s, pl.ds(j, 8, stride=seg), :] = y[8 * j:8 * j + 8, s * lanes:(s + 1) * lanes]
        for s in range(cchunk // lanes):
            act_ref[:, c * cchunk + s * lanes:c * cchunk + (s + 1) * lanes] = (
                ys_ref[s, HALO:HALO + tm, :].astype(act_ref.dtype))


def _inproj(h, nw, wc, wz, ws, conv_w, conv_b, s_bias, s_alog, gdn):
    L = h.shape[0]
    tm = min(ROW_TILE, L)
    nblk = L // tm
    per = tm // HALO
    last = L // HALO - 1
    kern = functools.partial(_inproj_kernel, tm=tm, nblk=nblk, cchunk=512, gdn=gdn)
    return pl.pallas_call(
        kern,
        grid=(nblk,),
        in_specs=[
            pl.BlockSpec((tm, D_MODEL), lambda i: (i, 0)),
            pl.BlockSpec((HALO, D_MODEL), lambda i: (jnp.maximum(i * per - 1, 0), 0)),
            pl.BlockSpec((HALO, D_MODEL), lambda i: (jnp.minimum((i + 1) * per, last), 0)),
            _resident((1, D_MODEL)),
            _resident((D_MODEL, CONV_CH)),
            _resident((D_MODEL, Z_CH)),
            _resident((D_MODEL, SMALL_CH)),
            _resident((8, CONV_CH)),
            _resident((1, CONV_CH)),
            _resident((1, SMALL_CH)),
            _resident((1, SMALL_CH)),
        ],
        out_specs=[
            pl.BlockSpec((tm, CONV_CH), lambda i: (i, 0)),
            pl.BlockSpec((tm, Z_CH), lambda i: (i, 0)),
            pl.BlockSpec((tm, SMALL_CH), lambda i: (i, 0)),
        ],
        out_shape=[
            jax.ShapeDtypeStruct((L, CONV_CH), BF16),
            jax.ShapeDtypeStruct((L, Z_CH), BF16),
            jax.ShapeDtypeStruct((L, SMALL_CH), F32),
        ],
        scratch_shapes=[pltpu.VMEM((D_MODEL // 128, tm + 2 * HALO, 128), F32),
                        pltpu.VMEM((512 // 128, tm + 2 * HALO, 128), F32)],
        compiler_params=_params("arbitrary"),
        name="inproj_conv",
    )(h, h, h, nw, wc, wz, ws, conv_w, conv_b, s_bias, s_alog)


def _pad_small(t):
    t = t.reshape(1, -1)
    return jnp.pad(t, ((0, 0), (0, SMALL_CH - t.shape[1])))


def _split_in_weights(w_in, conv_w, conv_b, conv_first):
    if conv_first:
        wc, wz = w_in[:, :CONV_CH], w_in[:, CONV_CH:CONV_CH + Z_CH]
    else:
        wz, wc = w_in[:, :Z_CH], w_in[:, Z_CH:Z_CH + CONV_CH]
    ws = w_in[:, CONV_CH + Z_CH:]
    ws = jnp.pad(ws, ((0, 0), (0, SMALL_CH - ws.shape[1])))
    cw = jnp.pad(conv_w, ((0, 8 - CONV_TAPS), (0, 0)))
    return wc.astype(BF16), wz.astype(BF16), ws.astype(BF16), cw, conv_b.reshape(1, CONV_CH)


def _ssd_core_kernel(xf_ref, xb_ref, dtf_ref, dtb_ref, alog_ref, dskip_ref, exb_ref,
                     yf_ref, yb_ref, sf_ref, sb_ref, *, q):
    @pl.when(pl.program_id(0) == 0)
    def _():
        sf_ref[...] = jnp.zeros_like(sf_ref)
        sb_ref[...] = jnp.zeros_like(sb_ref)

    row = lax.broadcasted_iota(jnp.int32, (q, q), 0)
    col = lax.broadcasted_iota(jnp.int32, (q, q), 1)
    lower = row >= col
    upper = row <= col
    lower_f = lower.astype(F32)
    upper_f = upper.astype(F32)
    a = -jnp.exp(alog_ref[...])
    p = SSD_HEAD_DIM
    per_group = SSD_HEADS // SSD_GROUPS
    gw = per_group * p
    pw = 2 * p
    b_off = SSD_D_INNER
    c_off = SSD_D_INNER + SSD_GROUPS * SSD_STATE
    left = lax.broadcasted_iota(jnp.int32, (q, pw), 1) < p

    dirs = ((xf_ref, dtf_ref, yf_ref, sf_ref), (xb_ref, dtb_ref, yb_ref, sb_ref))
    pre = []
    for d, (x_ref, dt_ref, y_ref, s_ref) in enumerate(dirs):
        dt = dt_ref[...]
        da = dt * a
        tri_c, tri_r = (lower_f, upper_f) if d == 0 else (upper_f, lower_f)
        cs = jnp.dot(tri_c, da, precision=HIGHEST, preferred_element_type=F32)
        cs_t = jnp.dot(da.T, tri_r, precision=HIGHEST, preferred_element_type=F32)
        tot = cs[q - 1:q, :] if d == 0 else cs[0:1, :]
        exb = exb_ref[d]
        dt_x = _dot(dt.astype(BF16), exb)
        e_x = _dot(jnp.exp(cs).astype(BF16), exb)
        w_x = _dot((dt * jnp.exp(tot - cs)).astype(BF16), exb)
        t_hi = tot.astype(BF16)
        t_mid = (tot - t_hi.astype(F32)).astype(BF16)
        t_lo = (tot - t_hi.astype(F32) - t_mid.astype(F32)).astype(BF16)
        pieces = jnp.concatenate([jnp.broadcast_to(t, (8, SMALL_CH)) for t in (t_hi, t_mid, t_lo)], axis=0)
        tot_3 = _dot(pieces, exb)
        tot_x = tot_3[0:1] + tot_3[8:9] + tot_3[16:17]
        pre.append((cs, cs_t, dt_x, e_x, w_x, jnp.exp(tot_x)))

    def front(d, g):
        x_ref, s_ref = dirs[d][0], dirs[d][3]
        bmat = x_ref[:, b_off + g * SSD_STATE:b_off + (g + 1) * SSD_STATE]
        cmat = x_ref[:, c_off + g * SSD_STATE:c_off + (g + 1) * SSD_STATE]
        cb = lax.dot_general(cmat, bmat, NT_DIMS, preferred_element_type=F32)
        yoff = _dot(cmat, s_ref[g].astype(BF16))
        return bmat, cb, yoff

    jobs = [(d, g) for d in range(2) for g in range(SSD_GROUPS)]
    ahead = front(*jobs[0])
    for idx, (d, g) in enumerate(jobs):
        bmat, cb, yoff = ahead
        if idx + 1 < len(jobs):
            ahead = front(*jobs[idx + 1])
        xg = dirs[d][0][:, g * gw:(g + 1) * gw].astype(F32)
        cs, cs_t, dt_x, e_x, w_x, cdec_x = pre[d]
        mask = lower if d == 0 else upper
        y_ref = dirs[d][2]
        for pr in range(per_group // 2):
            h0 = per_group * g + 2 * pr
            lms = []
            for hh in (h0, h0 + 1):
                ch = hh + SSD_HEADS * d
                lms.append(cb * jnp.exp(jnp.where(mask, cs[:, ch:ch + 1] - cs_t[ch:ch + 1, :], -jnp.inf)))
            m_pair = jnp.concatenate(lms, axis=1).astype(BF16)
            sl = slice(h0 * p, h0 * p + pw)
            lo = slice(2 * pr * p, 2 * pr * p + pw)
            xp = xg[:, lo]
            xdt = xp * dt_x[:, sl]
            rhs = jnp.concatenate([jnp.where(left, xdt, 0.0), jnp.where(left, 0.0, xdt)], axis=0).astype(BF16)
            y = _dot(m_pair, rhs) + yoff[:, lo] * e_x[:, sl]
            if d == 0:
                y = y + xp * dskip_ref[:, sl]
            y_ref[:, sl] = y.astype(y_ref.dtype)
        s_ref = dirs[d][3]
        gs = slice(g * gw, (g + 1) * gw)
        new = lax.dot_general(bmat, (xg * w_x[:, gs]).astype(BF16), TN_DIMS, preferred_element_type=F32)
        s_ref[g] = s_ref[g] * cdec_x[:, gs] + new


def _ssd_core(act, small, a_log, d_skip):
    L = act.shape[0]
    q = SSD_CHUNK
    nc = L // q
    alog = _pad_small(a_log)
    dskip = jnp.repeat(d_skip, SSD_HEAD_DIM).reshape(1, SSD_D_INNER)
    head_of_channel = jnp.arange(SSD_D_INNER) // SSD_HEAD_DIM
    col_id = jnp.arange(SMALL_CH)
    expand = jnp.stack([col_id[:, None] == head_of_channel[None, :] + SSD_HEADS * d for d in range(2)])
    gw = SSD_D_INNER // SSD_GROUPS
    fwd = lambda c: (c, 0)
    bwd = lambda c: (nc - 1 - c, 0)
    return pl.pallas_call(
        functools.partial(_ssd_core_kernel, q=q),
        grid=(nc,),
        in_specs=[
            pl.BlockSpec((q, CONV_CH), fwd),
            pl.BlockSpec((q, CONV_CH), bwd),
            pl.BlockSpec((q, SMALL_CH), fwd),
            pl.BlockSpec((q, SMALL_CH), bwd),
            _resident((1, SMALL_CH)),
            _resident((1, SSD_D_INNER)),
            _resident((2, SMALL_CH, SSD_D_INNER)),
        ],
        out_specs=[pl.BlockSpec((q, SSD_D_INNER), fwd), pl.BlockSpec((q, SSD_D_INNER), bwd)],
        out_shape=[jax.ShapeDtypeStruct((L, SSD_D_INNER), BF16)] * 2,
        scratch_shapes=[pltpu.VMEM((SSD_GROUPS, SSD_STATE, gw), F32)] * 2,
        compiler_params=_params("arbitrary"),
        name="ssd_core",
    )(act, act, small, small, alog, dskip, expand.astype(BF16))


def _ssd_out_kernel(yf_ref, yb_ref, z_ref, h_ref, gnw_ref, wo_ref, pnw_ref, *ffn_refs):
    y = yf_ref[...].astype(F32) + yb_ref[...].astype(F32)
    yz = y * _silu(z_ref[...].astype(F32))
    gw = SSD_D_INNER // SSD_GROUPS
    parts = []
    for g in range(SSD_GROUPS):
        blk = yz[:, g * gw:(g + 1) * gw]
        parts.append(blk * lax.rsqrt(jnp.mean(blk * blk, axis=-1, keepdims=True) + NORM_EPS))
    yn = (jnp.concatenate(parts, axis=1) * gnw_ref[...]).astype(BF16)
    m = _dot(yn, wo_ref[...])
    _ffn_block(h_ref[...] + _rms(m, pnw_ref[...]), *ffn_refs)


def _ffn_block(h, nw_ref, wgu_ref, wd_ref, pnw_ref, o_ref, act_ref):
    hn = _rms(h, nw_ref[...]).astype(BF16)
    for c in range(D_FF // FFN_CHUNK):
        gate = _dot(hn, wgu_ref[:, c * FFN_CHUNK:(c + 1) * FFN_CHUNK])
        up = _dot(hn, wgu_ref[:, D_FF + c * FFN_CHUNK:D_FF + (c + 1) * FFN_CHUNK])
        act_ref[:, c * FFN_CHUNK:(c + 1) * FFN_CHUNK] = (_silu(gate) * up).astype(BF16)
    f = _dot(act_ref[...], wd_ref[...])
    o_ref[...] = h + _rms(f, pnw_ref[...])


def _mixer_out_call(kern, name, h, streams, consts, ffn):
    L = h.shape[0]
    tm = min(ROW_TILE, L)
    nw_pre, nw_post, w_gate_up, w_down = ffn
    consts = tuple(consts) + (nw_pre.reshape(1, -1), w_gate_up.astype(BF16), w_down.astype(BF16),
                              nw_post.reshape(1, -1))
    tiled = lambda a: pl.BlockSpec((tm, a.shape[1]), lambda i: (i, 0))
    return pl.pallas_call(
        kern,
        grid=(L // tm,),
        in_specs=[tiled(a) for a in streams] + [tiled(h)] + [_resident(c.shape) for c in consts],
        out_specs=tiled(h),
        out_shape=jax.ShapeDtypeStruct(h.shape, F32),
        scratch_shapes=[pltpu.VMEM((tm, D_FF), BF16)],
        compiler_params=_params("arbitrary"),
        name=name,
    )(*streams, h, *consts)


def _ssd_layer(h, nw_pre, nw_post, w_in, conv_w, conv_b, dt_bias, a_log, d_skip, norm_w, w_out, ffn):
    wc, wz, ws, cw, cb = _split_in_weights(w_in, conv_w, conv_b, conv_first=False)
    act, z, small = _inproj(h, nw_pre.reshape(1, -1), wc, wz, ws, cw, cb, _pad_small(dt_bias), _pad_small(a_log),
                            gdn=False)
    yf, yb = _ssd_core(act, small, a_log, d_skip)
    consts = (norm_w.reshape(1, -1), w_out.astype(BF16), nw_post.reshape(1, -1))
    return _mixer_out_call(_ssd_out_kernel, "ssd_out_ffn", h, (yf, yb, z), consts, ffn)


def _gdn_core_kernel(xf_ref, xb_ref, gcf_ref, gcb_ref, grf_ref, grb_ref, ex_ref, exh_ref,
                     of_ref, ob_ref, s_ref, u_ref, wq_ref, kd_ref, qkm_ref, et_ref, *, rb, ck):
    @pl.when(pl.program_id(0) == 0)
    def _():
        s_ref[...] = jnp.zeros_like(s_ref)

    n = rb // ck
    hd = GDN_HEAD_DIM
    nq, nv = GDN_QK_HEADS, GDN_V_HEADS
    k_off, v_off = nq * hd, 2 * nq * hd
    row = lax.broadcasted_iota(jnp.int32, (ck, ck), 0)
    col = lax.broadcasted_iota(jnp.int32, (ck, ck), 1)
    lower, upper = row >= col, row <= col
    lower_s, upper_s = row > col, row < col
    lower_f, upper_f = lower.astype(F32), upper.astype(F32)
    row2 = lax.broadcasted_iota(jnp.int32, (ck, 2 * ck), 0)
    lane2 = lax.broadcasted_iota(jnp.int32, (ck, 2 * ck), 1)
    left = lane2 < ck
    col2 = jnp.where(left, lane2, lane2 - ck)
    pair = lambda t: jnp.concatenate([t, t], axis=1)
    eye2 = (row2 == col2).astype(F32)

    def block_diag(p2):
        return jnp.concatenate([jnp.where(left, p2, 0.0), jnp.where(left, 0.0, p2)], axis=0).astype(BF16)

    def solve_chunk(ci, carry):
        pairs = []
        for d in range(2):
            x_ref, gc_ref, gr_ref = (xf_ref, gcf_ref, grf_ref) if d == 0 else (xb_ref, gcb_ref, grb_ref)
            cc = ci if d == 0 else n - 1 - ci
            rows = pl.ds(pl.multiple_of(cc * ck, ck), ck)
            tri_c, tri_r = (lower_f, upper_f) if d == 0 else (upper_f, lower_f)
            mask2, strict2 = (row2 >= col2, row2 > col2) if d == 0 else (row2 <= col2, row2 < col2)
            gb_c = gc_ref[rows, :]
            gb_r = gr_ref[cc]
            cs_c = jnp.dot(tri_c, gb_c, precision=HIGHEST, preferred_element_type=F32)
            cs_r = jnp.dot(gb_r, tri_r, precision=HIGHEST, preferred_element_type=F32)
            tot = cs_c[ck - 1:ck, :] if d == 0 else cs_c[0:1, :]
            et_ref[d, cc] = jnp.exp(tot)
            beta_al = pltpu.roll(gb_c, SMALL_CH - 2 * nv, axis=1)
            ex, ex_half = ex_ref[d], exh_ref[d]
            beta_b = beta_al.astype(BF16)
            beta_h = _dot(beta_b, ex_half)
            stacked = jnp.concatenate([beta_b, jnp.exp(cs_c).astype(BF16), jnp.exp(tot - cs_c).astype(BF16)], axis=0)
            stacked_x = _dot(stacked, ex)
            beta_x = stacked_x[:ck]
            e_x = stacked_x[ck:2 * ck]
            kf_x = stacked_x[2 * ck:]
            for hq in range(nq):
                qn = x_ref[rows, hq * hd:(hq + 1) * hd]
                kn = x_ref[rows, k_off + hq * hd:k_off + (hq + 1) * hd]
                gram = lax.dot_general(jnp.concatenate([kn, qn], axis=0), kn, NT_DIMS,
                                       preferred_element_type=F32)
                kk2, qk2 = pair(gram[:ck]), pair(gram[ck:])
                c0 = nv * d + 2 * hq
                gcs_c2 = jnp.where(left, cs_c[:, c0:c0 + 1], cs_c[:, c0 + 1:c0 + 2])
                gcs_r2 = jnp.concatenate([cs_r[c0:c0 + 1, :], cs_r[c0 + 1:c0 + 2, :]], axis=1)
                beta2 = beta_h[:, hq * 2 * ck:(hq + 1) * 2 * ck]
                decay2 = jnp.exp(jnp.where(mask2, gcs_c2 - gcs_r2, -jnp.inf))
                a2 = jnp.where(strict2, beta2 * kk2 * decay2, 0.0)
                qkm_ref[d, hq, cc] = (qk2 * decay2).astype(BF16)
                qf, kf = qn.astype(F32), kn.astype(F32)
                rhs = []
                for j in range(2):
                    vh, cg = 2 * hq + j, c0 + j
                    hs = slice(vh * hd, (vh + 1) * hd)
                    beta_c, e_g = beta_x[:, hs], e_x[:, hs]
                    vx = x_ref[rows, v_off + vh * hd:v_off + (vh + 1) * hd].astype(F32)
                    rhs.append(jnp.concatenate([vx * beta_c, kf * (beta_c * e_g)], axis=1).astype(BF16))
                    wq_ref[d, vh, cc, ck:2 * ck, :] = (qf * e_g).astype(BF16)
                    kd_ref[d, vh, cc] = (kf * kf_x[:, hs]).astype(BF16)
                zero = jnp.zeros_like(rhs[0])
                rhs2 = jnp.concatenate([jnp.concatenate([rhs[0], zero], axis=1),
                                        jnp.concatenate([zero, rhs[1]], axis=1)], axis=0)
                pairs.append((d, hq, cc, a2, rhs2))
        pws = [_dot(a2.astype(BF16), block_diag(a2)) for (_, _, _, a2, _) in pairs]
        tis = [eye2 - a2 for (_, _, _, a2, _) in pairs]
        for level in range(5):
            if level < 4:
                res = [_dot(jnp.concatenate([t, p], axis=0).astype(BF16), block_diag(p)) for t, p in zip(tis, pws)]
                tis = [t + r[:ck] for t, r in zip(tis, res)]
                pws = [r[ck:] for r in res]
            else:
                tis = [t + _dot(t.astype(BF16), block_diag(p)) for t, p in zip(tis, pws)]
        for (d, hq, cc, _, rhs2), t in zip(pairs, tis):
            sol = _dot(t.astype(BF16), rhs2)
            for j in range(2):
                vh = 2 * hq + j
                u_ref[d, vh, cc] = sol[:, 2 * j * hd:(2 * j + 1) * hd]
                wq_ref[d, vh, cc, 0:ck, :] = sol[:, (2 * j + 1) * hd:(2 * j + 2) * hd].astype(BF16)
        return carry

    def scan_chunk(ci, carry):
        units = [(d, vh, ci if d == 0 else n - 1 - ci) for d in range(2) for vh in range(nv)]
        boths = [_dot(wq_ref[d, vh, cc], s_ref[d, vh].astype(BF16)) for d, vh, cc in units]
        v_news = [(u_ref[d, vh, cc] - both[:ck]).astype(BF16) for (d, vh, cc), both in zip(units, boths)]
        for i in range(0, len(units), 2):
            d, vh, cc = units[i]
            o_ref = of_ref if d == 0 else ob_ref
            rows = pl.ds(pl.multiple_of(cc * ck, ck), ck)
            zero = jnp.zeros_like(v_news[i])
            v2 = jnp.concatenate([jnp.concatenate([v_news[i], zero], axis=1),
                                  jnp.concatenate([zero, v_news[i + 1]], axis=1)], axis=0)
            o2 = jnp.concatenate([boths[i][ck:], boths[i + 1][ck:]], axis=1) + _dot(qkm_ref[d, vh // 2, cc], v2)
            o_ref[rows, vh * hd:(vh + 2) * hd] = o2.astype(o_ref.dtype)
        for (d, vh, cc), v_newb in zip(units, v_news):
            scale = et_ref[d, cc][:, nv * d + vh:nv * d + vh + 1]
            s_ref[d, vh] = s_ref[d, vh] * scale + lax.dot_general(
                kd_ref[d, vh, cc], v_newb, TN_DIMS, preferred_element_type=F32)
        return carry

    lax.fori_loop(0, n, solve_chunk, 0)
    lax.fori_loop(0, n, scan_chunk, 0)


def _gdn_core(act, small):
    L = act.shape[0]
    ck = GDN_CHUNK
    rb = min(GDN_ROWS, L)
    nb = L // rb
    nv = GDN_V_HEADS
    hd = GDN_HEAD_DIM
    n = rb // ck
    g_row = small[:, :4 * nv].reshape(L // ck, ck, 4 * nv).transpose(0, 2, 1)
    col_id = jnp.arange(SMALL_CH)[:, None]
    expand = lambda w: jnp.stack([col_id == jnp.arange(nv * w)[None, :] // w + nv * d
                                  for d in range(2)]).astype(BF16)
    ex, ex_half = expand(hd), expand(ck)
    fwd = lambda b: (b, 0)
    bwd = lambda b: (nb - 1 - b, 0)
    return pl.pallas_call(
        functools.partial(_gdn_core_kernel, rb=rb, ck=ck),
        grid=(nb,),
        in_specs=[
            pl.BlockSpec((rb, CONV_CH), fwd),
            pl.BlockSpec((rb, CONV_CH), bwd),
            pl.BlockSpec((rb, SMALL_CH), fwd),
            pl.BlockSpec((rb, SMALL_CH), bwd),
            pl.BlockSpec((n, 4 * nv, ck), lambda b: (b, 0, 0)),
            pl.BlockSpec((n, 4 * nv, ck), lambda b: (nb - 1 - b, 0, 0)),
            _resident(ex.shape),
            _resident(ex_half.shape),
        ],
        out_specs=[pl.BlockSpec((rb, GDN_VALUE_DIM), fwd), pl.BlockSpec((rb, GDN_VALUE_DIM), bwd)],
        out_shape=[jax.ShapeDtypeStruct((L, GDN_VALUE_DIM), BF16)] * 2,
        scratch_shapes=[
            pltpu.VMEM((2, nv, hd, hd), F32),
            pltpu.VMEM((2, nv, n, ck, hd), F32),
            pltpu.VMEM((2, nv, n, 2 * ck, hd), BF16),
            pltpu.VMEM((2, nv, n, ck, hd), BF16),
            pltpu.VMEM((2, nv // 2, n, ck, 2 * ck), BF16),
            pltpu.VMEM((2, n, 1, SMALL_CH), F32),
        ],
        compiler_params=_params("arbitrary"),
        name="gdn_core",
    )(act, act, small, small, g_row, g_row, ex, ex_half)


def _gdn_out_kernel(of_ref, ob_ref, z_ref, h_ref, hnw_ref, wo_ref, pnw_ref, *ffn_refs):
    o = of_ref[...].astype(F32) + ob_ref[...].astype(F32)
    z = z_ref[...].astype(F32)
    hd = GDN_HEAD_DIM
    hnw = hnw_ref[...]
    parts = []
    for v in range(GDN_VALUE_DIM // hd):
        blk = o[:, v * hd:(v + 1) * hd]
        parts.append(_rms(blk, hnw) * _silu(z[:, v * hd:(v + 1) * hd]))
    m = _dot(jnp.concatenate(parts, axis=1).astype(BF16), wo_ref[...])
    _ffn_block(h_ref[...] + _rms(m, pnw_ref[...]), *ffn_refs)


def _gdn_layer(h, nw_pre, nw_post, w_in, conv_w, conv_b, dt_bias, a_log, norm_w, w_out, ffn):
    wc, wz, ws, cw, cb = _split_in_weights(w_in, conv_w, conv_b, conv_first=True)
    act, z, small = _inproj(h, nw_pre.reshape(1, -1), wc, wz, ws, cw, cb, _pad_small(dt_bias), _pad_small(a_log),
                            gdn=True)
    of, ob = _gdn_core(act, small)
    consts = (norm_w.reshape(1, -1), w_out.astype(BF16), nw_post.reshape(1, -1))
    return _mixer_out_call(_gdn_out_kernel, "gdn_out_ffn", h, (of, ob, z), consts, ffn)


def _s5_prep_kernel(lre_ref, lim_ref, lst_ref, bre_ref, bim_ref, lre4_ref, lim4_ref, lst4_ref,
                    bbr_ref, bbi_ref, pw_ref):
    lre, lim = lre_ref[...], lim_ref[...]
    step = jnp.exp(lst_ref[...])
    mag, ang = jnp.exp(lre * step), lim * step
    lbr, lbi = mag * jnp.cos(ang), mag * jnp.sin(ang)
    inv_den = 1.0 / (lre * lre + lim * lim)
    ar, ai, br, bi = lbr - 1.0, lbi, lre * inv_den, -lim * inv_den
    zr, zi = ar * br - ai * bi, ar * bi + ai * br
    b_re, b_im = bre_ref[...], bim_ref[...]
    bbr_ref[...] = zr * b_re - zi * b_im
    bbi_ref[...] = zr * b_im + zi * b_re
    r = lax.broadcasted_iota(jnp.int32, (2 * S5_TILE, S5_LANES), 0)
    coarse = r < S5_TILE
    rr = jnp.where(coarse, r, r - S5_TILE).astype(F32)
    mult = jnp.where(coarse, float(S5_TILE), 1.0)
    for d in range(2):
        k = mult * (rr + 1.0 if d == 0 else float(S5_TILE) - rr)
        st = jnp.exp(lst4_ref[d])
        m, a = jnp.exp(k * (lre4_ref[d] * st)), k * (lim4_ref[d] * st)
        pw_ref[d, 0] = m * jnp.cos(a)
        pw_ref[d, 1] = m * jnp.sin(a)


def _s5_prep(lam_re, lam_im, log_step, b_re, b_im):
    g, n, c = S5_GROUPS, S5_STATE, S5_GROUP
    expand = lambda t: jnp.repeat(t, c, axis=-1)
    lst = jnp.broadcast_to(log_step[:, :, None], (2, g, n))
    flat = lambda t: t.reshape(2, 1, g * n)
    full = lambda s: pl.BlockSpec(s, lambda: (0,) * len(s))
    shapes = [(2, g, n * c)] * 3 + [(1, g, n * c)] * 2 + [(2, 1, g * n)] * 3
    return pl.pallas_call(
        _s5_prep_kernel,
        in_specs=[full(s) for s in shapes],
        out_specs=[full((2, g, n * c)), full((2, g, n * c)), full((2, 2, 2 * S5_TILE, g * n))],
        out_shape=[
            jax.ShapeDtypeStruct((2, g, n * c), F32),
            jax.ShapeDtypeStruct((2, g, n * c), F32),
            jax.ShapeDtypeStruct((2, 2, 2 * S5_TILE, g * n), F32),
        ],
        name="s5_prep",
    )(expand(lam_re), expand(lam_im), expand(lst), b_re.reshape(1, g, n * c), b_im.reshape(1, g, n * c),
      flat(lam_re), flat(lam_im), flat(lst))


def _s5_core_kernel(hf_ref, hb_ref, nw_ref, win_ref, wout_ref, pw_ref, yf_ref, yb_ref,
                    car_ref, hsc_ref, ysc_ref, *, rt):
    @pl.when(pl.program_id(0) == 0)
    def _():
        car_ref[...] = jnp.zeros_like(car_ref)

    nw = nw_ref[...]
    ts = S5_TILE
    nt = rt // ts
    w = S5_LANES // S5_KB
    cin = D_MODEL // S5_KB
    rowi = lax.broadcasted_iota(jnp.int32, (ts, w), 0)
    tile = lambda v: jnp.broadcast_to(v, (ts, w))

    def cmul_add(a_r, a_i, s_r, s_i, x_r, x_i):
        return a_r * s_r - a_i * s_i + x_r, a_r * s_i + a_i * s_r + x_i

    jobs = [(d, kb) for d in range(2) for kb in range(S5_KB)]
    refs = ((hf_ref, yf_ref), (hb_ref, yb_ref))
    lanes = 128
    nl = D_MODEL // lanes
    hn = []
    for d in range(2):
        for c in range(nl):
            hsc_ref[c] = refs[d][0][:, c * lanes:(c + 1) * lanes]
        hn.append([_rms(jnp.concatenate([hsc_ref[c, pl.ds(t, rt, stride=ts), :] for c in range(nl)], axis=1),
                        nw).astype(BF16) for t in range(ts)])

    def drive(d, kb):
        lhs = jnp.concatenate([hn[d][t][:, kb * cin:(kb + 1) * cin] for t in range(ts)], axis=0)
        return _dot(lhs, win_ref[d, kb])

    x_next = drive(*jobs[0])
    for idx, (d, kb) in enumerate(jobs):
        x = x_next
        if idx + 1 < len(jobs):
            x_next = drive(*jobs[idx + 1])
        y_ref = refs[d][1]
        xr = [x[t * rt:(t + 1) * rt, :w] for t in range(ts)]
        xi = [x[t * rt:(t + 1) * rt, w:] for t in range(ts)]
        p_r = pw_ref[d, 0, 0:ts, kb * w:(kb + 1) * w]
        p_i = pw_ref[d, 1, 0:ts, kb * w:(kb + 1) * w]
        one = ts if d == 0 else 2 * ts - 1
        a_r = pw_ref[d, 0, one:one + 1, kb * w:(kb + 1) * w]
        a_i = pw_ref[d, 1, one:one + 1, kb * w:(kb + 1) * w]
        order = list(range(ts)) if d == 0 else list(range(ts - 1, -1, -1))

        e_r, e_i = xr[order[0]], xi[order[0]]
        for t in order[1:]:
            e_r, e_i = cmul_add(a_r, a_i, e_r, e_i, xr[t], xi[t])

        consts = []
        for k in (1, 2, 4):
            src = k - 1 if d == 0 else ts - k
            valid = rowi >= k if d == 0 else rowi <= ts - 1 - k
            consts.append((jnp.where(valid, tile(p_r[src:src + 1]), 0.0),
                           jnp.where(valid, tile(p_i[src:src + 1]), 0.0)))
        last, edge, back = (ts - 1, 0, 1) if d == 0 else (0, ts - 1, ts - 1)
        c_r, c_i = car_ref[d, kb, 0], car_ref[d, kb, 1]
        in_r, in_i = [None] * nt, [None] * nt
        for jj in range(nt):
            j = jj if d == 0 else nt - 1 - jj
            s_r, s_i = e_r[j * ts:(j + 1) * ts], e_i[j * ts:(j + 1) * ts]
            for k, (k_r, k_i) in zip((1, 2, 4), consts):
                sh = k if d == 0 else ts - k
                t_r, t_i = pltpu.roll(s_r, sh, 0), pltpu.roll(s_i, sh, 0)
                s_r, s_i = s_r + k_r * t_r - k_i * t_i, s_i + k_r * t_i + k_i * t_r
            s_r, s_i = s_r + p_r * c_r - p_i * c_i, s_i + p_r * c_i + p_i * c_r
            in_r[j] = jnp.where(rowi == edge, c_r, pltpu.roll(s_r, back, 0))
            in_i[j] = jnp.where(rowi == edge, c_i, pltpu.roll(s_i, back, 0))
            c_r, c_i = tile(s_r[last:last + 1]), tile(s_i[last:last + 1])
        car_ref[d, kb, 0] = c_r
        car_ref[d, kb, 1] = c_i

        s_r, s_i = jnp.concatenate(in_r, axis=0), jnp.concatenate(in_i, axis=0)
        states = [None] * ts
        for t in order:
            s_r, s_i = cmul_add(a_r, a_i, s_r, s_i, xr[t], xi[t])
            states[t] = jnp.concatenate([s_r, s_i], axis=1).astype(BF16)
        yk = _dot(jnp.concatenate(states, axis=0), wout_ref[d, kb])
        for t in range(ts):
            for c in range(cin // lanes):
                ysc_ref[kb * (cin // lanes) + c, pl.ds(t, rt, stride=ts), :] = (
                    yk[t * rt:(t + 1) * rt, c * lanes:(c + 1) * lanes])
        if kb == S5_KB - 1:
            for c in range(nl):
                y_ref[:, c * lanes:(c + 1) * lanes] = ysc_ref[c].astype(y_ref.dtype)


def _s5_block_diag(t):
    per = S5_GROUPS // S5_KB
    a, b = t.shape[2], t.shape[3]
    t = t.reshape(2, S5_KB, per, a, b)
    eye = jnp.eye(per, dtype=t.dtype)
    return jnp.einsum('dkgab,gh->dkgahb', t, eye).reshape(2, S5_KB, per * a, per * b)


def _s5_core(h, nw, bb_r, bb_i, c_re, c_im, pw):
    L = h.shape[0]
    rb = min(S5_ROWS, L)
    nb = L // rb
    g, n, c = S5_GROUPS, S5_STATE, S5_GROUP
    to_in = lambda t: _s5_block_diag(t.reshape(2, g, n, c).transpose(0, 1, 3, 2))
    w_in = jnp.concatenate([to_in(bb_r), to_in(bb_i)], axis=-1).astype(BF16)
    to_out = lambda t: _s5_block_diag(t.transpose(0, 1, 3, 2))
    w_out = jnp.concatenate([to_out(c_re), to_out(-c_im)], axis=-2).astype(BF16)
    w = S5_LANES // S5_KB
    ts = S5_TILE
    rt = rb // ts
    fwd = lambda b: (b, 0)
    bwd = lambda b: (nb - 1 - b, 0)
    return pl.pallas_call(
        functools.partial(_s5_core_kernel, rt=rt),
        grid=(nb,),
        in_specs=[
            pl.BlockSpec((rb, D_MODEL), fwd),
            pl.BlockSpec((rb, D_MODEL), bwd),
            _resident((1, D_MODEL)),
            _resident(w_in.shape),
            _resident(w_out.shape),
            _resident(pw.shape),
        ],
        out_specs=[pl.BlockSpec((rb, D_MODEL), fwd), pl.BlockSpec((rb, D_MODEL), bwd)],
        out_shape=[jax.ShapeDtypeStruct((L, D_MODEL), BF16)] * 2,
        scratch_shapes=[pltpu.VMEM((2, S5_KB, 2, ts, w), F32),
                        pltpu.VMEM((D_MODEL // 128, rb, 128), F32),
                        pltpu.VMEM((D_MODEL // 128, rb, 128), F32)],
        compiler_params=_params("arbitrary"),
        name="s5_core",
    )(h, h, nw, w_in, w_out, pw)


def _s5_out_kernel(yf_ref, yb_ref, h_ref, nw_ref, dsk_ref, wg_ref, bg_ref, pnw_ref, *ffn_refs):
    h = h_ref[...]
    hn = _rms(h, nw_ref[...])
    y = yf_ref[...].astype(F32) + yb_ref[...].astype(F32) + dsk_ref[...] * hn
    y = jax.nn.gelu(y)
    vg = _dot(y.astype(BF16), wg_ref[...]) + bg_ref[...]
    m = vg[:, :D_MODEL] * jax.nn.sigmoid(vg[:, D_MODEL:])
    _ffn_block(h + _rms(m, pnw_ref[...]), *ffn_refs)


def _s5_layer(h, nw_pre, nw_post, lam_re, lam_im, log_step, b_re, b_im, c_re, c_im, d_skip, w_glu, b_glu,
              ffn):
    nw = nw_pre.reshape(1, -1)
    bb_r, bb_i, pw = _s5_prep(lam_re, lam_im, log_step, b_re, b_im)
    yf, yb = _s5_core(h, nw, bb_r, bb_i, c_re, c_im, pw)
    consts = (nw, d_skip.reshape(1, -1), w_glu.astype(BF16), b_glu.reshape(1, -1), nw_post.reshape(1, -1))
    return _mixer_out_call(_s5_out_kernel, "s5_out_ffn", h, (yf, yb), consts, ffn)


def kernel(x, norm_w, ssd_w_in, ssd_conv_w, ssd_conv_b, ssd_dt_bias, ssd_a_log, ssd_d, ssd_norm_w, ssd_w_out, gdn_w_in, gdn_conv_w, gdn_conv_b, gdn_dt_bias, gdn_a_log, gdn_norm_w, gdn_w_out, s5_lam_re, s5_lam_im, s5_log_step, s5_b_re, s5_b_im, s5_c_re, s5_c_im, s5_d, s5_w_glu, s5_b_glu, ffn_w_gate_up, ffn_w_down):
    bt, L, d = x.shape
    assert bt == 1 and d == D_MODEL
    h = x.reshape(L, d)
    for i in range(norm_w.shape[0]):
        kind, j = i % 3, i // 3
        ffn = (norm_w[i, 2], norm_w[i, 3], ffn_w_gate_up[i], ffn_w_down[i])
        if kind == 0:
            h = _ssd_layer(h, norm_w[i, 0], norm_w[i, 1], ssd_w_in[j], ssd_conv_w[j], ssd_conv_b[j],
                           ssd_dt_bias[j], ssd_a_log[j], ssd_d[j], ssd_norm_w[j], ssd_w_out[j], ffn)
        elif kind == 1:
            h = _gdn_layer(h, norm_w[i, 0], norm_w[i, 1], gdn_w_in[j], gdn_conv_w[j], gdn_conv_b[j],
                           gdn_dt_bias[j], gdn_a_log[j], gdn_norm_w[j], gdn_w_out[j], ffn)
        else:
            h = _s5_layer(h, norm_w[i, 0], norm_w[i, 1], s5_lam_re[j], s5_lam_im[j], s5_log_step[j],
                          s5_b_re[j], s5_b_im[j], s5_c_re[j], s5_c_im[j], s5_d[j], s5_w_glu[j], s5_b_glu[j], ffn)
    return h.reshape(bt, L, d)
```

```python
import functools

import jax
import jax.numpy as jnp
from jax import lax
from jax.experimental import pallas as pl
from jax.experimental.pallas import tpu as pltpu

F32 = jnp.float32
BF16 = jnp.bfloat16
HIGHEST = lax.Precision.HIGHEST

D_MODEL = 1024
NORM_EPS = 1e-6
CONV_TAPS = 5
CONV_CH = 4096
Z_CH = 2048
SMALL_CH = 128
HALO = 8

SSD_HEADS = 32
SSD_HEAD_DIM = 64
SSD_GROUPS = 8
SSD_STATE = 128
SSD_CHUNK = 128
SSD_D_INNER = 2048

GDN_HEAD_DIM = 128
GDN_QK_HEADS = 8
GDN_V_HEADS = 16
GDN_KEY_DIM = GDN_QK_HEADS * GDN_HEAD_DIM
GDN_CHUNK = 64
GDN_VALUE_DIM = 2048

S5_GROUPS = 64
S5_GROUP = 16
S5_STATE = 64
S5_LANES = S5_GROUPS * S5_STATE
S5_KB = 4
S5_TILE = 8

D_FF = 2816
FFN_CHUNK = 256

ROW_TILE = 512
GDN_ROWS = 256
S5_ROWS = 512
VMEM_LIMIT_BYTES = 56 * 1024 * 1024

NT_DIMS = (((1,), (1,)), ((), ()))
TN_DIMS = (((0,), (0,)), ((), ()))


def _params(*semantics):
    return pltpu.CompilerParams(dimension_semantics=semantics, vmem_limit_bytes=VMEM_LIMIT_BYTES)


def _resident(shape):
    nd = len(shape)
    return pl.BlockSpec(shape, lambda *_: (0,) * nd, pipeline_mode=pl.Buffered(1))


def _rms(x, w):
    return x * lax.rsqrt(jnp.mean(x * x, axis=-1, keepdims=True) + NORM_EPS) * w


def _silu(x):
    return x * jax.nn.sigmoid(x)


def _dot(a, b):
    return jnp.dot(a, b, preferred_element_type=F32)


def _inproj_kernel(h_ref, hp_ref, hx_ref, nw_ref, wc_ref, wz_ref, ws_ref, cw_ref, cb_ref, sb_ref, sa_ref,
                   act_ref, z_ref, s_ref, hs_ref, ys_ref, *, tm, nblk, cchunk, gdn):
    i = pl.program_id(0)
    nw = nw_ref[...]
    curb = _rms(h_ref[...], nw).astype(BF16)
    z_ref[...] = _dot(curb, wz_ref[...]).astype(z_ref.dtype)
    raw = _dot(curb, ws_ref[...]) + sb_ref[...]
    if gdn:
        is_decay = lax.broadcasted_iota(jnp.int32, raw.shape, 1) < 2 * GDN_V_HEADS
        s_ref[...] = jnp.where(is_decay, -jnp.exp(sa_ref[...]) * jax.nn.softplus(raw), jax.nn.sigmoid(raw))
    else:
        s_ref[...] = jax.nn.softplus(raw)
    pad = CONV_TAPS // 2
    hd = GDN_HEAD_DIM
    lanes = 128
    rows = tm + 2 * HALO
    seg = rows // 8
    for s in range(D_MODEL // lanes):
        ls = slice(s * lanes, (s + 1) * lanes)
        hs_ref[s, 0:HALO, :] = hp_ref[:, ls]
        hs_ref[s, HALO:HALO + tm, :] = h_ref[:, ls]
        hs_ref[s, HALO + tm:rows, :] = hx_ref[:, ls]
    ext = jnp.concatenate(
        [jnp.concatenate([hs_ref[s, pl.ds(j, 8, stride=seg), :] for s in range(D_MODEL // lanes)], axis=1)
         for j in range(seg)], axis=0)
    r = lax.broadcasted_iota(jnp.int32, (rows, 1), 0)
    orig = (r % 8) * seg + r // 8
    inside = jnp.logical_and(jnp.logical_or(orig >= HALO, i > 0),
                             jnp.logical_or(orig < HALO + tm, i < nblk - 1))
    ext = jnp.where(inside, _rms(ext, nw), 0.0).astype(BF16)
    for c in range(CONV_CH // cchunk):
        sl = slice(c * cchunk, (c + 1) * cchunk)
        p = _dot(ext, wc_ref[:, sl])
        head = [pltpu.roll(p[(seg - pad + m) * 8:(seg - pad + m + 1) * 8], 1, 0) for m in range(pad)]
        tail = [pltpu.roll(p[m * 8:(m + 1) * 8], 7, 0) for m in range(pad)]
        p_ext = jnp.concatenate(head + [p] + tail, axis=0)
        acc = cb_ref[:, sl]
        for k in range(CONV_TAPS):
            acc = acc + cw_ref[k:k + 1, sl] * p_ext[8 * k:8 * k + rows]
        y = _silu(acc)
        if gdn and c * cchunk < 2 * GDN_KEY_DIM:
            parts = []
            for s in range(cchunk // hd):
                blk = y[:, s * hd:(s + 1) * hd]
                inv = lax.rsqrt(jnp.sum(blk * blk, axis=-1, keepdims=True) + NORM_EPS)
                if c * cchunk + s * hd < GDN_KEY_DIM:
                    inv = inv * (hd ** -0.5)
                parts.append(blk * inv)
            y = jnp.concatenate(parts, axis=1)
        for s in range(cchunk // lanes):
            for j in range(seg):
                ys_ref[s, pl.ds(j, 8, stride=seg), :] = y[8 * j:8 * j + 8, s * lanes:(s + 1) * lanes]
        for s in range(cchunk // lanes):
            act_ref[:, c * cchunk + s * lanes:c * cchunk + (s + 1) * lanes] = (
                ys_ref[s, HALO:HALO + tm, :].astype(act_ref.dtype))


def _inproj(h, nw, wc, wz, ws, conv_w, conv_b, s_bias, s_alog, gdn):
    L = h.shape[0]
    tm = min(ROW_TILE, L)
    nblk = L // tm
    per = tm // HALO
    last = L // HALO - 1
    kern = functools.partial(_inproj_kernel, tm=tm, nblk=nblk, cchunk=512, gdn=gdn)
    return pl.pallas_call(
        kern,
        grid=(nblk,),
        in_specs=[
            pl.BlockSpec((tm, D_MODEL), lambda i: (i, 0)),
            pl.BlockSpec((HALO, D_MODEL), lambda i: (jnp.maximum(i * per - 1, 0), 0)),
            pl.BlockSpec((HALO, D_MODEL), lambda i: (jnp.minimum((i + 1) * per, last), 0)),
            _resident((1, D_MODEL)),
            _resident((D_MODEL, CONV_CH)),
            _resident((D_MODEL, Z_CH)),
            _resident((D_MODEL, SMALL_CH)),
            _resident((8, CONV_CH)),
            _resident((1, CONV_CH)),
            _resident((1, SMALL_CH)),
            _resident((1, SMALL_CH)),
        ],
        out_specs=[
            pl.BlockSpec((tm, CONV_CH), lambda i: (i, 0)),
            pl.BlockSpec((tm, Z_CH), lambda i: (i, 0)),
            pl.BlockSpec((tm, SMALL_CH), lambda i: (i, 0)),
        ],
        out_shape=[
            jax.ShapeDtypeStruct((L, CONV_CH), BF16),
            jax.ShapeDtypeStruct((L, Z_CH), BF16),
            jax.ShapeDtypeStruct((L, SMALL_CH), F32),
        ],
        scratch_shapes=[pltpu.VMEM((D_MODEL // 128, tm + 2 * HALO, 128), F32),
                        pltpu.VMEM((512 // 128, tm + 2 * HALO, 128), F32)],
        compiler_params=_params("arbitrary"),
        name="inproj_conv",
    )(h, h, h, nw, wc, wz, ws, conv_w, conv_b, s_bias, s_alog)


def _pad_small(t):
    t = t.reshape(1, -1)
    return jnp.pad(t, ((0, 0), (0, SMALL_CH - t.shape[1])))


def _split_in_weights(w_in, conv_w, conv_b, conv_first):
    if conv_first:
        wc, wz = w_in[:, :CONV_CH], w_in[:, CONV_CH:CONV_CH + Z_CH]
    else:
        wz, wc = w_in[:, :Z_CH], w_in[:, Z_CH:Z_CH + CONV_CH]
    ws = w_in[:, CONV_CH + Z_CH:]
    ws = jnp.pad(ws, ((0, 0), (0, SMALL_CH - ws.shape[1])))
    cw = jnp.pad(conv_w, ((0, 8 - CONV_TAPS), (0, 0)))
    return wc.astype(BF16), wz.astype(BF16), ws.astype(BF16), cw, conv_b.reshape(1, CONV_CH)


def _ssd_core_kernel(xf_ref, xb_ref, dtf_ref, dtb_ref, alog_ref, dskip_ref, exb_ref,
                     yf_ref, yb_ref, sf_ref, sb_ref, *, q):
    @pl.when(pl.program_id(0) == 0)
    def _():
        sf_ref[...] = jnp.zeros_like(sf_ref)
        sb_ref[...] = jnp.zeros_like(sb_ref)

    row = lax.broadcasted_iota(jnp.int32, (q, q), 0)
    col = lax.broadcasted_iota(jnp.int32, (q, q), 1)
    lower = row >= col
    upper = row <= col
    lower_f = lower.astype(F32)
    upper_f = upper.astype(F32)
    a = -jnp.exp(alog_ref[...])
    p = SSD_HEAD_DIM
    per_group = SSD_HEADS // SSD_GROUPS
    gw = per_group * p
    pw = 2 * p
    b_off = SSD_D_INNER
    c_off = SSD_D_INNER + SSD_GROUPS * SSD_STATE
    left = lax.broadcasted_iota(jnp.int32, (q, pw), 1) < p

    dirs = ((xf_ref, dtf_ref, yf_ref, sf_ref), (xb_ref, dtb_ref, yb_ref, sb_ref))
    pre = []
    for d, (x_ref, dt_ref, y_ref, s_ref) in enumerate(dirs):
        dt = dt_ref[...]
        da = dt * a
        tri_c, tri_r = (lower_f, upper_f) if d == 0 else (upper_f, lower_f)
        cs = jnp.dot(tri_c, da, precision=HIGHEST, preferred_element_type=F32)
        cs_t = jnp.dot(da.T, tri_r, precision=HIGHEST, preferred_element_type=F32)
        tot = cs[q - 1:q, :] if d == 0 else cs[0:1, :]
        exb = exb_ref[d]
        dt_x = _dot(dt.astype(BF16), exb)
        e_x = _dot(jnp.exp(cs).astype(BF16), exb)
        w_x = _dot((dt * jnp.exp(tot - cs)).astype(BF16), exb)
        t_hi = tot.astype(BF16)
        t_mid = (tot - t_hi.astype(F32)).astype(BF16)
        t_lo = (tot - t_hi.astype(F32) - t_mid.astype(F32)).astype(BF16)
        pieces = jnp.concatenate([jnp.broadcast_to(t, (8, SMALL_CH)) for t in (t_hi, t_mid, t_lo)], axis=0)
        tot_3 = _dot(pieces, exb)
        tot_x = tot_3[0:1] + tot_3[8:9] + tot_3[16:17]
        pre.append((cs, cs_t, dt_x, e_x, w_x, jnp.exp(tot_x)))

    def front(d, g):
        x_ref, s_ref = dirs[d][0], dirs[d][3]
        bmat = x_ref[:, b_off + g * SSD_STATE:b_off + (g + 1) * SSD_STATE]
        cmat = x_ref[:, c_off + g * SSD_STATE:c_off + (g + 1) * SSD_STATE]
        cb = lax.dot_general(cmat, bmat, NT_DIMS, preferred_element_type=F32)
        yoff = _dot(cmat, s_ref[g].astype(BF16))
        return bmat, cb, yoff

    jobs = [(d, g) for d in range(2) for g in range(SSD_GROUPS)]
    ahead = front(*jobs[0])
    for idx, (d, g) in enumerate(jobs):
        bmat, cb, yoff = ahead
        if idx + 1 < len(jobs):
            ahead = front(*jobs[idx + 1])
        xg = dirs[d][0][:, g * gw:(g + 1) * gw].astype(F32)
        cs, cs_t, dt_x, e_x, w_x, cdec_x = pre[d]
        mask = lower if d == 0 else upper
        y_ref = dirs[d][2]
        for pr in range(per_group // 2):
            h0 = per_group * g + 2 * pr
            lms = []
            for hh in (h0, h0 + 1):
                ch = hh + SSD_HEADS * d
                lms.append(cb * jnp.exp(jnp.where(mask, cs[:, ch:ch + 1] - cs_t[ch:ch + 1, :], -jnp.inf)))
            m_pair = jnp.concatenate(lms, axis=1).astype(BF16)
            sl = slice(h0 * p, h0 * p + pw)
            lo = slice(2 * pr * p, 2 * pr * p + pw)
            xp = xg[:, lo]
            xdt = xp * dt_x[:, sl]
            rhs = jnp.concatenate([jnp.where(left, xdt, 0.0), jnp.where(left, 0.0, xdt)], axis=0).astype(BF16)
            y = _dot(m_pair, rhs) + yoff[:, lo] * e_x[:, sl]
            if d == 0:
                y = y + xp * dskip_ref[:, sl]
            y_ref[:, sl] = y.astype(y_ref.dtype)
        s_ref = dirs[d][3]
        gs = slice(g * gw, (g + 1) * gw)
        new = lax.dot_general(bmat, (xg * w_x[:, gs]).astype(BF16), TN_DIMS, preferred_element_type=F32)
        s_ref[g] = s_ref[g] * cdec_x[:, gs] + new


def _ssd_core(act, small, a_log, d_skip):
    L = act.shape[0]
    q = SSD_CHUNK
    nc = L // q
    alog = _pad_small(a_log)
    dskip = jnp.repeat(d_skip, SSD_HEAD_DIM).reshape(1, SSD_D_INNER)
    head_of_channel = jnp.arange(SSD_D_INNER) // SSD_HEAD_DIM
    col_id = jnp.arange(SMALL_CH)
    expand = jnp.stack([col_id[:, None] == head_of_channel[None, :] + SSD_HEADS * d for d in range(2)])
    gw = SSD_D_INNER // SSD_GROUPS
    fwd = lambda c: (c, 0)
    bwd = lambda c: (nc - 1 - c, 0)
    return pl.pallas_call(
        functools.partial(_ssd_core_kernel, q=q),
        grid=(nc,),
        in_specs=[
            pl.BlockSpec((q, CONV_CH), fwd),
            pl.BlockSpec((q, CONV_CH), bwd),
            pl.BlockSpec((q, SMALL_CH), fwd),
            pl.BlockSpec((q, SMALL_CH), bwd),
            _resident((1, SMALL_CH)),
            _resident((1, SSD_D_INNER)),
            _resident((2, SMALL_CH, SSD_D_INNER)),
        ],
        out_specs=[pl.BlockSpec((q, SSD_D_INNER), fwd), pl.BlockSpec((q, SSD_D_INNER), bwd)],
        out_shape=[jax.ShapeDtypeStruct((L, SSD_D_INNER), BF16)] * 2,
        scratch_shapes=[pltpu.VMEM((SSD_GROUPS, SSD_STATE, gw), F32)] * 2,
        compiler_params=_params("arbitrary"),
        name="ssd_core",
    )(act, act, small, small, alog, dskip, expand.astype(BF16))


def _ssd_out_kernel(yf_ref, yb_ref, z_ref, h_ref, gnw_ref, wo_ref, pnw_ref, *ffn_refs):
    y = yf_ref[...].astype(F32) + yb_ref[...].astype(F32)
    yz = y * _silu(z_ref[...].astype(F32))
    gw = SSD_D_INNER // SSD_GROUPS
    parts = []
    for g in range(SSD_GROUPS):
        blk = yz[:, g * gw:(g + 1) * gw]
        parts.append(blk * lax.rsqrt(jnp.mean(blk * blk, axis=-1, keepdims=True) + NORM_EPS))
    yn = (jnp.concatenate(parts, axis=1) * gnw_ref[...]).astype(BF16)
    m = _dot(yn, wo_ref[...])
    _ffn_block(h_ref[...] + _rms(m, pnw_ref[...]), *ffn_refs)


def _ffn_block(h, nw_ref, wgu_ref, wd_ref, pnw_ref, o_ref, act_ref):
    hn = _rms(h, nw_ref[...]).astype(BF16)
    for c in range(D_FF // FFN_CHUNK):
        gate = _dot(hn, wgu_ref[:, c * FFN_CHUNK:(c + 1) * FFN_CHUNK])
        up = _dot(hn, wgu_ref[:, D_FF + c * FFN_CHUNK:D_FF + (c + 1) * FFN_CHUNK])
        act_ref[:, c * FFN_CHUNK:(c + 1) * FFN_CHUNK] = (_silu(gate) * up).astype(BF16)
    f = _dot(act_ref[...], wd_ref[...])
    o_ref[...] = h + _rms(f, pnw_ref[...])


def _mixer_out_call(kern, name, h, streams, consts, ffn):
    L = h.shape[0]
    tm = min(ROW_TILE, L)
    nw_pre, nw_post, w_gate_up, w_down = ffn
    consts = tuple(consts) + (nw_pre.reshape(1, -1), w_gate_up.astype(BF16), w_down.astype(BF16),
                              nw_post.reshape(1, -1))
    tiled = lambda a: pl.BlockSpec((tm, a.shape[1]), lambda i: (i, 0))
    return pl.pallas_call(
        kern,
        grid=(L // tm,),
        in_specs=[tiled(a) for a in streams] + [tiled(h)] + [_resident(c.shape) for c in consts],
        out_specs=tiled(h),
        out_shape=jax.ShapeDtypeStruct(h.shape, F32),
        scratch_shapes=[pltpu.VMEM((tm, D_FF), BF16)],
        compiler_params=_params("arbitrary"),
        name=name,
    )(*streams, h, *consts)


def _ssd_layer(h, nw_pre, nw_post, w_in, conv_w, conv_b, dt_bias, a_log, d_skip, norm_w, w_out, ffn):
    wc, wz, ws, cw, cb = _split_in_weights(w_in, conv_w, conv_b, conv_first=False)
    act, z, small = _inproj(h, nw_pre.reshape(1, -1), wc, wz, ws, cw, cb, _pad_small(dt_bias), _pad_small(a_log),
                            gdn=False)
    yf, yb = _ssd_core(act, small, a_log, d_skip)
    consts = (norm_w.reshape(1, -1), w_out.astype(BF16), nw_post.reshape(1, -1))
    return _mixer_out_call(_ssd_out_kernel, "ssd_out_ffn", h, (yf, yb, z), consts, ffn)


def _gdn_core_kernel(xf_ref, xb_ref, gcf_ref, gcb_ref, grf_ref, grb_ref, ex_ref, exh_ref,
                     of_ref, ob_ref, s_ref, u_ref, wq_ref, kd_ref, qkm_ref, et_ref, *, rb, ck):
    @pl.when(pl.program_id(0) == 0)
    def _():
        s_ref[...] = jnp.zeros_like(s_ref)

    n = rb // ck
    hd = GDN_HEAD_DIM
    nq, nv = GDN_QK_HEADS, GDN_V_HEADS
    k_off, v_off = nq * hd, 2 * nq * hd
    row = lax.broadcasted_iota(jnp.int32, (ck, ck), 0)
    col = lax.broadcasted_iota(jnp.int32, (ck, ck), 1)
    lower, upper = row >= col, row <= col
    lower_s, upper_s = row > col, row < col
    lower_f, upper_f = lower.astype(F32), upper.astype(F32)
    row2 = lax.broadcasted_iota(jnp.int32, (ck, 2 * ck), 0)
    lane2 = lax.broadcasted_iota(jnp.int32, (ck, 2 * ck), 1)
    left = lane2 < ck
    col2 = jnp.where(left, lane2, lane2 - ck)
    pair = lambda t: jnp.concatenate([t, t], axis=1)
    eye2 = (row2 == col2).astype(F32)

    def block_diag(p2):
        return jnp.concatenate([jnp.where(left, p2, 0.0), jnp.where(left, 0.0, p2)], axis=0).astype(BF16)

    def solve_chunk(ci, carry):
        pairs = []
        for d in range(2):
            x_ref, gc_ref, gr_ref = (xf_ref, gcf_ref, grf_ref) if d == 0 else (xb_ref, gcb_ref, grb_ref)
            cc = ci if d == 0 else n - 1 - ci
            rows = pl.ds(pl.multiple_of(cc * ck, ck), ck)
            tri_c, tri_r = (lower_f, upper_f) if d == 0 else (upper_f, lower_f)
            mask2, strict2 = (row2 >= col2, row2 > col2) if d == 0 else (row2 <= col2, row2 < col2)
            gb_c = gc_ref[rows, :]
            gb_r = gr_ref[cc]
            cs_c = jnp.dot(tri_c, gb_c, precision=HIGHEST, preferred_element_type=F32)
            cs_r = jnp.dot(gb_r, tri_r, precision=HIGHEST, preferred_element_type=F32)
            tot = cs_c[ck - 1:ck, :] if d == 0 else cs_c[0:1, :]
            et_ref[d, cc] = jnp.exp(tot)
            beta_al = pltpu.roll(gb_c, SMALL_CH - 2 * nv, axis=1)
            ex, ex_half = ex_ref[d], exh_ref[d]
            beta_b = beta_al.astype(BF16)
            beta_h = _dot(beta_b, ex_half)
            stacked = jnp.concatenate([beta_b, jnp.exp(cs_c).astype(BF16), jnp.exp(tot - cs_c).astype(BF16)], axis=0)
            stacked_x = _dot(stacked, ex)
            beta_x = stacked_x[:ck]
            e_x = stacked_x[ck:2 * ck]
            kf_x = stacked_x[2 * ck:]
            for hq in range(nq):
                qn = x_ref[rows, hq * hd:(hq + 1) * hd]
                kn = x_ref[rows, k_off + hq * hd:k_off + (hq + 1) * hd]
                gram = lax.dot_general(jnp.concatenate([kn, qn], axis=0), kn, NT_DIMS,
                                       preferred_element_type=F32)
                kk2, qk2 = pair(gram[:ck]), pair(gram[ck:])
                c0 = nv * d + 2 * hq
                gcs_c2 = jnp.where(left, cs_c[:, c0:c0 + 1], cs_c[:, c0 + 1:c0 + 2])
                gcs_r2 = jnp.concatenate([cs_r[c0:c0 + 1, :], cs_r[c0 + 1:c0 + 2, :]], axis=1)
                beta2 = beta_h[:, hq * 2 * ck:(hq + 1) * 2 * ck]
                decay2 = jnp.exp(jnp.where(mask2, gcs_c2 - gcs_r2, -jnp.inf))
                a2 = jnp.where(strict2, beta2 * kk2 * decay2, 0.0)
                qkm_ref[d, hq, cc] = (qk2 * decay2).astype(BF16)
                qf, kf = qn.astype(F32), kn.astype(F32)
                rhs = []
                for j in range(2):
                    vh, cg = 2 * hq + j, c0 + j
                    hs = slice(vh * hd, (vh + 1) * hd)
                    beta_c, e_g = beta_x[:, hs], e_x[:, hs]
                    vx = x_ref[rows, v_off + vh * hd:v_off + (vh + 1) * hd].astype(F32)
                    rhs.append(jnp.concatenate([vx * beta_c, kf * (beta_c * e_g)], axis=1).astype(BF16))
                    wq_ref[d, vh, cc, ck:2 * ck, :] = (qf * e_g).astype(BF16)
                    kd_ref[d, vh, cc] = (kf * kf_x[:, hs]).astype(BF16)
                zero = jnp.zeros_like(rhs[0])
                rhs2 = jnp.concatenate([jnp.concatenate([rhs[0], zero], axis=1),
                                        jnp.concatenate([zero, rhs[1]], axis=1)], axis=0)
                pairs.append((d, hq, cc, a2, rhs2))
        pws = [_dot(a2.astype(BF16), block_diag(a2)) for (_, _, _, a2, _) in pairs]
        tis = [eye2 - a2 for (_, _, _, a2, _) in pairs]
        for level in range(5):
            if level < 4:
                res = [_dot(jnp.concatenate([t, p], axis=0).astype(BF16), block_diag(p)) for t, p in zip(tis, pws)]
                tis = [t + r[:ck] for t, r in zip(tis, res)]
                pws = [r[ck:] for r in res]
            else:
                tis = [t + _dot(t.astype(BF16), block_diag(p)) for t, p in zip(tis, pws)]
        for (d, hq, cc, _, rhs2), t in zip(pairs, tis):
            sol = _dot(t.astype(BF16), rhs2)
            for j in range(2):
                vh = 2 * hq + j
                u_ref[d, vh, cc] = sol[:, 2 * j * hd:(2 * j + 1) * hd]
                wq_ref[d, vh, cc, 0:ck, :] = sol[:, (2 * j + 1) * hd:(2 * j + 2) * hd].astype(BF16)
        return carry

    def scan_chunk(ci, carry):
        units = [(d, vh, ci if d == 0 else n - 1 - ci) for d in range(2) for vh in range(nv)]
        boths = [_dot(wq_ref[d, vh, cc], s_ref[d, vh].astype(BF16)) for d, vh, cc in units]
        v_news = [(u_ref[d, vh, cc] - both[:ck]).astype(BF16) for (d, vh, cc), both in zip(units, boths)]
        for i in range(0, len(units), 2):
            d, vh, cc = units[i]
            o_ref = of_ref if d == 0 else ob_ref
            rows = pl.ds(pl.multiple_of(cc * ck, ck), ck)
            zero = jnp.zeros_like(v_news[i])
            v2 = jnp.concatenate([jnp.concatenate([v_news[i], zero], axis=1),
                                  jnp.concatenate([zero, v_news[i + 1]], axis=1)], axis=0)
            o2 = jnp.concatenate([boths[i][ck:], boths[i + 1][ck:]], axis=1) + _dot(qkm_ref[d, vh // 2, cc], v2)
            o_ref[rows, vh * hd:(vh + 2) * hd] = o2.astype(o_ref.dtype)
        for (d, vh, cc), v_newb in zip(units, v_news):
            scale = et_ref[d, cc][:, nv * d + vh:nv * d + vh + 1]
            s_ref[d, vh] = s_ref[d, vh] * scale + lax.dot_general(
                kd_ref[d, vh, cc], v_newb, TN_DIMS, preferred_element_type=F32)
        return carry

    lax.fori_loop(0, n, solve_chunk, 0)
    lax.fori_loop(0, n, scan_chunk, 0)


def _gdn_core(act, small):
    L = act.shape[0]
    ck = GDN_CHUNK
    rb = min(GDN_ROWS, L)
    nb = L // rb
    nv = GDN_V_HEADS
    hd = GDN_HEAD_DIM
    n = rb // ck
    g_row = small[:, :4 * nv].reshape(L // ck, ck, 4 * nv).transpose(0, 2, 1)
    col_id = jnp.arange(SMALL_CH)[:, None]
    expand = lambda w: jnp.stack([col_id == jnp.arange(nv * w)[None, :] // w + nv * d
                                  for d in range(2)]).astype(BF16)
    ex, ex_half = expand(hd), expand(ck)
    fwd = lambda b: (b, 0)
    bwd = lambda b: (nb - 1 - b, 0)
    return pl.pallas_call(
        functools.partial(_gdn_core_kernel, rb=rb, ck=ck),
        grid=(nb,),
        in_specs=[
            pl.BlockSpec((rb, CONV_CH), fwd),
            pl.BlockSpec((rb, CONV_CH), bwd),
            pl.BlockSpec((rb, SMALL_CH), fwd),
            pl.BlockSpec((rb, SMALL_CH), bwd),
            pl.BlockSpec((n, 4 * nv, ck), lambda b: (b, 0, 0)),
            pl.BlockSpec((n, 4 * nv, ck), lambda b: (nb - 1 - b, 0, 0)),
            _resident(ex.shape),
            _resident(ex_half.shape),
        ],
        out_specs=[pl.BlockSpec((rb, GDN_VALUE_DIM), fwd), pl.BlockSpec((rb, GDN_VALUE_DIM), bwd)],
        out_shape=[jax.ShapeDtypeStruct((L, GDN_VALUE_DIM), BF16)] * 2,
        scratch_shapes=[
            pltpu.VMEM((2, nv, hd, hd), F32),
            pltpu.VMEM((2, nv, n, ck, hd), F32),
            pltpu.VMEM((2, nv, n, 2 * ck, hd), BF16),
            pltpu.VMEM((2, nv, n, ck, hd), BF16),
            pltpu.VMEM((2, nv // 2, n, ck, 2 * ck), BF16),
            pltpu.VMEM((2, n, 1, SMALL_CH), F32),
        ],
        compiler_params=_params("arbitrary"),
        name="gdn_core",
    )(act, act, small, small, g_row, g_row, ex, ex_half)


def _gdn_out_kernel(of_ref, ob_ref, z_ref, h_ref, hnw_ref, wo_ref, pnw_ref, *ffn_refs):
    o = of_ref[...].astype(F32) + ob_ref[...].astype(F32)
    z = z_ref[...].astype(F32)
    hd = GDN_HEAD_DIM
    hnw = hnw_ref[...]
    parts = []
    for v in range(GDN_VALUE_DIM // hd):
        blk = o[:, v * hd:(v + 1) * hd]
        parts.append(_rms(blk, hnw) * _silu(z[:, v * hd:(v + 1) * hd]))
    m = _dot(jnp.concatenate(parts, axis=1).astype(BF16), wo_ref[...])
    _ffn_block(h_ref[...] + _rms(m, pnw_ref[...]), *ffn_refs)


def _gdn_layer(h, nw_pre, nw_post, w_in, conv_w, conv_b, dt_bias, a_log, norm_w, w_out, ffn):
    wc, wz, ws, cw, cb = _split_in_weights(w_in, conv_w, conv_b, conv_first=True)
    act, z, small = _inproj(h, nw_pre.reshape(1, -1), wc, wz, ws, cw, cb, _pad_small(dt_bias), _pad_small(a_log),
                            gdn=True)
    of, ob = _gdn_core(act, small)
    consts = (norm_w.reshape(1, -1), w_out.astype(BF16), nw_post.reshape(1, -1))
    return _mixer_out_call(_gdn_out_kernel, "gdn_out_ffn", h, (of, ob, z), consts, ffn)


def _s5_weights_kernel(lre_ref, lim_ref, lst_ref, bre_ref, bim_ref, cre_ref, cim_ref, win_ref, wout_ref):
    lre, lim = lre_ref[0, 0], lim_ref[0, 0]
    step = jnp.exp(lst_ref[0, 0])
    mag, ang = jnp.exp(lre * step), lim * step
    lbr, lbi = mag * jnp.cos(ang), mag * jnp.sin(ang)
    inv_den = 1.0 / (lre * lre + lim * lim)
    ar, ai, br, bi = lbr - 1.0, lbi, lre * inv_den, -lim * inv_den
    zr, zi = ar * br - ai * bi, ar * bi + ai * br
    b_re, b_im = bre_ref[0], bim_ref[0]
    rows, n = lre.shape
    w = (rows // S5_GROUP) * n
    assert n & (n - 1) == 0 and S5_GROUP & (S5_GROUP - 1) == 0
    spread = (lax.broadcasted_iota(jnp.int32, (n, w), 1) & (n - 1)) == lax.broadcasted_iota(jnp.int32, (n, w), 0)
    own = ((lax.broadcasted_iota(jnp.int32, (rows, w), 0) >> (S5_GROUP.bit_length() - 1))
           == (lax.broadcasted_iota(jnp.int32, (rows, w), 1) >> (n.bit_length() - 1)))

    def block_diag(t):
        tiled = _dot(t.astype(BF16), jnp.where(spread, 1.0, 0.0).astype(BF16))
        return jnp.where(own, tiled, 0.0)

    win_ref[0, 0, :, 0:w] = block_diag(zr * b_re - zi * b_im).astype(BF16)
    win_ref[0, 0, :, w:2 * w] = block_diag(zr * b_im + zi * b_re).astype(BF16)
    wout_ref[0, 0, 0:w, :] = block_diag(cre_ref[0, 0]).T.astype(BF16)
    wout_ref[0, 0, w:2 * w, :] = block_diag(-cim_ref[0, 0]).T.astype(BF16)


def _s5_weights(lam_re, lam_im, log_step, b_re, b_im, c_re, c_im):
    g, n, c = S5_GROUPS, S5_STATE, S5_GROUP
    per = g // S5_KB
    rows, w = per * c, per * n
    by_row = lambda t: jnp.broadcast_to(t.reshape(2, S5_KB, per, 1, n), (2, S5_KB, per, c, n)).reshape(
        2, S5_KB, rows, n)
    lst = jnp.broadcast_to(log_step[:, :, None], (2, g, n))
    b_rows = lambda t: t.transpose(0, 2, 1).reshape(S5_KB, rows, n)
    c_rows = lambda t: t.reshape(2, S5_KB, rows, n)
    per_dir = pl.BlockSpec((1, 1, rows, n), lambda d, k: (d, k, 0, 0))
    shared = pl.BlockSpec((1, rows, n), lambda d, k: (k, 0, 0))
    return pl.pallas_call(
        _s5_weights_kernel,
        grid=(2, S5_KB),
        in_specs=[per_dir, per_dir, per_dir, shared, shared, per_dir, per_dir],
        out_specs=[pl.BlockSpec((1, 1, rows, 2 * w), lambda d, k: (d, k, 0, 0)),
                   pl.BlockSpec((1, 1, 2 * w, rows), lambda d, k: (d, k, 0, 0))],
        out_shape=[jax.ShapeDtypeStruct((2, S5_KB, rows, 2 * w), BF16),
                   jax.ShapeDtypeStruct((2, S5_KB, 2 * w, rows), BF16)],
        compiler_params=_params("arbitrary", "arbitrary"),
        name="s5_weights",
    )(by_row(lam_re), by_row(lam_im), by_row(lst), b_rows(b_re), b_rows(b_im), c_rows(c_re), c_rows(c_im))


def _s5_prep_kernel(lre4_ref, lim4_ref, lst4_ref, pw_ref):
    r = lax.broadcasted_iota(jnp.int32, (2 * S5_TILE, S5_LANES), 0)
    coarse = r < S5_TILE
    rr = jnp.where(coarse, r, r - S5_TILE).astype(F32)
    mult = jnp.where(coarse, float(S5_TILE), 1.0)
    for d in range(2):
        k = mult * (rr + 1.0 if d == 0 else float(S5_TILE) - rr)
        st = jnp.exp(lst4_ref[d])
        m, a = jnp.exp(k * (lre4_ref[d] * st)), k * (lim4_ref[d] * st)
        pw_ref[d, 0] = m * jnp.cos(a)
        pw_ref[d, 1] = m * jnp.sin(a)


def _s5_prep(lam_re, lam_im, log_step):
    g, n = S5_GROUPS, S5_STATE
    lst = jnp.broadcast_to(log_step[:, :, None], (2, g, n))
    flat = lambda t: t.reshape(2, 1, g * n)
    full = lambda s: pl.BlockSpec(s, lambda: (0,) * len(s))
    return pl.pallas_call(
        _s5_prep_kernel,
        in_specs=[full((2, 1, g * n))] * 3,
        out_specs=full((2, 2, 2 * S5_TILE, g * n)),
        out_shape=jax.ShapeDtypeStruct((2, 2, 2 * S5_TILE, g * n), F32),
        name="s5_prep",
    )(flat(lam_re), flat(lam_im), flat(lst))


def _s5_core_kernel(hf_ref, hb_ref, nw_ref, win_ref, wout_ref, pw_ref, yf_ref, yb_ref,
                    car_ref, hsc_ref, ysc_ref, *, rt):
    @pl.when(pl.program_id(0) == 0)
    def _():
        car_ref[...] = jnp.zeros_like(car_ref)

    nw = nw_ref[...]
    ts = S5_TILE
    nt = rt // ts
    w = S5_LANES // S5_KB
    cin = D_MODEL // S5_KB
    rowi = lax.broadcasted_iota(jnp.int32, (ts, w), 0)
    tile = lambda v: jnp.broadcast_to(v, (ts, w))

    def cmul_add(a_r, a_i, s_r, s_i, x_r, x_i):
        return a_r * s_r - a_i * s_i + x_r, a_r * s_i + a_i * s_r + x_i

    jobs = [(d, kb) for d in range(2) for kb in range(S5_KB)]
    refs = ((hf_ref, yf_ref), (hb_ref, yb_ref))
    lanes = 128
    nl = D_MODEL // lanes
    hn = []
    for d in range(2):
        for c in range(nl):
            hsc_ref[c] = refs[d][0][:, c * lanes:(c + 1) * lanes]
        hn.append([_rms(jnp.concatenate([hsc_ref[c, pl.ds(t, rt, stride=ts), :] for c in range(nl)], axis=1),
                        nw).astype(BF16) for t in range(ts)])

    def drive(d, kb):
        lhs = jnp.concatenate([hn[d][t][:, kb * cin:(kb + 1) * cin] for t in range(ts)], axis=0)
        return _dot(lhs, win_ref[d, kb])

    x_next = drive(*jobs[0])
    for idx, (d, kb) in enumerate(jobs):
        x = x_next
        if idx + 1 < len(jobs):
            x_next = drive(*jobs[idx + 1])
        y_ref = refs[d][1]
        xr = [x[t * rt:(t + 1) * rt, :w] for t in range(ts)]
        xi = [x[t * rt:(t + 1) * rt, w:] for t in range(ts)]
        p_r = pw_ref[d, 0, 0:ts, kb * w:(kb + 1) * w]
        p_i = pw_ref[d, 1, 0:ts, kb * w:(kb + 1) * w]
        one = ts if d == 0 else 2 * ts - 1
        a_r = pw_ref[d, 0, one:one + 1, kb * w:(kb + 1) * w]
        a_i = pw_ref[d, 1, one:one + 1, kb * w:(kb + 1) * w]
        order = list(range(ts)) if d == 0 else list(range(ts - 1, -1, -1))

        e_r, e_i = xr[order[0]], xi[order[0]]
        for t in order[1:]:
            e_r, e_i = cmul_add(a_r, a_i, e_r, e_i, xr[t], xi[t])

        consts = []
        for k in (1, 2, 4):
            src = k - 1 if d == 0 else ts - k
            valid = rowi >= k if d == 0 else rowi <= ts - 1 - k
            consts.append((jnp.where(valid, tile(p_r[src:src + 1]), 0.0),
                           jnp.where(valid, tile(p_i[src:src + 1]), 0.0)))
        last, edge, back = (ts - 1, 0, 1) if d == 0 else (0, ts - 1, ts - 1)
        c_r, c_i = car_ref[d, kb, 0], car_ref[d, kb, 1]
        in_r, in_i = [None] * nt, [None] * nt
        for jj in range(nt):
            j = jj if d == 0 else nt - 1 - jj
            s_r, s_i = e_r[j * ts:(j + 1) * ts], e_i[j * ts:(j + 1) * ts]
            for k, (k_r, k_i) in zip((1, 2, 4), consts):
                sh = k if d == 0 else ts - k
                t_r, t_i = pltpu.roll(s_r, sh, 0), pltpu.roll(s_i, sh, 0)
                s_r, s_i = s_r + k_r * t_r - k_i * t_i, s_i + k_r * t_i + k_i * t_r
            s_r, s_i = s_r + p_r * c_r - p_i * c_i, s_i + p_r * c_i + p_i * c_r
            in_r[j] = jnp.where(rowi == edge, c_r, pltpu.roll(s_r, back, 0))
            in_i[j] = jnp.where(rowi == edge, c_i, pltpu.roll(s_i, back, 0))
            c_r, c_i = tile(s_r[last:last + 1]), tile(s_i[last:last + 1])
        car_ref[d, kb, 0] = c_r
        car_ref[d, kb, 1] = c_i

        s_r, s_i = jnp.concatenate(in_r, axis=0), jnp.concatenate(in_i, axis=0)
        states = [None] * ts
        for t in order:
            s_r, s_i = cmul_add(a_r, a_i, s_r, s_i, xr[t], xi[t])
            states[t] = jnp.concatenate([s_r, s_i], axis=1).astype(BF16)
        yk = _dot(jnp.concatenate(states, axis=0), wout_ref[d, kb])
        for t in range(ts):
            for c in range(cin // lanes):
                ysc_ref[kb * (cin // lanes) + c, pl.ds(t, rt, stride=ts), :] = (
                    yk[t * rt:(t + 1) * rt, c * lanes:(c + 1) * lanes])
        if kb == S5_KB - 1:
            for c in range(nl):
                y_ref[:, c * lanes:(c + 1) * lanes] = ysc_ref[c].astype(y_ref.dtype)


def _s5_core(h, nw, w_in, w_out, pw):
    L = h.shape[0]
    rb = min(S5_ROWS, L)
    nb = L // rb
    w = S5_LANES // S5_KB
    ts = S5_TILE
    rt = rb // ts
    fwd = lambda b: (b, 0)
    bwd = lambda b: (nb - 1 - b, 0)
    return pl.pallas_call(
        functools.partial(_s5_core_kernel, rt=rt),
        grid=(nb,),
        in_specs=[
            pl.BlockSpec((rb, D_MODEL), fwd),
            pl.BlockSpec((rb, D_MODEL), bwd),
            _resident((1, D_MODEL)),
            _resident(w_in.shape),
            _resident(w_out.shape),
            _resident(pw.shape),
        ],
        out_specs=[pl.BlockSpec((rb, D_MODEL), fwd), pl.BlockSpec((rb, D_MODEL), bwd)],
        out_shape=[jax.ShapeDtypeStruct((L, D_MODEL), BF16)] * 2,
        scratch_shapes=[pltpu.VMEM((2, S5_KB, 2, ts, w), F32),
                        pltpu.VMEM((D_MODEL // 128, rb, 128), F32),
                        pltpu.VMEM((D_MODEL // 128, rb, 128), F32)],
        compiler_params=_params("arbitrary"),
        name="s5_core",
    )(h, h, nw, w_in, w_out, pw)


def _s5_out_kernel(yf_ref, yb_ref, h_ref, nw_ref, dsk_ref, wg_ref, bg_ref, pnw_ref, *ffn_refs):
    h = h_ref[...]
    hn = _rms(h, nw_ref[...])
    y = yf_ref[...].astype(F32) + yb_ref[...].astype(F32) + dsk_ref[...] * hn
    y = jax.nn.gelu(y)
    vg = _dot(y.astype(BF16), wg_ref[...]) + bg_ref[...]
    m = vg[:, :D_MODEL] * jax.nn.sigmoid(vg[:, D_MODEL:])
    _ffn_block(h + _rms(m, pnw_ref[...]), *ffn_refs)


def _s5_layer(h, nw_pre, nw_post, lam_re, lam_im, log_step, b_re, b_im, c_re, c_im, d_skip, w_glu, b_glu,
              ffn):
    nw = nw_pre.reshape(1, -1)
    pw = _s5_prep(lam_re, lam_im, log_step)
    w_in, w_out = _s5_weights(lam_re, lam_im, log_step, b_re, b_im, c_re, c_im)
    yf, yb = _s5_core(h, nw, w_in, w_out, pw)
    consts = (nw, d_skip.reshape(1, -1), w_glu.astype(BF16), b_glu.reshape(1, -1), nw_post.reshape(1, -1))
    return _mixer_out_call(_s5_out_kernel, "s5_out_ffn", h, (yf, yb), consts, ffn)


def kernel(x, norm_w, ssd_w_in, ssd_conv_w, ssd_conv_b, ssd_dt_bias, ssd_a_log, ssd_d, ssd_norm_w, ssd_w_out, gdn_w_in, gdn_conv_w, gdn_conv_b, gdn_dt_bias, gdn_a_log, gdn_norm_w, gdn_w_out, s5_lam_re, s5_lam_im, s5_log_step, s5_b_re, s5_b_im, s5_c_re, s5_c_im, s5_d, s5_w_glu, s5_b_glu, ffn_w_gate_up, ffn_w_down):
    bt, L, d = x.shape
    assert bt == 1 and d == D_MODEL
    h = x.reshape(L, d)
    for i in range(norm_w.shape[0]):
        kind, j = i % 3, i // 3
        ffn = (norm_w[i, 2], norm_w[i, 3], ffn_w_gate_up[i], ffn_w_down[i])
        if kind == 0:
            h = _ssd_layer(h, norm_w[i, 0], norm_w[i, 1], ssd_w_in[j], ssd_conv_w[j], ssd_conv_b[j],
                           ssd_dt_bias[j], ssd_a_log[j], ssd_d[j], ssd_norm_w[j], ssd_w_out[j], ffn)
        elif kind == 1:
            h = _gdn_layer(h, norm_w[i, 0], norm_w[i, 1], gdn_w_in[j], gdn_conv_w[j], gdn_conv_b[j],
                           gdn_dt_bias[j], gdn_a_log[j], gdn_norm_w[j], gdn_w_out[j], ffn)
        else:
            h = _s5_layer(h, norm_w[i, 0], norm_w[i, 1], s5_lam_re[j], s5_lam_im[j], s5_log_step[j],
                          s5_b_re[j], s5_b_im[j], s5_c_re[j], s5_c_im[j], s5_d[j], s5_w_glu[j], s5_b_glu[j], ffn)
    return h.reshape(bt, L, d)
```

```python
import functools

import jax
import jax.numpy as jnp
from jax import lax
from jax.experimental import pallas as pl
from jax.experimental.pallas import tpu as pltpu

F32 = jnp.float32
BF16 = jnp.bfloat16
HIGHEST = lax.Precision.HIGHEST

D_MODEL = 1024
NORM_EPS = 1e-6
CONV_TAPS = 5
CONV_CH = 4096
Z_CH = 2048
SMALL_CH = 128
HALO = 8
CONV_CHUNK = 512

SSD_HEADS = 32
SSD_HEAD_DIM = 64
SSD_GROUPS = 8
SSD_STATE = 128
SSD_CHUNK = 128
SSD_D_INNER = 2048

GDN_HEAD_DIM = 128
GDN_QK_HEADS = 8
GDN_V_HEADS = 16
GDN_KEY_DIM = GDN_QK_HEADS * GDN_HEAD_DIM
GDN_CHUNK = 64
GDN_VALUE_DIM = 2048

S5_GROUPS = 64
S5_GROUP = 16
S5_STATE = 64
S5_LANES = S5_GROUPS * S5_STATE
S5_KB = 4
S5_TILE = 8

D_FF = 2816
FFN_CHUNK = 256

ROW_TILE = 512
GDN_ROWS = 256
S5_ROWS = 512
VMEM_LIMIT_BYTES = 56 * 1024 * 1024

NT_DIMS = (((1,), (1,)), ((), ()))
TN_DIMS = (((0,), (0,)), ((), ()))


def _params(*semantics):
    return pltpu.CompilerParams(dimension_semantics=semantics, vmem_limit_bytes=VMEM_LIMIT_BYTES)


def _resident(shape):
    nd = len(shape)
    return pl.BlockSpec(shape, lambda *_: (0,) * nd, pipeline_mode=pl.Buffered(1))


def _rms(x, w):
    return x * lax.rsqrt(jnp.mean(x * x, axis=-1, keepdims=True) + NORM_EPS) * w


def _silu(x):
    return x * jax.nn.sigmoid(x)


def _dot(a, b):
    return jnp.dot(a, b, preferred_element_type=F32)


def _inproj_kernel(h_ref, hp_ref, hx_ref, nw_ref, wc_ref, wz_ref, ws_ref, cw_ref, cb_ref, sb_ref, sa_ref,
                   act_ref, z_ref, s_ref, hs_ref, ys_ref, *, tm, nblk, cchunk, gdn):
    i = pl.program_id(0)
    nw = nw_ref[...]
    curb = _rms(h_ref[...], nw).astype(BF16)
    z_ref[...] = _dot(curb, wz_ref[...]).astype(z_ref.dtype)
    raw = _dot(curb, ws_ref[...]) + sb_ref[...]
    if gdn:
        is_decay = lax.broadcasted_iota(jnp.int32, raw.shape, 1) < 2 * GDN_V_HEADS
        s_ref[...] = jnp.where(is_decay, -jnp.exp(sa_ref[...]) * jax.nn.softplus(raw), jax.nn.sigmoid(raw))
    else:
        s_ref[...] = jax.nn.softplus(raw)
    pad = CONV_TAPS // 2
    hd = GDN_HEAD_DIM
    lanes = 128
    rows = tm + 2 * HALO
    seg = rows // 8
    for s in range(D_MODEL // lanes):
        ls = slice(s * lanes, (s + 1) * lanes)
        hs_ref[s, 0:HALO, :] = hp_ref[:, ls]
        hs_ref[s, HALO:HALO + tm, :] = h_ref[:, ls]
        hs_ref[s, HALO + tm:rows, :] = hx_ref[:, ls]
    ext = jnp.concatenate(
        [jnp.concatenate([hs_ref[s, pl.ds(j, 8, stride=seg), :] for s in range(D_MODEL // lanes)], axis=1)
         for j in range(seg)], axis=0)
    r = lax.broadcasted_iota(jnp.int32, (rows, 1), 0)
    orig = (r % 8) * seg + r // 8
    inside = jnp.logical_and(jnp.logical_or(orig >= HALO, i > 0),
                             jnp.logical_or(orig < HALO + tm, i < nblk - 1))
    ext = jnp.where(inside, _rms(ext, nw), 0.0).astype(BF16)
    for c in range(CONV_CH // cchunk):
        sl = slice(c * cchunk, (c + 1) * cchunk)
        p = _dot(ext, wc_ref[:, sl])
        head = [pltpu.roll(p[(seg - pad + m) * 8:(seg - pad + m + 1) * 8], 1, 0) for m in range(pad)]
        tail = [pltpu.roll(p[m * 8:(m + 1) * 8], 7, 0) for m in range(pad)]
        p_ext = jnp.concatenate(head + [p] + tail, axis=0)
        acc = cb_ref[:, sl]
        for k in range(CONV_TAPS):
            acc = acc + cw_ref[k:k + 1, sl] * p_ext[8 * k:8 * k + rows]
        y = _silu(acc)
        if gdn and c * cchunk < 2 * GDN_KEY_DIM:
            parts = []
            for s in range(cchunk // hd):
                blk = y[:, s * hd:(s + 1) * hd]
                inv = lax.rsqrt(jnp.sum(blk * blk, axis=-1, keepdims=True) + NORM_EPS)
                if c * cchunk + s * hd < GDN_KEY_DIM:
                    inv = inv * (hd ** -0.5)
                parts.append(blk * inv)
            y = jnp.concatenate(parts, axis=1)
        for s in range(cchunk // lanes):
            for j in range(seg):
                ys_ref[s, pl.ds(j, 8, stride=seg), :] = y[8 * j:8 * j + 8, s * lanes:(s + 1) * lanes]
        for s in range(cchunk // lanes):
            act_ref[:, c * cchunk + s * lanes:c * cchunk + (s + 1) * lanes] = (
                ys_ref[s, HALO:HALO + tm, :].astype(act_ref.dtype))


def _inproj(h, nw, wc, wz, ws, conv_w, conv_b, s_bias, s_alog, gdn):
    L = h.shape[0]
    tm = min(ROW_TILE, L)
    nblk = L // tm
    per = tm // HALO
    last = L // HALO - 1
    kern = functools.partial(_inproj_kernel, tm=tm, nblk=nblk, cchunk=CONV_CHUNK, gdn=gdn)
    return pl.pallas_call(
        kern,
        grid=(nblk,),
        in_specs=[
            pl.BlockSpec((tm, D_MODEL), lambda i: (i, 0)),
            pl.BlockSpec((HALO, D_MODEL), lambda i: (jnp.maximum(i * per - 1, 0), 0)),
            pl.BlockSpec((HALO, D_MODEL), lambda i: (jnp.minimum((i + 1) * per, last), 0)),
            _resident((1, D_MODEL)),
            _resident((D_MODEL, CONV_CH)),
            _resident((D_MODEL, Z_CH)),
            _resident((D_MODEL, SMALL_CH)),
            _resident((8, CONV_CH)),
            _resident((1, CONV_CH)),
            _resident((1, SMALL_CH)),
            _resident((1, SMALL_CH)),
        ],
        out_specs=[
            pl.BlockSpec((tm, CONV_CH), lambda i: (i, 0)),
            pl.BlockSpec((tm, Z_CH), lambda i: (i, 0)),
            pl.BlockSpec((tm, SMALL_CH), lambda i: (i, 0)),
        ],
        out_shape=[
            jax.ShapeDtypeStruct((L, CONV_CH), BF16),
            jax.ShapeDtypeStruct((L, Z_CH), BF16),
            jax.ShapeDtypeStruct((L, SMALL_CH), F32),
        ],
        scratch_shapes=[pltpu.VMEM((D_MODEL // 128, tm + 2 * HALO, 128), F32),
                        pltpu.VMEM((CONV_CHUNK // 128, tm + 2 * HALO, 128), F32)],
        compiler_params=_params("arbitrary"),
        name="inproj_conv",
    )(h, h, h, nw, wc, wz, ws, conv_w, conv_b, s_bias, s_alog)


def _pad_small(t):
    t = t.reshape(1, -1)
    return jnp.pad(t, ((0, 0), (0, SMALL_CH - t.shape[1])))


def _split_in_weights(w_in, conv_w, conv_b, conv_first):
    if conv_first:
        wc, wz = w_in[:, :CONV_CH], w_in[:, CONV_CH:CONV_CH + Z_CH]
    else:
        wz, wc = w_in[:, :Z_CH], w_in[:, Z_CH:Z_CH + CONV_CH]
    ws = w_in[:, CONV_CH + Z_CH:]
    ws = jnp.pad(ws, ((0, 0), (0, SMALL_CH - ws.shape[1])))
    cw = jnp.pad(conv_w, ((0, 8 - CONV_TAPS), (0, 0)))
    return wc.astype(BF16), wz.astype(BF16), ws.astype(BF16), cw, conv_b.reshape(1, CONV_CH)


def _ssd_core_kernel(xf_ref, xb_ref, dtf_ref, dtb_ref, alog_ref, dskip_ref, exb_ref,
                     yf_ref, yb_ref, sf_ref, sb_ref, *, q):
    @pl.when(pl.program_id(0) == 0)
    def _():
        sf_ref[...] = jnp.zeros_like(sf_ref)
        sb_ref[...] = jnp.zeros_like(sb_ref)

    row = lax.broadcasted_iota(jnp.int32, (q, q), 0)
    col = lax.broadcasted_iota(jnp.int32, (q, q), 1)
    lower = row >= col
    upper = row <= col
    lower_f = lower.astype(F32)
    upper_f = upper.astype(F32)
    a = -jnp.exp(alog_ref[...])
    p = SSD_HEAD_DIM
    per_group = SSD_HEADS // SSD_GROUPS
    gw = per_group * p
    pw = 2 * p
    b_off = SSD_D_INNER
    c_off = SSD_D_INNER + SSD_GROUPS * SSD_STATE
    left = lax.broadcasted_iota(jnp.int32, (q, pw), 1) < p

    dirs = ((xf_ref, dtf_ref, yf_ref, sf_ref), (xb_ref, dtb_ref, yb_ref, sb_ref))
    pre = []
    for d, (x_ref, dt_ref, y_ref, s_ref) in enumerate(dirs):
        dt = dt_ref[...]
        da = dt * a
        tri_c, tri_r = (lower_f, upper_f) if d == 0 else (upper_f, lower_f)
        cs = jnp.dot(tri_c, da, precision=HIGHEST, preferred_element_type=F32)
        cs_t = jnp.dot(da.T, tri_r, precision=HIGHEST, preferred_element_type=F32)
        tot = cs[q - 1:q, :] if d == 0 else cs[0:1, :]
        exb = exb_ref[d]
        dt_x = _dot(dt.astype(BF16), exb)
        e_x = _dot(jnp.exp(cs).astype(BF16), exb)
        w_x = _dot((dt * jnp.exp(tot - cs)).astype(BF16), exb)
        t_hi = tot.astype(BF16)
        t_mid = (tot - t_hi.astype(F32)).astype(BF16)
        t_lo = (tot - t_hi.astype(F32) - t_mid.astype(F32)).astype(BF16)
        pieces = jnp.concatenate([jnp.broadcast_to(t, (8, SMALL_CH)) for t in (t_hi, t_mid, t_lo)], axis=0)
        tot_3 = _dot(pieces, exb)
        tot_x = tot_3[0:1] + tot_3[8:9] + tot_3[16:17]
        pre.append((cs, cs_t, dt_x, e_x, w_x, jnp.exp(tot_x)))

    def front(d, g):
        x_ref, s_ref = dirs[d][0], dirs[d][3]
        bmat = x_ref[:, b_off + g * SSD_STATE:b_off + (g + 1) * SSD_STATE]
        cmat = x_ref[:, c_off + g * SSD_STATE:c_off + (g + 1) * SSD_STATE]
        cb = lax.dot_general(cmat, bmat, NT_DIMS, preferred_element_type=F32)
        yoff = _dot(cmat, s_ref[g].astype(BF16))
        return bmat, cb, yoff

    jobs = [(d, g) for d in range(2) for g in range(SSD_GROUPS)]
    ahead = front(*jobs[0])
    for idx, (d, g) in enumerate(jobs):
        bmat, cb, yoff = ahead
        if idx + 1 < len(jobs):
            ahead = front(*jobs[idx + 1])
        xg = dirs[d][0][:, g * gw:(g + 1) * gw].astype(F32)
        cs, cs_t, dt_x, e_x, w_x, cdec_x = pre[d]
        mask = lower if d == 0 else upper
        y_ref = dirs[d][2]
        for pr in range(per_group // 2):
            h0 = per_group * g + 2 * pr
            lms = []
            for hh in (h0, h0 + 1):
                ch = hh + SSD_HEADS * d
                lms.append(cb * jnp.exp(jnp.where(mask, cs[:, ch:ch + 1] - cs_t[ch:ch + 1, :], -jnp.inf)))
            m_pair = jnp.concatenate(lms, axis=1).astype(BF16)
            sl = slice(h0 * p, h0 * p + pw)
            lo = slice(2 * pr * p, 2 * pr * p + pw)
            xp = xg[:, lo]
            xdt = xp * dt_x[:, sl]
            rhs = jnp.concatenate([jnp.where(left, xdt, 0.0), jnp.where(left, 0.0, xdt)], axis=0).astype(BF16)
            y = _dot(m_pair, rhs) + yoff[:, lo] * e_x[:, sl]
            if d == 0:
                y = y + xp * dskip_ref[:, sl]
            y_ref[:, sl] = y.astype(y_ref.dtype)
        s_ref = dirs[d][3]
        gs = slice(g * gw, (g + 1) * gw)
        new = lax.dot_general(bmat, (xg * w_x[:, gs]).astype(BF16), TN_DIMS, preferred_element_type=F32)
        s_ref[g] = s_ref[g] * cdec_x[:, gs] + new


def _ssd_core(act, small, a_log, d_skip):
    L = act.shape[0]
    q = SSD_CHUNK
    nc = L // q
    alog = _pad_small(a_log)
    dskip = jnp.repeat(d_skip, SSD_HEAD_DIM).reshape(1, SSD_D_INNER)
    head_of_channel = jnp.arange(SSD_D_INNER) // SSD_HEAD_DIM
    col_id = jnp.arange(SMALL_CH)
    expand = jnp.stack([col_id[:, None] == head_of_channel[None, :] + SSD_HEADS * d for d in range(2)])
    gw = SSD_D_INNER // SSD_GROUPS
    fwd = lambda c: (c, 0)
    bwd = lambda c: (nc - 1 - c, 0)
    return pl.pallas_call(
        functools.partial(_ssd_core_kernel, q=q),
        grid=(nc,),
        in_specs=[
            pl.BlockSpec((q, CONV_CH), fwd),
            pl.BlockSpec((q, CONV_CH), bwd),
            pl.BlockSpec((q, SMALL_CH), fwd),
            pl.BlockSpec((q, SMALL_CH), bwd),
            _resident((1, SMALL_CH)),
            _resident((1, SSD_D_INNER)),
            _resident((2, SMALL_CH, SSD_D_INNER)),
        ],
        out_specs=[pl.BlockSpec((q, SSD_D_INNER), fwd), pl.BlockSpec((q, SSD_D_INNER), bwd)],
        out_shape=[jax.ShapeDtypeStruct((L, SSD_D_INNER), BF16)] * 2,
        scratch_shapes=[pltpu.VMEM((SSD_GROUPS, SSD_STATE, gw), F32)] * 2,
        compiler_params=_params("arbitrary"),
        name="ssd_core",
    )(act, act, small, small, alog, dskip, expand.astype(BF16))


def _ssd_out_kernel(yf_ref, yb_ref, z_ref, h_ref, gnw_ref, wo_ref, pnw_ref, *ffn_refs):
    y = yf_ref[...].astype(F32) + yb_ref[...].astype(F32)
    yz = y * _silu(z_ref[...].astype(F32))
    gw = SSD_D_INNER // SSD_GROUPS
    parts = []
    for g in range(SSD_GROUPS):
        blk = yz[:, g * gw:(g + 1) * gw]
        parts.append(blk * lax.rsqrt(jnp.mean(blk * blk, axis=-1, keepdims=True) + NORM_EPS))
    yn = (jnp.concatenate(parts, axis=1) * gnw_ref[...]).astype(BF16)
    m = _dot(yn, wo_ref[...])
    _ffn_block(h_ref[...] + _rms(m, pnw_ref[...]), *ffn_refs)


def _ffn_block(h, nw_ref, wgu_ref, wd_ref, pnw_ref, o_ref, act_ref):
    hn = _rms(h, nw_ref[...]).astype(BF16)
    for c in range(D_FF // FFN_CHUNK):
        gate = _dot(hn, wgu_ref[:, c * FFN_CHUNK:(c + 1) * FFN_CHUNK])
        up = _dot(hn, wgu_ref[:, D_FF + c * FFN_CHUNK:D_FF + (c + 1) * FFN_CHUNK])
        act_ref[:, c * FFN_CHUNK:(c + 1) * FFN_CHUNK] = (_silu(gate) * up).astype(BF16)
    f = _dot(act_ref[...], wd_ref[...])
    o_ref[...] = h + _rms(f, pnw_ref[...])


def _mixer_out_call(kern, name, h, streams, consts, ffn):
    L = h.shape[0]
    tm = min(ROW_TILE, L)
    nw_pre, nw_post, w_gate_up, w_down = ffn
    consts = tuple(consts) + (nw_pre.reshape(1, -1), w_gate_up.astype(BF16), w_down.astype(BF16),
                              nw_post.reshape(1, -1))
    tiled = lambda a: pl.BlockSpec((tm, a.shape[1]), lambda i: (i, 0))
    return pl.pallas_call(
        kern,
        grid=(L // tm,),
        in_specs=[tiled(a) for a in streams] + [tiled(h)] + [_resident(c.shape) for c in consts],
        out_specs=tiled(h),
        out_shape=jax.ShapeDtypeStruct(h.shape, F32),
        scratch_shapes=[pltpu.VMEM((tm, D_FF), BF16)],
        compiler_params=_params("arbitrary"),
        name=name,
    )(*streams, h, *consts)


def _ssd_layer(h, nw_pre, nw_post, w_in, conv_w, conv_b, dt_bias, a_log, d_skip, norm_w, w_out, ffn):
    wc, wz, ws, cw, cb = _split_in_weights(w_in, conv_w, conv_b, conv_first=False)
    act, z, small = _inproj(h, nw_pre.reshape(1, -1), wc, wz, ws, cw, cb, _pad_small(dt_bias), _pad_small(a_log),
                            gdn=False)
    yf, yb = _ssd_core(act, small, a_log, d_skip)
    consts = (norm_w.reshape(1, -1), w_out.astype(BF16), nw_post.reshape(1, -1))
    return _mixer_out_call(_ssd_out_kernel, "ssd_out_ffn", h, (yf, yb, z), consts, ffn)


def _gdn_core_kernel(xf_ref, xb_ref, gcf_ref, gcb_ref, grf_ref, grb_ref, ex_ref,
                     of_ref, ob_ref, s_ref, u_ref, wq_ref, kd_ref, qkm_ref, et_ref, *, rb, ck):
    @pl.when(pl.program_id(0) == 0)
    def _():
        s_ref[...] = jnp.zeros_like(s_ref)

    n = rb // ck
    hd = GDN_HEAD_DIM
    nq, nv = GDN_QK_HEADS, GDN_V_HEADS
    k_off, v_off = nq * hd, 2 * nq * hd
    row = lax.broadcasted_iota(jnp.int32, (ck, ck), 0)
    col = lax.broadcasted_iota(jnp.int32, (ck, ck), 1)
    lower_f, upper_f = (row >= col).astype(F32), (row <= col).astype(F32)
    row2 = lax.broadcasted_iota(jnp.int32, (ck, 2 * ck), 0)
    lane2 = lax.broadcasted_iota(jnp.int32, (ck, 2 * ck), 1)
    left = lane2 < ck
    col2 = jnp.where(left, lane2, lane2 - ck)
    pair = lambda t: jnp.concatenate([t, t], axis=1)
    eye2 = (row2 == col2).astype(F32)

    def block_diag(p2):
        return jnp.concatenate([jnp.where(left, p2, 0.0), jnp.where(left, 0.0, p2)], axis=0).astype(BF16)

    def solve_chunk(ci, carry):
        pairs = []
        for d in range(2):
            x_ref, gc_ref, gr_ref = (xf_ref, gcf_ref, grf_ref) if d == 0 else (xb_ref, gcb_ref, grb_ref)
            cc = ci if d == 0 else n - 1 - ci
            rows = pl.ds(pl.multiple_of(cc * ck, ck), ck)
            tri_c, tri_r = (lower_f, upper_f) if d == 0 else (upper_f, lower_f)
            mask2, strict2 = (row2 >= col2, row2 > col2) if d == 0 else (row2 <= col2, row2 < col2)
            gb_c = gc_ref[rows, :]
            gb_r = gr_ref[cc]
            cs_c = jnp.dot(tri_c, gb_c, precision=HIGHEST, preferred_element_type=F32)
            cs_r = jnp.dot(gb_r, tri_r, precision=HIGHEST, preferred_element_type=F32)
            tot = cs_c[ck - 1:ck, :] if d == 0 else cs_c[0:1, :]
            et_ref[d, cc] = jnp.exp(tot)
            beta_al = pltpu.roll(gb_c, SMALL_CH - 2 * nv, axis=1)
            stacked = jnp.concatenate([beta_al.astype(BF16), jnp.exp(cs_c).astype(BF16),
                                       jnp.exp(tot - cs_c).astype(BF16)], axis=0)
            stacked_x = _dot(stacked, ex_ref[d])
            beta_x = stacked_x[:ck]
            e_x = stacked_x[ck:2 * ck]
            kf_x = stacked_x[2 * ck:]
            for hq in range(nq):
                qn = x_ref[rows, hq * hd:(hq + 1) * hd]
                kn = x_ref[rows, k_off + hq * hd:k_off + (hq + 1) * hd]
                gram = lax.dot_general(jnp.concatenate([kn, qn], axis=0), kn, NT_DIMS,
                                       preferred_element_type=F32)
                kk2, qk2 = pair(gram[:ck]), pair(gram[ck:])
                c0 = nv * d + 2 * hq
                gcs_c2 = jnp.where(left, cs_c[:, c0:c0 + 1], cs_c[:, c0 + 1:c0 + 2])
                gcs_r2 = jnp.concatenate([cs_r[c0:c0 + 1, :], cs_r[c0 + 1:c0 + 2, :]], axis=1)
                beta2 = jnp.where(left, beta_x[:, 2 * hq * hd:(2 * hq + 1) * hd],
                                  beta_x[:, (2 * hq + 1) * hd:(2 * hq + 2) * hd])
                decay2 = jnp.exp(jnp.where(mask2, gcs_c2 - gcs_r2, -jnp.inf))
                a2 = jnp.where(strict2, beta2 * kk2 * decay2, 0.0)
                qkm_ref[d, hq, cc] = (qk2 * decay2).astype(BF16)
                qf, kf = qn.astype(F32), kn.astype(F32)
                rhs = []
                for j in range(2):
                    vh = 2 * hq + j
                    hs = slice(vh * hd, (vh + 1) * hd)
                    beta_c, e_g = beta_x[:, hs], e_x[:, hs]
                    vx = x_ref[rows, v_off + vh * hd:v_off + (vh + 1) * hd].astype(F32)
                    rhs.append(jnp.concatenate([vx * beta_c, kf * (beta_c * e_g)], axis=1).astype(BF16))
                    wq_ref[d, vh, cc, ck:2 * ck, :] = (qf * e_g).astype(BF16)
                    kd_ref[d, vh, cc] = (kf * kf_x[:, hs]).astype(BF16)
                zero = jnp.zeros_like(rhs[0])
                rhs2 = jnp.concatenate([jnp.concatenate([rhs[0], zero], axis=1),
                                        jnp.concatenate([zero, rhs[1]], axis=1)], axis=0)
                pairs.append((d, hq, cc, a2, rhs2))
        pws = [_dot(a2.astype(BF16), block_diag(a2)) for (_, _, _, a2, _) in pairs]
        tis = [eye2 - a2 for (_, _, _, a2, _) in pairs]
        for level in range(5):
            if level < 4:
                res = [_dot(jnp.concatenate([t, p], axis=0).astype(BF16), block_diag(p)) for t, p in zip(tis, pws)]
                tis = [t + r[:ck] for t, r in zip(tis, res)]
                pws = [r[ck:] for r in res]
            else:
                tis = [t + _dot(t.astype(BF16), block_diag(p)) for t, p in zip(tis, pws)]
        for (d, hq, cc, _, rhs2), t in zip(pairs, tis):
            sol = _dot(t.astype(BF16), rhs2)
            for j in range(2):
                vh = 2 * hq + j
                u_ref[d, vh, cc] = sol[:, 2 * j * hd:(2 * j + 1) * hd]
                wq_ref[d, vh, cc, 0:ck, :] = sol[:, (2 * j + 1) * hd:(2 * j + 2) * hd].astype(BF16)
        return carry

    def scan_chunk(ci, carry):
        units = [(d, vh, ci if d == 0 else n - 1 - ci) for d in range(2) for vh in range(nv)]
        boths = [_dot(wq_ref[d, vh, cc], s_ref[d, vh].astype(BF16)) for d, vh, cc in units]
        v_news = [(u_ref[d, vh, cc] - both[:ck]).astype(BF16) for (d, vh, cc), both in zip(units, boths)]
        for i in range(0, len(units), 2):
            d, vh, cc = units[i]
            o_ref = of_ref if d == 0 else ob_ref
            rows = pl.ds(pl.multiple_of(cc * ck, ck), ck)
            zero = jnp.zeros_like(v_news[i])
            v2 = jnp.concatenate([jnp.concatenate([v_news[i], zero], axis=1),
                                  jnp.concatenate([zero, v_news[i + 1]], axis=1)], axis=0)
            o2 = jnp.concatenate([boths[i][ck:], boths[i + 1][ck:]], axis=1) + _dot(qkm_ref[d, vh // 2, cc], v2)
            o_ref[rows, vh * hd:(vh + 2) * hd] = o2.astype(o_ref.dtype)
        for (d, vh, cc), v_newb in zip(units, v_news):
            scale = et_ref[d, cc][:, nv * d + vh:nv * d + vh + 1]
            s_ref[d, vh] = s_ref[d, vh] * scale + lax.dot_general(
                kd_ref[d, vh, cc], v_newb, TN_DIMS, preferred_element_type=F32)
        return carry

    lax.fori_loop(0, n, solve_chunk, 0)
    lax.fori_loop(0, n, scan_chunk, 0)


def _gdn_core(act, small):
    L = act.shape[0]
    ck = GDN_CHUNK
    rb = min(GDN_ROWS, L)
    nb = L // rb
    nv = GDN_V_HEADS
    hd = GDN_HEAD_DIM
    n = rb // ck
    g_row = small[:, :4 * nv].reshape(L // ck, ck, 4 * nv).transpose(0, 2, 1)
    col_id = jnp.arange(SMALL_CH)[:, None]
    ex = jnp.stack([col_id == jnp.arange(nv * hd)[None, :] // hd + nv * d for d in range(2)]).astype(BF16)
    fwd = lambda b: (b, 0)
    bwd = lambda b: (nb - 1 - b, 0)
    return pl.pallas_call(
        functools.partial(_gdn_core_kernel, rb=rb, ck=ck),
        grid=(nb,),
        in_specs=[
            pl.BlockSpec((rb, CONV_CH), fwd),
            pl.BlockSpec((rb, CONV_CH), bwd),
            pl.BlockSpec((rb, SMALL_CH), fwd),
            pl.BlockSpec((rb, SMALL_CH), bwd),
            pl.BlockSpec((n, 4 * nv, ck), lambda b: (b, 0, 0)),
            pl.BlockSpec((n, 4 * nv, ck), lambda b: (nb - 1 - b, 0, 0)),
            _resident(ex.shape),
        ],
        out_specs=[pl.BlockSpec((rb, GDN_VALUE_DIM), fwd), pl.BlockSpec((rb, GDN_VALUE_DIM), bwd)],
        out_shape=[jax.ShapeDtypeStruct((L, GDN_VALUE_DIM), BF16)] * 2,
        scratch_shapes=[
            pltpu.VMEM((2, nv, hd, hd), F32),
            pltpu.VMEM((2, nv, n, ck, hd), F32),
            pltpu.VMEM((2, nv, n, 2 * ck, hd), BF16),
            pltpu.VMEM((2, nv, n, ck, hd), BF16),
            pltpu.VMEM((2, nv // 2, n, ck, 2 * ck), BF16),
            pltpu.VMEM((2, n, 1, SMALL_CH), F32),
        ],
        compiler_params=_params("arbitrary"),
        name="gdn_core",
    )(act, act, small, small, g_row, g_row, ex)


def _gdn_out_kernel(of_ref, ob_ref, z_ref, h_ref, hnw_ref, wo_ref, pnw_ref, *ffn_refs):
    o = of_ref[...].astype(F32) + ob_ref[...].astype(F32)
    z = z_ref[...].astype(F32)
    hd = GDN_HEAD_DIM
    hnw = hnw_ref[...]
    parts = []
    for v in range(GDN_VALUE_DIM // hd):
        blk = o[:, v * hd:(v + 1) * hd]
        parts.append(_rms(blk, hnw) * _silu(z[:, v * hd:(v + 1) * hd]))
    m = _dot(jnp.concatenate(parts, axis=1).astype(BF16), wo_ref[...])
    _ffn_block(h_ref[...] + _rms(m, pnw_ref[...]), *ffn_refs)


def _gdn_layer(h, nw_pre, nw_post, w_in, conv_w, conv_b, dt_bias, a_log, norm_w, w_out, ffn):
    wc, wz, ws, cw, cb = _split_in_weights(w_in, conv_w, conv_b, conv_first=True)
    act, z, small = _inproj(h, nw_pre.reshape(1, -1), wc, wz, ws, cw, cb, _pad_small(dt_bias), _pad_small(a_log),
                            gdn=True)
    of, ob = _gdn_core(act, small)
    consts = (norm_w.reshape(1, -1), w_out.astype(BF16), nw_post.reshape(1, -1))
    return _mixer_out_call(_gdn_out_kernel, "gdn_out_ffn", h, (of, ob, z), consts, ffn)


def _s5_weights_kernel(lre_ref, lim_ref, lst_ref, bre_ref, bim_ref, cre_ref, cim_ref, win_ref, wout_ref):
    lre, lim = lre_ref[0, 0], lim_ref[0, 0]
    step = jnp.exp(lst_ref[0, 0])
    mag, ang = jnp.exp(lre * step), lim * step
    lbr, lbi = mag * jnp.cos(ang), mag * jnp.sin(ang)
    inv_den = 1.0 / (lre * lre + lim * lim)
    ar, ai, br, bi = lbr - 1.0, lbi, lre * inv_den, -lim * inv_den
    zr, zi = ar * br - ai * bi, ar * bi + ai * br
    b_re, b_im = bre_ref[0], bim_ref[0]
    rows, n = lre.shape
    w = (rows // S5_GROUP) * n
    assert n & (n - 1) == 0 and S5_GROUP & (S5_GROUP - 1) == 0
    spread = (lax.broadcasted_iota(jnp.int32, (n, w), 1) & (n - 1)) == lax.broadcasted_iota(jnp.int32, (n, w), 0)
    own = ((lax.broadcasted_iota(jnp.int32, (rows, w), 0) >> (S5_GROUP.bit_length() - 1))
           == (lax.broadcasted_iota(jnp.int32, (rows, w), 1) >> (n.bit_length() - 1)))

    def block_diag(t):
        tiled = _dot(t.astype(BF16), jnp.where(spread, 1.0, 0.0).astype(BF16))
        return jnp.where(own, tiled, 0.0)

    win_ref[0, 0, :, 0:w] = block_diag(zr * b_re - zi * b_im).astype(BF16)
    win_ref[0, 0, :, w:2 * w] = block_diag(zr * b_im + zi * b_re).astype(BF16)
    wout_ref[0, 0, 0:w, :] = block_diag(cre_ref[0, 0]).T.astype(BF16)
    wout_ref[0, 0, w:2 * w, :] = block_diag(-cim_ref[0, 0]).T.astype(BF16)


def _s5_weights(lam_re, lam_im, log_step, b_re, b_im, c_re, c_im):
    g, n, c = S5_GROUPS, S5_STATE, S5_GROUP
    per = g // S5_KB
    rows, w = per * c, per * n
    by_row = lambda t: jnp.broadcast_to(t.reshape(2, S5_KB, per, 1, n), (2, S5_KB, per, c, n)).reshape(
        2, S5_KB, rows, n)
    lst = jnp.broadcast_to(log_step[:, :, None], (2, g, n))
    b_rows = lambda t: t.transpose(0, 2, 1).reshape(S5_KB, rows, n)
    c_rows = lambda t: t.reshape(2, S5_KB, rows, n)
    per_dir = pl.BlockSpec((1, 1, rows, n), lambda d, k: (d, k, 0, 0))
    shared = pl.BlockSpec((1, rows, n), lambda d, k: (k, 0, 0))
    return pl.pallas_call(
        _s5_weights_kernel,
        grid=(2, S5_KB),
        in_specs=[per_dir, per_dir, per_dir, shared, shared, per_dir, per_dir],
        out_specs=[pl.BlockSpec((1, 1, rows, 2 * w), lambda d, k: (d, k, 0, 0)),
                   pl.BlockSpec((1, 1, 2 * w, rows), lambda d, k: (d, k, 0, 0))],
        out_shape=[jax.ShapeDtypeStruct((2, S5_KB, rows, 2 * w), BF16),
                   jax.ShapeDtypeStruct((2, S5_KB, 2 * w, rows), BF16)],
        compiler_params=_params("arbitrary", "arbitrary"),
        name="s5_weights",
    )(by_row(lam_re), by_row(lam_im), by_row(lst), b_rows(b_re), b_rows(b_im), c_rows(c_re), c_rows(c_im))


def _s5_prep_kernel(lre4_ref, lim4_ref, lst4_ref, pw_ref):
    r = lax.broadcasted_iota(jnp.int32, (2 * S5_TILE, S5_LANES), 0)
    coarse = r < S5_TILE
    rr = jnp.where(coarse, r, r - S5_TILE).astype(F32)
    mult = jnp.where(coarse, float(S5_TILE), 1.0)
    for d in range(2):
        k = mult * (rr + 1.0 if d == 0 else float(S5_TILE) - rr)
        st = jnp.exp(lst4_ref[d])
        m, a = jnp.exp(k * (lre4_ref[d] * st)), k * (lim4_ref[d] * st)
        pw_ref[d, 0] = m * jnp.cos(a)
        pw_ref[d, 1] = m * jnp.sin(a)


def _s5_prep(lam_re, lam_im, log_step):
    g, n = S5_GROUPS, S5_STATE
    lst = jnp.broadcast_to(log_step[:, :, None], (2, g, n))
    flat = lambda t: t.reshape(2, 1, g * n)
    full = lambda s: pl.BlockSpec(s, lambda: (0,) * len(s))
    return pl.pallas_call(
        _s5_prep_kernel,
        in_specs=[full((2, 1, g * n))] * 3,
        out_specs=full((2, 2, 2 * S5_TILE, g * n)),
        out_shape=jax.ShapeDtypeStruct((2, 2, 2 * S5_TILE, g * n), F32),
        name="s5_prep",
    )(flat(lam_re), flat(lam_im), flat(lst))


def _s5_core_kernel(hf_ref, hb_ref, nw_ref, win_ref, wout_ref, pw_ref, yf_ref, yb_ref,
                    car_ref, hsc_ref, ysc_ref, *, rt):
    @pl.when(pl.program_id(0) == 0)
    def _():
        car_ref[...] = jnp.zeros_like(car_ref)

    nw = nw_ref[...]
    ts = S5_TILE
    nt = rt // ts
    w = S5_LANES // S5_KB
    cin = D_MODEL // S5_KB
    rowi = lax.broadcasted_iota(jnp.int32, (ts, w), 0)
    tile = lambda v: jnp.broadcast_to(v, (ts, w))

    def cmul_add(a_r, a_i, s_r, s_i, x_r, x_i):
        return a_r * s_r - a_i * s_i + x_r, a_r * s_i + a_i * s_r + x_i

    jobs = [(d, kb) for d in range(2) for kb in range(S5_KB)]
    refs = ((hf_ref, yf_ref), (hb_ref, yb_ref))
    lanes = 128
    nl = D_MODEL // lanes
    hn = []
    for d in range(2):
        for c in range(nl):
            hsc_ref[c] = refs[d][0][:, c * lanes:(c + 1) * lanes]
        hn.append([_rms(jnp.concatenate([hsc_ref[c, pl.ds(t, rt, stride=ts), :] for c in range(nl)], axis=1),
                        nw).astype(BF16) for t in range(ts)])

    def drive(d, kb):
        lhs = jnp.concatenate([hn[d][t][:, kb * cin:(kb + 1) * cin] for t in range(ts)], axis=0)
        return _dot(lhs, win_ref[d, kb])

    x_next = drive(*jobs[0])
    for idx, (d, kb) in enumerate(jobs):
        x = x_next
        if idx + 1 < len(jobs):
            x_next = drive(*jobs[idx + 1])
        y_ref = refs[d][1]
        xr = [x[t * rt:(t + 1) * rt, :w] for t in range(ts)]
        xi = [x[t * rt:(t + 1) * rt, w:] for t in range(ts)]
        p_r = pw_ref[d, 0, 0:ts, kb * w:(kb + 1) * w]
        p_i = pw_ref[d, 1, 0:ts, kb * w:(kb + 1) * w]
        one = ts if d == 0 else 2 * ts - 1
        a_r = pw_ref[d, 0, one:one + 1, kb * w:(kb + 1) * w]
        a_i = pw_ref[d, 1, one:one + 1, kb * w:(kb + 1) * w]
        order = list(range(ts)) if d == 0 else list(range(ts - 1, -1, -1))

        e_r, e_i = xr[order[0]], xi[order[0]]
        for t in order[1:]:
            e_r, e_i = cmul_add(a_r, a_i, e_r, e_i, xr[t], xi[t])

        consts = []
        for k in (1, 2, 4):
            src = k - 1 if d == 0 else ts - k
            valid = rowi >= k if d == 0 else rowi <= ts - 1 - k
            consts.append((jnp.where(valid, tile(p_r[src:src + 1]), 0.0),
                           jnp.where(valid, tile(p_i[src:src + 1]), 0.0)))
        last, edge, back = (ts - 1, 0, 1) if d == 0 else (0, ts - 1, ts - 1)
        c_r, c_i = car_ref[d, kb, 0], car_ref[d, kb, 1]
        in_r, in_i = [None] * nt, [None] * nt
        for jj in range(nt):
            j = jj if d == 0 else nt - 1 - jj
            s_r, s_i = e_r[j * ts:(j + 1) * ts], e_i[j * ts:(j + 1) * ts]
            for k, (k_r, k_i) in zip((1, 2, 4), consts):
                sh = k if d == 0 else ts - k
                t_r, t_i = pltpu.roll(s_r, sh, 0), pltpu.roll(s_i, sh, 0)
                s_r, s_i = s_r + k_r * t_r - k_i * t_i, s_i + k_r * t_i + k_i * t_r
            s_r, s_i = s_r + p_r * c_r - p_i * c_i, s_i + p_r * c_i + p_i * c_r
            in_r[j] = jnp.where(rowi == edge, c_r, pltpu.roll(s_r, back, 0))
            in_i[j] = jnp.where(rowi == edge, c_i, pltpu.roll(s_i, back, 0))
            c_r, c_i = tile(s_r[last:last + 1]), tile(s_i[last:last + 1])
        car_ref[d, kb, 0] = c_r
        car_ref[d, kb, 1] = c_i

        s_r, s_i = jnp.concatenate(in_r, axis=0), jnp.concatenate(in_i, axis=0)
        states = [None] * ts
        for t in order:
            s_r, s_i = cmul_add(a_r, a_i, s_r, s_i, xr[t], xi[t])
            states[t] = jnp.concatenate([s_r, s_i], axis=1).astype(BF16)
        yk = _dot(jnp.concatenate(states, axis=0), wout_ref[d, kb])
        for t in range(ts):
            for c in range(cin // lanes):
                ysc_ref[kb * (cin // lanes) + c, pl.ds(t, rt, stride=ts), :] = (
                    yk[t * rt:(t + 1) * rt, c * lanes:(c + 1) * lanes])
        if kb == S5_KB - 1:
            for c in range(nl):
                y_ref[:, c * lanes:(c + 1) * lanes] = ysc_ref[c].astype(y_ref.dtype)


def _s5_core(h, nw, w_in, w_out, pw):
    L = h.shape[0]
    rb = min(S5_ROWS, L)
    nb = L // rb
    w = S5_LANES // S5_KB
    ts = S5_TILE
    rt = rb // ts
    fwd = lambda b: (b, 0)
    bwd = lambda b: (nb - 1 - b, 0)
    return pl.pallas_call(
        functools.partial(_s5_core_kernel, rt=rt),
        grid=(nb,),
        in_specs=[
            pl.BlockSpec((rb, D_MODEL), fwd),
            pl.BlockSpec((rb, D_MODEL), bwd),
            _resident((1, D_MODEL)),
            _resident(w_in.shape),
            _resident(w_out.shape),
            _resident(pw.shape),
        ],
        out_specs=[pl.BlockSpec((rb, D_MODEL), fwd), pl.BlockSpec((rb, D_MODEL), bwd)],
        out_shape=[jax.ShapeDtypeStruct((L, D_MODEL), BF16)] * 2,
        scratch_shapes=[pltpu.VMEM((2, S5_KB, 2, ts, w), F32),
                        pltpu.VMEM((D_MODEL // 128, rb, 128), F32),
                        pltpu.VMEM((D_MODEL // 128, rb, 128), F32)],
        compiler_params=_params("arbitrary"),
        name="s5_core",
    )(h, h, nw, w_in, w_out, pw)


def _s5_out_kernel(yf_ref, yb_ref, h_ref, nw_ref, dsk_ref, wg_ref, bg_ref, pnw_ref, *ffn_refs):
    h = h_ref[...]
    hn = _rms(h, nw_ref[...])
    y = yf_ref[...].astype(F32) + yb_ref[...].astype(F32) + dsk_ref[...] * hn
    y = jax.nn.gelu(y)
    vg = _dot(y.astype(BF16), wg_ref[...]) + bg_ref[...]
    m = vg[:, :D_MODEL] * jax.nn.sigmoid(vg[:, D_MODEL:])
    _ffn_block(h + _rms(m, pnw_ref[...]), *ffn_refs)


def _s5_layer(h, nw_pre, nw_post, lam_re, lam_im, log_step, b_re, b_im, c_re, c_im, d_skip, w_glu, b_glu,
              ffn):
    nw = nw_pre.reshape(1, -1)
    pw = _s5_prep(lam_re, lam_im, log_step)
    w_in, w_out = _s5_weights(lam_re, lam_im, log_step, b_re, b_im, c_re, c_im)
    yf, yb = _s5_core(h, nw, w_in, w_out, pw)
    consts = (nw, d_skip.reshape(1, -1), w_glu.astype(BF16), b_glu.reshape(1, -1), nw_post.reshape(1, -1))
    return _mixer_out_call(_s5_out_kernel, "s5_out_ffn", h, (yf, yb), consts, ffn)


def kernel(x, norm_w, ssd_w_in, ssd_conv_w, ssd_conv_b, ssd_dt_bias, ssd_a_log, ssd_d, ssd_norm_w, ssd_w_out, gdn_w_in, gdn_conv_w, gdn_conv_b, gdn_dt_bias, gdn_a_log, gdn_norm_w, gdn_w_out, s5_lam_re, s5_lam_im, s5_log_step, s5_b_re, s5_b_im, s5_c_re, s5_c_im, s5_d, s5_w_glu, s5_b_glu, ffn_w_gate_up, ffn_w_down):
    bt, L, d = x.shape
    assert bt == 1 and d == D_MODEL
    h = x.reshape(L, d)
    for i in range(norm_w.shape[0]):
        kind, j = i % 3, i // 3
        ffn = (norm_w[i, 2], norm_w[i, 3], ffn_w_gate_up[i], ffn_w_down[i])
        if kind == 0:
            h = _ssd_layer(h, norm_w[i, 0], norm_w[i, 1], ssd_w_in[j], ssd_conv_w[j], ssd_conv_b[j],
                           ssd_dt_bias[j], ssd_a_log[j], ssd_d[j], ssd_norm_w[j], ssd_w_out[j], ffn)
        elif kind == 1:
            h = _gdn_layer(h, norm_w[i, 0], norm_w[i, 1], gdn_w_in[j], gdn_conv_w[j], gdn_conv_b[j],
                           gdn_dt_bias[j], gdn_a_log[j], gdn_norm_w[j], gdn_w_out[j], ffn)
        else:
            h = _s5_layer(h, norm_w[i, 0], norm_w[i, 1], s5_lam_re[j], s5_lam_im[j], s5_log_step[j],
                          s5_b_re[j], s5_b_im[j], s5_c_re[j], s5_c_im[j], s5_d[j], s5_w_glu[j], s5_b_glu[j], ffn)
    return h.reshape(bt, L, d)
```

```python
import functools

import jax
import jax.numpy as jnp
from jax import lax
from jax.experimental import pallas as pl
from jax.experimental.pallas import tpu as pltpu

F32 = jnp.float32
BF16 = jnp.bfloat16
HIGHEST = lax.Precision.HIGHEST

D_MODEL = 1024
NORM_EPS = 1e-6
CONV_TAPS = 5
CONV_CH = 4096
Z_CH = 2048
SMALL_CH = 128
HALO = 8

SSD_HEADS = 32
SSD_HEAD_DIM = 64
SSD_GROUPS = 8
SSD_STATE = 128
SSD_CHUNK = 128
SSD_D_INNER = 2048

GDN_HEAD_DIM = 128
GDN_QK_HEADS = 8
GDN_V_HEADS = 16
GDN_KEY_DIM = GDN_QK_HEADS * GDN_HEAD_DIM
GDN_CHUNK = 64
GDN_VALUE_DIM = 2048

S5_GROUPS = 64
S5_GROUP = 16
S5_STATE = 64
S5_LANES = S5_GROUPS * S5_STATE
S5_KB = 4
S5_TILE = 8

D_FF = 2816
FFN_CHUNK = 256

ROW_TILE = 512
GDN_ROWS = 512
S5_ROWS = 512
VMEM_LIMIT_BYTES = 56 * 1024 * 1024

NT_DIMS = (((1,), (1,)), ((), ()))
TN_DIMS = (((0,), (0,)), ((), ()))


def _params(*semantics):
    return pltpu.CompilerParams(dimension_semantics=semantics, vmem_limit_bytes=VMEM_LIMIT_BYTES)


def _resident(shape):
    nd = len(shape)
    return pl.BlockSpec(shape, lambda *_: (0,) * nd, pipeline_mode=pl.Buffered(1))


def _rms(x, w):
    return x * lax.rsqrt(jnp.mean(x * x, axis=-1, keepdims=True) + NORM_EPS) * w


def _silu(x):
    return x * jax.nn.sigmoid(x)


def _dot(a, b):
    return jnp.dot(a, b, preferred_element_type=F32)


def _inproj_kernel(h_ref, hp_ref, hx_ref, nw_ref, wc_ref, wz_ref, ws_ref, cw_ref, cb_ref, sb_ref, sa_ref,
                   act_ref, z_ref, s_ref, hs_ref, ys_ref, *, tm, nblk, cchunk, gdn):
    i = pl.program_id(0)
    nw = nw_ref[...]
    curb = _rms(h_ref[...], nw).astype(BF16)
    z_ref[...] = _dot(curb, wz_ref[...]).astype(z_ref.dtype)
    raw = _dot(curb, ws_ref[...]) + sb_ref[...]
    if gdn:
        is_decay = lax.broadcasted_iota(jnp.int32, raw.shape, 1) < 2 * GDN_V_HEADS
        s_ref[...] = jnp.where(is_decay, -jnp.exp(sa_ref[...]) * jax.nn.softplus(raw), jax.nn.sigmoid(raw))
    else:
        s_ref[...] = jax.nn.softplus(raw)
    pad = CONV_TAPS // 2
    hd = GDN_HEAD_DIM
    lanes = 128
    rows = tm + 2 * HALO
    seg = rows // 8
    for s in range(D_MODEL // lanes):
        ls = slice(s * lanes, (s + 1) * lanes)
        hs_ref[s, 0:HALO, :] = hp_ref[:, ls]
        hs_ref[s, HALO:HALO + tm, :] = h_ref[:, ls]
        hs_ref[s, HALO + tm:rows, :] = hx_ref[:, ls]
    ext = jnp.concatenate(
        [jnp.concatenate([hs_ref[s, pl.ds(j, 8, stride=seg), :] for s in range(D_MODEL // lanes)], axis=1)
         for j in range(seg)], axis=0)
    r = lax.broadcasted_iota(jnp.int32, (rows, 1), 0)
    orig = (r % 8) * seg + r // 8
    inside = jnp.logical_and(jnp.logical_or(orig >= HALO, i > 0),
                             jnp.logical_or(orig < HALO + tm, i < nblk - 1))
    ext = jnp.where(inside, _rms(ext, nw), 0.0).astype(BF16)
    for c in range(CONV_CH // cchunk):
        sl = slice(c * cchunk, (c + 1) * cchunk)
        p = _dot(ext, wc_ref[:, sl])
        head = [pltpu.roll(p[(seg - pad + m) * 8:(seg - pad + m + 1) * 8], 1, 0) for m in range(pad)]
        tail = [pltpu.roll(p[m * 8:(m + 1) * 8], 7, 0) for m in range(pad)]
        p_ext = jnp.concatenate(head + [p] + tail, axis=0)
        acc = cb_ref[:, sl]
        for k in range(CONV_TAPS):
            acc = acc + cw_ref[k:k + 1, sl] * p_ext[8 * k:8 * k + rows]
        y = _silu(acc)
        if gdn and c * cchunk < 2 * GDN_KEY_DIM:
            parts = []
            for s in range(cchunk // hd):
                blk = y[:, s * hd:(s + 1) * hd]
                inv = lax.rsqrt(jnp.sum(blk * blk, axis=-1, keepdims=True) + NORM_EPS)
                if c * cchunk + s * hd < GDN_KEY_DIM:
                    inv = inv * (hd ** -0.5)
                parts.append(blk * inv)
            y = jnp.concatenate(parts, axis=1)
        for s in range(cchunk // lanes):
            for j in range(seg):
                ys_ref[s, pl.ds(j, 8, stride=seg), :] = y[8 * j:8 * j + 8, s * lanes:(s + 1) * lanes]
        for s in range(cchunk // lanes):
            act_ref[:, c * cchunk + s * lanes:c * cchunk + (s + 1) * lanes] = (
                ys_ref[s, HALO:HALO + tm, :].astype(act_ref.dtype))


def _inproj(h, nw, wc, wz, ws, conv_w, conv_b, s_bias, s_alog, gdn):
    L = h.shape[0]
    tm = min(ROW_TILE, L)
    nblk = L // tm
    per = tm // HALO
    last = L // HALO - 1
    kern = functools.partial(_inproj_kernel, tm=tm, nblk=nblk, cchunk=512, gdn=gdn)
    return pl.pallas_call(
        kern,
        grid=(nblk,),
        in_specs=[
            pl.BlockSpec((tm, D_MODEL), lambda i: (i, 0)),
            pl.BlockSpec((HALO, D_MODEL), lambda i: (jnp.maximum(i * per - 1, 0), 0)),
            pl.BlockSpec((HALO, D_MODEL), lambda i: (jnp.minimum((i + 1) * per, last), 0)),
            _resident((1, D_MODEL)),
            _resident((D_MODEL, CONV_CH)),
            _resident((D_MODEL, Z_CH)),
            _resident((D_MODEL, SMALL_CH)),
            _resident((8, CONV_CH)),
            _resident((1, CONV_CH)),
            _resident((1, SMALL_CH)),
            _resident((1, SMALL_CH)),
        ],
        out_specs=[
            pl.BlockSpec((tm, CONV_CH), lambda i: (i, 0)),
            pl.BlockSpec((tm, Z_CH), lambda i: (i, 0)),
            pl.BlockSpec((tm, SMALL_CH), lambda i: (i, 0)),
        ],
        out_shape=[
            jax.ShapeDtypeStruct((L, CONV_CH), BF16),
            jax.ShapeDtypeStruct((L, Z_CH), BF16),
            jax.ShapeDtypeStruct((L, SMALL_CH), F32),
        ],
        scratch_shapes=[pltpu.VMEM((D_MODEL // 128, tm + 2 * HALO, 128), F32),
                        pltpu.VMEM((512 // 128, tm + 2 * HALO, 128), F32)],
        compiler_params=_params("arbitrary"),
        name="inproj_conv",
    )(h, h, h, nw, wc, wz, ws, conv_w, conv_b, s_bias, s_alog)


def _pad_small(t):
    t = t.reshape(1, -1)
    return jnp.pad(t, ((0, 0), (0, SMALL_CH - t.shape[1])))


def _split_in_weights(w_in, conv_w, conv_b, conv_first):
    if conv_first:
        wc, wz = w_in[:, :CONV_CH], w_in[:, CONV_CH:CONV_CH + Z_CH]
    else:
        wz, wc = w_in[:, :Z_CH], w_in[:, Z_CH:Z_CH + CONV_CH]
    ws = w_in[:, CONV_CH + Z_CH:]
    ws = jnp.pad(ws, ((0, 0), (0, SMALL_CH - ws.shape[1])))
    cw = jnp.pad(conv_w, ((0, 8 - CONV_TAPS), (0, 0)))
    return wc.astype(BF16), wz.astype(BF16), ws.astype(BF16), cw, conv_b.reshape(1, CONV_CH)


def _ssd_core_kernel(xf_ref, xb_ref, dtf_ref, dtb_ref, alog_ref, dskip_ref, exb_ref,
                     yf_ref, yb_ref, sf_ref, sb_ref, *, q):
    @pl.when(pl.program_id(0) == 0)
    def _():
        sf_ref[...] = jnp.zeros_like(sf_ref)
        sb_ref[...] = jnp.zeros_like(sb_ref)

    row = lax.broadcasted_iota(jnp.int32, (q, q), 0)
    col = lax.broadcasted_iota(jnp.int32, (q, q), 1)
    lower = row >= col
    upper = row <= col
    lower_f = lower.astype(F32)
    upper_f = upper.astype(F32)
    a = -jnp.exp(alog_ref[...])
    p = SSD_HEAD_DIM
    per_group = SSD_HEADS // SSD_GROUPS
    gw = per_group * p
    pw = 2 * p
    b_off = SSD_D_INNER
    c_off = SSD_D_INNER + SSD_GROUPS * SSD_STATE
    left = lax.broadcasted_iota(jnp.int32, (q, pw), 1) < p

    dirs = ((xf_ref, dtf_ref, yf_ref, sf_ref), (xb_ref, dtb_ref, yb_ref, sb_ref))
    pre = []
    for d, (x_ref, dt_ref, y_ref, s_ref) in enumerate(dirs):
        dt = dt_ref[...]
        da = dt * a
        tri_c, tri_r = (lower_f, upper_f) if d == 0 else (upper_f, lower_f)
        cs = jnp.dot(tri_c, da, precision=HIGHEST, preferred_element_type=F32)
        cs_t = jnp.dot(da.T, tri_r, precision=HIGHEST, preferred_element_type=F32)
        tot = cs[q - 1:q, :] if d == 0 else cs[0:1, :]
        exb = exb_ref[d]
        dt_x = _dot(dt.astype(BF16), exb)
        e_x = _dot(jnp.exp(cs).astype(BF16), exb)
        w_x = _dot((dt * jnp.exp(tot - cs)).astype(BF16), exb)
        t_hi = tot.astype(BF16)
        t_mid = (tot - t_hi.astype(F32)).astype(BF16)
        t_lo = (tot - t_hi.astype(F32) - t_mid.astype(F32)).astype(BF16)
        pieces = jnp.concatenate([jnp.broadcast_to(t, (8, SMALL_CH)) for t in (t_hi, t_mid, t_lo)], axis=0)
        tot_3 = _dot(pieces, exb)
        tot_x = tot_3[0:1] + tot_3[8:9] + tot_3[16:17]
        pre.append((cs, cs_t, dt_x, e_x, w_x, jnp.exp(tot_x)))

    def front(d, g):
        x_ref, s_ref = dirs[d][0], dirs[d][3]
        bmat = x_ref[:, b_off + g * SSD_STATE:b_off + (g + 1) * SSD_STATE]
        cmat = x_ref[:, c_off + g * SSD_STATE:c_off + (g + 1) * SSD_STATE]
        cb = lax.dot_general(cmat, bmat, NT_DIMS, preferred_element_type=F32)
        yoff = _dot(cmat, s_ref[g].astype(BF16))
        return bmat, cb, yoff

    jobs = [(d, g) for d in range(2) for g in range(SSD_GROUPS)]
    ahead = front(*jobs[0])
    for idx, (d, g) in enumerate(jobs):
        bmat, cb, yoff = ahead
        if idx + 1 < len(jobs):
            ahead = front(*jobs[idx + 1])
        xg = dirs[d][0][:, g * gw:(g + 1) * gw].astype(F32)
        cs, cs_t, dt_x, e_x, w_x, cdec_x = pre[d]
        mask = lower if d == 0 else upper
        y_ref = dirs[d][2]
        for pr in range(per_group // 2):
            h0 = per_group * g + 2 * pr
            lms = []
            for hh in (h0, h0 + 1):
                ch = hh + SSD_HEADS * d
                lms.append(cb * jnp.exp(jnp.where(mask, cs[:, ch:ch + 1] - cs_t[ch:ch + 1, :], -jnp.inf)))
            m_pair = jnp.concatenate(lms, axis=1).astype(BF16)
            sl = slice(h0 * p, h0 * p + pw)
            lo = slice(2 * pr * p, 2 * pr * p + pw)
            xp = xg[:, lo]
            xdt = xp * dt_x[:, sl]
            rhs = jnp.concatenate([jnp.where(left, xdt, 0.0), jnp.where(left, 0.0, xdt)], axis=0).astype(BF16)
            y = _dot(m_pair, rhs) + yoff[:, lo] * e_x[:, sl]
            if d == 0:
                y = y + xp * dskip_ref[:, sl]
            y_ref[:, sl] = y.astype(y_ref.dtype)
        s_ref = dirs[d][3]
        gs = slice(g * gw, (g + 1) * gw)
        new = lax.dot_general(bmat, (xg * w_x[:, gs]).astype(BF16), TN_DIMS, preferred_element_type=F32)
        s_ref[g] = s_ref[g] * cdec_x[:, gs] + new


def _ssd_core(act, small, a_log, d_skip):
    L = act.shape[0]
    q = SSD_CHUNK
    nc = L // q
    alog = _pad_small(a_log)
    dskip = jnp.repeat(d_skip, SSD_HEAD_DIM).reshape(1, SSD_D_INNER)
    head_of_channel = jnp.arange(SSD_D_INNER) // SSD_HEAD_DIM
    col_id = jnp.arange(SMALL_CH)
    expand = jnp.stack([col_id[:, None] == head_of_channel[None, :] + SSD_HEADS * d for d in range(2)])
    gw = SSD_D_INNER // SSD_GROUPS
    fwd = lambda c: (c, 0)
    bwd = lambda c: (nc - 1 - c, 0)
    return pl.pallas_call(
        functools.partial(_ssd_core_kernel, q=q),
        grid=(nc,),
        in_specs=[
            pl.BlockSpec((q, CONV_CH), fwd),
            pl.BlockSpec((q, CONV_CH), bwd),
            pl.BlockSpec((q, SMALL_CH), fwd),
            pl.BlockSpec((q, SMALL_CH), bwd),
            _resident((1, SMALL_CH)),
            _resident((1, SSD_D_INNER)),
            _resident((2, SMALL_CH, SSD_D_INNER)),
        ],
        out_specs=[pl.BlockSpec((q, SSD_D_INNER), fwd), pl.BlockSpec((q, SSD_D_INNER), bwd)],
        out_shape=[jax.ShapeDtypeStruct((L, SSD_D_INNER), BF16)] * 2,
        scratch_shapes=[pltpu.VMEM((SSD_GROUPS, SSD_STATE, gw), F32)] * 2,
        compiler_params=_params("arbitrary"),
        name="ssd_core",
    )(act, act, small, small, alog, dskip, expand.astype(BF16))


def _ssd_out_kernel(yf_ref, yb_ref, z_ref, h_ref, gnw_ref, wo_ref, pnw_ref, *ffn_refs):
    y = yf_ref[...].astype(F32) + yb_ref[...].astype(F32)
    yz = y * _silu(z_ref[...].astype(F32))
    gw = SSD_D_INNER // SSD_GROUPS
    parts = []
    for g in range(SSD_GROUPS):
        blk = yz[:, g * gw:(g + 1) * gw]
        parts.append(blk * lax.rsqrt(jnp.mean(blk * blk, axis=-1, keepdims=True) + NORM_EPS))
    yn = (jnp.concatenate(parts, axis=1) * gnw_ref[...]).astype(BF16)
    m = _dot(yn, wo_ref[...])
    _ffn_block(h_ref[...] + _rms(m, pnw_ref[...]), *ffn_refs)


def _ffn_block(h, nw_ref, wgu_ref, wd_ref, pnw_ref, o_ref, act_ref):
    hn = _rms(h, nw_ref[...]).astype(BF16)
    for c in range(D_FF // FFN_CHUNK):
        gate = _dot(hn, wgu_ref[:, c * FFN_CHUNK:(c + 1) * FFN_CHUNK])
        up = _dot(hn, wgu_ref[:, D_FF + c * FFN_CHUNK:D_FF + (c + 1) * FFN_CHUNK])
        act_ref[:, c * FFN_CHUNK:(c + 1) * FFN_CHUNK] = (_silu(gate) * up).astype(BF16)
    f = _dot(act_ref[...], wd_ref[...])
    o_ref[...] = h + _rms(f, pnw_ref[...])


def _mixer_out_call(kern, name, h, streams, consts, ffn):
    L = h.shape[0]
    tm = min(ROW_TILE, L)
    nw_pre, nw_post, w_gate_up, w_down = ffn
    consts = tuple(consts) + (nw_pre.reshape(1, -1), w_gate_up.astype(BF16), w_down.astype(BF16),
                              nw_post.reshape(1, -1))
    tiled = lambda a: pl.BlockSpec((tm, a.shape[1]), lambda i: (i, 0))
    return pl.pallas_call(
        kern,
        grid=(L // tm,),
        in_specs=[tiled(a) for a in streams] + [tiled(h)] + [_resident(c.shape) for c in consts],
        out_specs=tiled(h),
        out_shape=jax.ShapeDtypeStruct(h.shape, F32),
        scratch_shapes=[pltpu.VMEM((tm, D_FF), BF16)],
        compiler_params=_params("arbitrary"),
        name=name,
    )(*streams, h, *consts)


def _ssd_layer(h, nw_pre, nw_post, w_in, conv_w, conv_b, dt_bias, a_log, d_skip, norm_w, w_out, ffn):
    wc, wz, ws, cw, cb = _split_in_weights(w_in, conv_w, conv_b, conv_first=False)
    act, z, small = _inproj(h, nw_pre.reshape(1, -1), wc, wz, ws, cw, cb, _pad_small(dt_bias), _pad_small(a_log),
                            gdn=False)
    yf, yb = _ssd_core(act, small, a_log, d_skip)
    consts = (norm_w.reshape(1, -1), w_out.astype(BF16), nw_post.reshape(1, -1))
    return _mixer_out_call(_ssd_out_kernel, "ssd_out_ffn", h, (yf, yb, z), consts, ffn)


def _gdn_core_kernel(xf_ref, xb_ref, gcf_ref, gcb_ref, grf_ref, grb_ref, ex_ref, exh_ref,
                     of_ref, ob_ref, s_ref, u_ref, wq_ref, kd_ref, qkm_ref, et_ref, *, rb, ck):
    @pl.when(pl.program_id(0) == 0)
    def _():
        s_ref[...] = jnp.zeros_like(s_ref)

    n = rb // ck
    hd = GDN_HEAD_DIM
    nq, nv = GDN_QK_HEADS, GDN_V_HEADS
    k_off, v_off = nq * hd, 2 * nq * hd
    row = lax.broadcasted_iota(jnp.int32, (ck, ck), 0)
    col = lax.broadcasted_iota(jnp.int32, (ck, ck), 1)
    lower, upper = row >= col, row <= col
    lower_s, upper_s = row > col, row < col
    lower_f, upper_f = lower.astype(F32), upper.astype(F32)
    row2 = lax.broadcasted_iota(jnp.int32, (ck, 2 * ck), 0)
    lane2 = lax.broadcasted_iota(jnp.int32, (ck, 2 * ck), 1)
    left = lane2 < ck
    col2 = jnp.where(left, lane2, lane2 - ck)
    pair = lambda t: jnp.concatenate([t, t], axis=1)
    eye2 = (row2 == col2).astype(F32)

    def block_diag(p2):
        return jnp.concatenate([jnp.where(left, p2, 0.0), jnp.where(left, 0.0, p2)], axis=0).astype(BF16)

    def solve_chunk(ci, carry):
        pairs = []
        for d in range(2):
            x_ref, gc_ref, gr_ref = (xf_ref, gcf_ref, grf_ref) if d == 0 else (xb_ref, gcb_ref, grb_ref)
            cc = ci if d == 0 else n - 1 - ci
            rows = pl.ds(pl.multiple_of(cc * ck, ck), ck)
            tri_c, tri_r = (lower_f, upper_f) if d == 0 else (upper_f, lower_f)
            mask2, strict2 = (row2 >= col2, row2 > col2) if d == 0 else (row2 <= col2, row2 < col2)
            gb_c = gc_ref[rows, :]
            gb_r = gr_ref[cc]
            cs_c = jnp.dot(tri_c, gb_c, precision=HIGHEST, preferred_element_type=F32)
            cs_r = jnp.dot(gb_r, tri_r, precision=HIGHEST, preferred_element_type=F32)
            tot = cs_c[ck - 1:ck, :] if d == 0 else cs_c[0:1, :]
            et_ref[d, cc] = jnp.exp(tot)
            beta_al = pltpu.roll(gb_c, SMALL_CH - 2 * nv, axis=1)
            ex, ex_half = ex_ref[d], exh_ref[d]
            beta_b = beta_al.astype(BF16)
            beta_h = _dot(beta_b, ex_half)
            stacked = jnp.concatenate([beta_b, jnp.exp(cs_c).astype(BF16), jnp.exp(tot - cs_c).astype(BF16)], axis=0)
            stacked_x = _dot(stacked, ex)
            beta_x = stacked_x[:ck]
            e_x = stacked_x[ck:2 * ck]
            kf_x = stacked_x[2 * ck:]
            for hq in range(nq):
                qn = x_ref[rows, hq * hd:(hq + 1) * hd]
                kn = x_ref[rows, k_off + hq * hd:k_off + (hq + 1) * hd]
                gram = lax.dot_general(jnp.concatenate([kn, qn], axis=0), kn, NT_DIMS,
                                       preferred_element_type=F32)
                kk2, qk2 = pair(gram[:ck]), pair(gram[ck:])
                c0 = nv * d + 2 * hq
                gcs_c2 = jnp.where(left, cs_c[:, c0:c0 + 1], cs_c[:, c0 + 1:c0 + 2])
                gcs_r2 = jnp.concatenate([cs_r[c0:c0 + 1, :], cs_r[c0 + 1:c0 + 2, :]], axis=1)
                beta2 = beta_h[:, hq * 2 * ck:(hq + 1) * 2 * ck]
                decay2 = jnp.exp(jnp.where(mask2, gcs_c2 - gcs_r2, -jnp.inf))
                a2 = jnp.where(strict2, beta2 * kk2 * decay2, 0.0)
                qkm_ref[d, hq, cc] = (qk2 * decay2).astype(BF16)
                qf, kf = qn.astype(F32), kn.astype(F32)
                rhs = []
                for j in range(2):
                    vh, cg = 2 * hq + j, c0 + j
                    hs = slice(vh * hd, (vh + 1) * hd)
                    beta_c, e_g = beta_x[:, hs], e_x[:, hs]
                    vx = x_ref[rows, v_off + vh * hd:v_off + (vh + 1) * hd].astype(F32)
                    rhs.append(jnp.concatenate([vx * beta_c, kf * (beta_c * e_g)], axis=1).astype(BF16))
                    wq_ref[d, vh, cc, ck:2 * ck, :] = (qf * e_g).astype(BF16)
                    kd_ref[d, vh, cc] = (kf * kf_x[:, hs]).astype(BF16)
                zero = jnp.zeros_like(rhs[0])
                rhs2 = jnp.concatenate([jnp.concatenate([rhs[0], zero], axis=1),
                                        jnp.concatenate([zero, rhs[1]], axis=1)], axis=0)
                pairs.append((d, hq, cc, a2, rhs2))
        pws = [_dot(a2.astype(BF16), block_diag(a2)) for (_, _, _, a2, _) in pairs]
        tis = [eye2 - a2 for (_, _, _, a2, _) in pairs]
        for level in range(5):
            if level < 4:
                res = [_dot(jnp.concatenate([t, p], axis=0).astype(BF16), block_diag(p)) for t, p in zip(tis, pws)]
                tis = [t + r[:ck] for t, r in zip(tis, res)]
                pws = [r[ck:] for r in res]
            else:
                tis = [t + _dot(t.astype(BF16), block_diag(p)) for t, p in zip(tis, pws)]
        for (d, hq, cc, _, rhs2), t in zip(pairs, tis):
            sol = _dot(t.astype(BF16), rhs2)
            for j in range(2):
                vh = 2 * hq + j
                u_ref[d, vh, cc] = sol[:, 2 * j * hd:(2 * j + 1) * hd]
                wq_ref[d, vh, cc, 0:ck, :] = sol[:, (2 * j + 1) * hd:(2 * j + 2) * hd].astype(BF16)
        return carry

    def scan_chunk(ci, carry):
        units = [(d, vh, ci if d == 0 else n - 1 - ci) for d in range(2) for vh in range(nv)]
        boths = [_dot(wq_ref[d, vh, cc], s_ref[d, vh].astype(BF16)) for d, vh, cc in units]
        v_news = [(u_ref[d, vh, cc] - both[:ck]).astype(BF16) for (d, vh, cc), both in zip(units, boths)]
        for i in range(0, len(units), 2):
            d, vh, cc = units[i]
            o_ref = of_ref if d == 0 else ob_ref
            rows = pl.ds(pl.multiple_of(cc * ck, ck), ck)
            zero = jnp.zeros_like(v_news[i])
            v2 = jnp.concatenate([jnp.concatenate([v_news[i], zero], axis=1),
                                  jnp.concatenate([zero, v_news[i + 1]], axis=1)], axis=0)
            o2 = jnp.concatenate([boths[i][ck:], boths[i + 1][ck:]], axis=1) + _dot(qkm_ref[d, vh // 2, cc], v2)
            o_ref[rows, vh * hd:(vh + 2) * hd] = o2.astype(o_ref.dtype)
        for (d, vh, cc), v_newb in zip(units, v_news):
            scale = et_ref[d, cc][:, nv * d + vh:nv * d + vh + 1]
            s_ref[d, vh] = s_ref[d, vh] * scale + lax.dot_general(
                kd_ref[d, vh, cc], v_newb, TN_DIMS, preferred_element_type=F32)
        return carry

    lax.fori_loop(0, n, solve_chunk, 0)
    lax.fori_loop(0, n, scan_chunk, 0)


def _gdn_core(act, small):
    L = act.shape[0]
    ck = GDN_CHUNK
    rb = min(GDN_ROWS, L)
    nb = L // rb
    nv = GDN_V_HEADS
    hd = GDN_HEAD_DIM
    n = rb // ck
    g_row = small[:, :4 * nv].reshape(L // ck, ck, 4 * nv).transpose(0, 2, 1)
    col_id = jnp.arange(SMALL_CH)[:, None]
    expand = lambda w: jnp.stack([col_id == jnp.arange(nv * w)[None, :] // w + nv * d
                                  for d in range(2)]).astype(BF16)
    ex, ex_half = expand(hd), expand(ck)
    fwd = lambda b: (b, 0)
    bwd = lambda b: (nb - 1 - b, 0)
    return pl.pallas_call(
        functools.partial(_gdn_core_kernel, rb=rb, ck=ck),
        grid=(nb,),
        in_specs=[
            pl.BlockSpec((rb, CONV_CH), fwd),
            pl.BlockSpec((rb, CONV_CH), bwd),
            pl.BlockSpec((rb, SMALL_CH), fwd),
            pl.BlockSpec((rb, SMALL_CH), bwd),
            pl.BlockSpec((n, 4 * nv, ck), lambda b: (b, 0, 0)),
            pl.BlockSpec((n, 4 * nv, ck), lambda b: (nb - 1 - b, 0, 0)),
            _resident(ex.shape),
            _resident(ex_half.shape),
        ],
        out_specs=[pl.BlockSpec((rb, GDN_VALUE_DIM), fwd), pl.BlockSpec((rb, GDN_VALUE_DIM), bwd)],
        out_shape=[jax.ShapeDtypeStruct((L, GDN_VALUE_DIM), BF16)] * 2,
        scratch_shapes=[
            pltpu.VMEM((2, nv, hd, hd), F32),
            pltpu.VMEM((2, nv, n, ck, hd), F32),
            pltpu.VMEM((2, nv, n, 2 * ck, hd), BF16),
            pltpu.VMEM((2, nv, n, ck, hd), BF16),
            pltpu.VMEM((2, nv // 2, n, ck, 2 * ck), BF16),
            pltpu.VMEM((2, n, 1, SMALL_CH), F32),
        ],
        compiler_params=_params("arbitrary"),
        name="gdn_core",
    )(act, act, small, small, g_row, g_row, ex, ex_half)


def _gdn_out_kernel(of_ref, ob_ref, z_ref, h_ref, hnw_ref, wo_ref, pnw_ref, *ffn_refs):
    o = of_ref[...].astype(F32) + ob_ref[...].astype(F32)
    z = z_ref[...].astype(F32)
    hd = GDN_HEAD_DIM
    hnw = hnw_ref[...]
    parts = []
    for v in range(GDN_VALUE_DIM // hd):
        blk = o[:, v * hd:(v + 1) * hd]
        parts.append(_rms(blk, hnw) * _silu(z[:, v * hd:(v + 1) * hd]))
    m = _dot(jnp.concatenate(parts, axis=1).astype(BF16), wo_ref[...])
    _ffn_block(h_ref[...] + _rms(m, pnw_ref[...]), *ffn_refs)


def _gdn_layer(h, nw_pre, nw_post, w_in, conv_w, conv_b, dt_bias, a_log, norm_w, w_out, ffn):
    wc, wz, ws, cw, cb = _split_in_weights(w_in, conv_w, conv_b, conv_first=True)
    act, z, small = _inproj(h, nw_pre.reshape(1, -1), wc, wz, ws, cw, cb, _pad_small(dt_bias), _pad_small(a_log),
                            gdn=True)
    of, ob = _gdn_core(act, small)
    consts = (norm_w.reshape(1, -1), w_out.astype(BF16), nw_post.reshape(1, -1))
    return _mixer_out_call(_gdn_out_kernel, "gdn_out_ffn", h, (of, ob, z), consts, ffn)


def _s5_weights_kernel(lre_ref, lim_ref, lst_ref, bre_ref, bim_ref, cre_ref, cim_ref, win_ref, wout_ref):
    lre, lim = lre_ref[0, 0], lim_ref[0, 0]
    step = jnp.exp(lst_ref[0, 0])
    mag, ang = jnp.exp(lre * step), lim * step
    lbr, lbi = mag * jnp.cos(ang), mag * jnp.sin(ang)
    inv_den = 1.0 / (lre * lre + lim * lim)
    ar, ai, br, bi = lbr - 1.0, lbi, lre * inv_den, -lim * inv_den
    zr, zi = ar * br - ai * bi, ar * bi + ai * br
    b_re, b_im = bre_ref[0], bim_ref[0]
    rows, n = lre.shape
    w = (rows // S5_GROUP) * n
    assert n & (n - 1) == 0 and S5_GROUP & (S5_GROUP - 1) == 0
    spread = (lax.broadcasted_iota(jnp.int32, (n, w), 1) & (n - 1)) == lax.broadcasted_iota(jnp.int32, (n, w), 0)
    own = ((lax.broadcasted_iota(jnp.int32, (rows, w), 0) >> (S5_GROUP.bit_length() - 1))
           == (lax.broadcasted_iota(jnp.int32, (rows, w), 1) >> (n.bit_length() - 1)))

    def block_diag(t):
        tiled = _dot(t.astype(BF16), jnp.where(spread, 1.0, 0.0).astype(BF16))
        return jnp.where(own, tiled, 0.0)

    win_ref[0, 0, :, 0:w] = block_diag(zr * b_re - zi * b_im).astype(BF16)
    win_ref[0, 0, :, w:2 * w] = block_diag(zr * b_im + zi * b_re).astype(BF16)
    wout_ref[0, 0, 0:w, :] = block_diag(cre_ref[0, 0]).T.astype(BF16)
    wout_ref[0, 0, w:2 * w, :] = block_diag(-cim_ref[0, 0]).T.astype(BF16)


def _s5_weights(lam_re, lam_im, log_step, b_re, b_im, c_re, c_im):
    g, n, c = S5_GROUPS, S5_STATE, S5_GROUP
    per = g // S5_KB
    rows, w = per * c, per * n
    by_row = lambda t: jnp.broadcast_to(t.reshape(2, S5_KB, per, 1, n), (2, S5_KB, per, c, n)).reshape(
        2, S5_KB, rows, n)
    lst = jnp.broadcast_to(log_step[:, :, None], (2, g, n))
    b_rows = lambda t: t.transpose(0, 2, 1).reshape(S5_KB, rows, n)
    c_rows = lambda t: t.reshape(2, S5_KB, rows, n)
    per_dir = pl.BlockSpec((1, 1, rows, n), lambda d, k: (d, k, 0, 0))
    shared = pl.BlockSpec((1, rows, n), lambda d, k: (k, 0, 0))
    return pl.pallas_call(
        _s5_weights_kernel,
        grid=(2, S5_KB),
        in_specs=[per_dir, per_dir, per_dir, shared, shared, per_dir, per_dir],
        out_specs=[pl.BlockSpec((1, 1, rows, 2 * w), lambda d, k: (d, k, 0, 0)),
                   pl.BlockSpec((1, 1, 2 * w, rows), lambda d, k: (d, k, 0, 0))],
        out_shape=[jax.ShapeDtypeStruct((2, S5_KB, rows, 2 * w), BF16),
                   jax.ShapeDtypeStruct((2, S5_KB, 2 * w, rows), BF16)],
        compiler_params=_params("arbitrary", "arbitrary"),
        name="s5_weights",
    )(by_row(lam_re), by_row(lam_im), by_row(lst), b_rows(b_re), b_rows(b_im), c_rows(c_re), c_rows(c_im))


def _s5_prep_kernel(lre4_ref, lim4_ref, lst4_ref, pw_ref):
    r = lax.broadcasted_iota(jnp.int32, (2 * S5_TILE, S5_LANES), 0)
    coarse = r < S5_TILE
    rr = jnp.where(coarse, r, r - S5_TILE).astype(F32)
    mult = jnp.where(coarse, float(S5_TILE), 1.0)
    for d in range(2):
        k = mult * (rr + 1.0 if d == 0 else float(S5_TILE) - rr)
        st = jnp.exp(lst4_ref[d])
        m, a = jnp.exp(k * (lre4_ref[d] * st)), k * (lim4_ref[d] * st)
        pw_ref[d, 0] = m * jnp.cos(a)
        pw_ref[d, 1] = m * jnp.sin(a)


def _s5_prep(lam_re, lam_im, log_step):
    g, n = S5_GROUPS, S5_STATE
    lst = jnp.broadcast_to(log_step[:, :, None], (2, g, n))
    flat = lambda t: t.reshape(2, 1, g * n)
    full = lambda s: pl.BlockSpec(s, lambda: (0,) * len(s))
    return pl.pallas_call(
        _s5_prep_kernel,
        in_specs=[full((2, 1, g * n))] * 3,
        out_specs=full((2, 2, 2 * S5_TILE, g * n)),
        out_shape=jax.ShapeDtypeStruct((2, 2, 2 * S5_TILE, g * n), F32),
        name="s5_prep",
    )(flat(lam_re), flat(lam_im), flat(lst))


def _s5_core_kernel(hf_ref, hb_ref, nw_ref, win_ref, wout_ref, pw_ref, yf_ref, yb_ref,
                    car_ref, hsc_ref, ysc_ref, *, rt):
    @pl.when(pl.program_id(0) == 0)
    def _():
        car_ref[...] = jnp.zeros_like(car_ref)

    nw = nw_ref[...]
    ts = S5_TILE
    nt = rt // ts
    w = S5_LANES // S5_KB
    cin = D_MODEL // S5_KB
    rowi = lax.broadcasted_iota(jnp.int32, (ts, w), 0)
    tile = lambda v: jnp.broadcast_to(v, (ts, w))

    def cmul_add(a_r, a_i, s_r, s_i, x_r, x_i):
        return a_r * s_r - a_i * s_i + x_r, a_r * s_i + a_i * s_r + x_i

    jobs = [(d, kb) for d in range(2) for kb in range(S5_KB)]
    refs = ((hf_ref, yf_ref), (hb_ref, yb_ref))
    lanes = 128
    nl = D_MODEL // lanes
    hn = []
    for d in range(2):
        for c in range(nl):
            hsc_ref[c] = refs[d][0][:, c * lanes:(c + 1) * lanes]
        hn.append([_rms(jnp.concatenate([hsc_ref[c, pl.ds(t, rt, stride=ts), :] for c in range(nl)], axis=1),
                        nw).astype(BF16) for t in range(ts)])

    def drive(d, kb):
        lhs = jnp.concatenate([hn[d][t][:, kb * cin:(kb + 1) * cin] for t in range(ts)], axis=0)
        return _dot(lhs, win_ref[d, kb])

    x_next = drive(*jobs[0])
    for idx, (d, kb) in enumerate(jobs):
        x = x_next
        if idx + 1 < len(jobs):
            x_next = drive(*jobs[idx + 1])
        y_ref = refs[d][1]
        xr = [x[t * rt:(t + 1) * rt, :w] for t in range(ts)]
        xi = [x[t * rt:(t + 1) * rt, w:] for t in range(ts)]
        p_r = pw_ref[d, 0, 0:ts, kb * w:(kb + 1) * w]
        p_i = pw_ref[d, 1, 0:ts, kb * w:(kb + 1) * w]
        one = ts if d == 0 else 2 * ts - 1
        a_r = pw_ref[d, 0, one:one + 1, kb * w:(kb + 1) * w]
        a_i = pw_ref[d, 1, one:one + 1, kb * w:(kb + 1) * w]
        order = list(range(ts)) if d == 0 else list(range(ts - 1, -1, -1))

        e_r, e_i = xr[order[0]], xi[order[0]]
        for t in order[1:]:
            e_r, e_i = cmul_add(a_r, a_i, e_r, e_i, xr[t], xi[t])

        consts = []
        for k in (1, 2, 4):
            src = k - 1 if d == 0 else ts - k
            valid = rowi >= k if d == 0 else rowi <= ts - 1 - k
            consts.append((jnp.where(valid, tile(p_r[src:src + 1]), 0.0),
                           jnp.where(valid, tile(p_i[src:src + 1]), 0.0)))
        last, edge, back = (ts - 1, 0, 1) if d == 0 else (0, ts - 1, ts - 1)
        c_r, c_i = car_ref[d, kb, 0], car_ref[d, kb, 1]
        in_r, in_i = [None] * nt, [None] * nt
        for jj in range(nt):
            j = jj if d == 0 else nt - 1 - jj
            s_r, s_i = e_r[j * ts:(j + 1) * ts], e_i[j * ts:(j + 1) * ts]
            for k, (k_r, k_i) in zip((1, 2, 4), consts):
                sh = k if d == 0 else ts - k
                t_r, t_i = pltpu.roll(s_r, sh, 0), pltpu.roll(s_i, sh, 0)
                s_r, s_i = s_r + k_r * t_r - k_i * t_i, s_i + k_r * t_i + k_i * t_r
            s_r, s_i = s_r + p_r * c_r - p_i * c_i, s_i + p_r * c_i + p_i * c_r
            in_r[j] = jnp.where(rowi == edge, c_r, pltpu.roll(s_r, back, 0))
            in_i[j] = jnp.where(rowi == edge, c_i, pltpu.roll(s_i, back, 0))
            c_r, c_i = tile(s_r[last:last + 1]), tile(s_i[last:last + 1])
        car_ref[d, kb, 0] = c_r
        car_ref[d, kb, 1] = c_i

        s_r, s_i = jnp.concatenate(in_r, axis=0), jnp.concatenate(in_i, axis=0)
        states = [None] * ts
        for t in order:
            s_r, s_i = cmul_add(a_r, a_i, s_r, s_i, xr[t], xi[t])
            states[t] = jnp.concatenate([s_r, s_i], axis=1).astype(BF16)
        yk = _dot(jnp.concatenate(states, axis=0), wout_ref[d, kb])
        for t in range(ts):
            for c in range(cin // lanes):
                ysc_ref[kb * (cin // lanes) + c, pl.ds(t, rt, stride=ts), :] = (
                    yk[t * rt:(t + 1) * rt, c * lanes:(c + 1) * lanes])
        if kb == S5_KB - 1:
            for c in range(nl):
                y_ref[:, c * lanes:(c + 1) * lanes] = ysc_ref[c].astype(y_ref.dtype)


def _s5_core(h, nw, w_in, w_out, pw):
    L = h.shape[0]
    rb = min(S5_ROWS, L)
    nb = L // rb
    w = S5_LANES // S5_KB
    ts = S5_TILE
    rt = rb // ts
    fwd = lambda b: (b, 0)
    bwd = lambda b: (nb - 1 - b, 0)
    return pl.pallas_call(
        functools.partial(_s5_core_kernel, rt=rt),
        grid=(nb,),
        in_specs=[
            pl.BlockSpec((rb, D_MODEL), fwd),
            pl.BlockSpec((rb, D_MODEL), bwd),
            _resident((1, D_MODEL)),
            _resident(w_in.shape),
            _resident(w_out.shape),
            _resident(pw.shape),
        ],
        out_specs=[pl.BlockSpec((rb, D_MODEL), fwd), pl.BlockSpec((rb, D_MODEL), bwd)],
        out_shape=[jax.ShapeDtypeStruct((L, D_MODEL), BF16)] * 2,
        scratch_shapes=[pltpu.VMEM((2, S5_KB, 2, ts, w), F32),
                        pltpu.VMEM((D_MODEL // 128, rb, 128), F32),
                        pltpu.VMEM((D_MODEL // 128, rb, 128), F32)],
        compiler_params=_params("arbitrary"),
        name="s5_core",
    )(h, h, nw, w_in, w_out, pw)


def _s5_out_kernel(yf_ref, yb_ref, h_ref, nw_ref, dsk_ref, wg_ref, bg_ref, pnw_ref, *ffn_refs):
    h = h_ref[...]
    hn = _rms(h, nw_ref[...])
    y = yf_ref[...].astype(F32) + yb_ref[...].astype(F32) + dsk_ref[...] * hn
    y = jax.nn.gelu(y)
    vg = _dot(y.astype(BF16), wg_ref[...]) + bg_ref[...]
    m = vg[:, :D_MODEL] * jax.nn.sigmoid(vg[:, D_MODEL:])
    _ffn_block(h + _rms(m, pnw_ref[...]), *ffn_refs)


def _s5_layer(h, nw_pre, nw_post, lam_re, lam_im, log_step, b_re, b_im, c_re, c_im, d_skip, w_glu, b_glu,
              ffn):
    nw = nw_pre.reshape(1, -1)
    pw = _s5_prep(lam_re, lam_im, log_step)
    w_in, w_out = _s5_weights(lam_re, lam_im, log_step, b_re, b_im, c_re, c_im)
    yf, yb = _s5_core(h, nw, w_in, w_out, pw)
    consts = (nw, d_skip.reshape(1, -1), w_glu.astype(BF16), b_glu.reshape(1, -1), nw_post.reshape(1, -1))
    return _mixer_out_call(_s5_out_kernel, "s5_out_ffn", h, (yf, yb), consts, ffn)


def kernel(x, norm_w, ssd_w_in, ssd_conv_w, ssd_conv_b, ssd_dt_bias, ssd_a_log, ssd_d, ssd_norm_w, ssd_w_out, gdn_w_in, gdn_conv_w, gdn_conv_b, gdn_dt_bias, gdn_a_log, gdn_norm_w, gdn_w_out, s5_lam_re, s5_lam_im, s5_log_step, s5_b_re, s5_b_im, s5_c_re, s5_c_im, s5_d, s5_w_glu, s5_b_glu, ffn_w_gate_up, ffn_w_down):
    bt, L, d = x.shape
    assert bt == 1 and d == D_MODEL
    h = x.reshape(L, d)
    for i in range(norm_w.shape[0]):
        kind, j = i % 3, i // 3
        ffn = (norm_w[i, 2], norm_w[i, 3], ffn_w_gate_up[i], ffn_w_down[i])
        if kind == 0:
            h = _ssd_layer(h, norm_w[i, 0], norm_w[i, 1], ssd_w_in[j], ssd_conv_w[j], ssd_conv_b[j],
                           ssd_dt_bias[j], ssd_a_log[j], ssd_d[j], ssd_norm_w[j], ssd_w_out[j], ffn)
        elif kind == 1:
            h = _gdn_layer(h, norm_w[i, 0], norm_w[i, 1], gdn_w_in[j], gdn_conv_w[j], gdn_conv_b[j],
                           gdn_dt_bias[j], gdn_a_log[j], gdn_norm_w[j], gdn_w_out[j], ffn)
        else:
            h = _s5_layer(h, norm_w[i, 0], norm_w[i, 1], s5_lam_re[j], s5_lam_im[j], s5_log_step[j],
                          s5_b_re[j], s5_b_im[j], s5_c_re[j], s5_c_im[j], s5_d[j], s5_w_glu[j], s5_b_glu[j], ffn)
    return h.reshape(bt, L, d)
```

```python
import functools

import jax
import jax.numpy as jnp
from jax import lax
from jax.experimental import pallas as pl
from jax.experimental.pallas import tpu as pltpu

F32 = jnp.float32
BF16 = jnp.bfloat16
HIGHEST = lax.Precision.HIGHEST

D_MODEL = 1024
NORM_EPS = 1e-6
CONV_TAPS = 5
CONV_CH = 4096
Z_CH = 2048
SMALL_CH = 128
HALO = 8
CONV_CHUNK = 512

SSD_HEADS = 32
SSD_HEAD_DIM = 64
SSD_GROUPS = 8
SSD_STATE = 128
SSD_CHUNK = 128
SSD_D_INNER = 2048

GDN_HEAD_DIM = 128
GDN_QK_HEADS = 8
GDN_V_HEADS = 16
GDN_KEY_DIM = GDN_QK_HEADS * GDN_HEAD_DIM
GDN_CHUNK = 64
GDN_VALUE_DIM = 2048

S5_GROUPS = 64
S5_GROUP = 16
S5_STATE = 64
S5_LANES = S5_GROUPS * S5_STATE
S5_KB = 4
S5_TILE = 8

D_FF = 2816
FFN_CHUNK = 256

ROW_TILE = 512
GDN_ROWS = 512
S5_ROWS = 512
VMEM_LIMIT_BYTES = 56 * 1024 * 1024

NT_DIMS = (((1,), (1,)), ((), ()))
TN_DIMS = (((0,), (0,)), ((), ()))


def _params(*semantics):
    return pltpu.CompilerParams(dimension_semantics=semantics, vmem_limit_bytes=VMEM_LIMIT_BYTES)


def _resident(shape):
    nd = len(shape)
    return pl.BlockSpec(shape, lambda *_: (0,) * nd, pipeline_mode=pl.Buffered(1))


def _rms(x, w):
    return x * lax.rsqrt(jnp.mean(x * x, axis=-1, keepdims=True) + NORM_EPS) * w


def _silu(x):
    return x * jax.nn.sigmoid(x)


def _dot(a, b):
    return jnp.dot(a, b, preferred_element_type=F32)


def _inproj_kernel(h_ref, hp_ref, hx_ref, nw_ref, wc_ref, wz_ref, ws_ref, cw_ref, cb_ref, sb_ref, sa_ref,
                   act_ref, z_ref, s_ref, hs_ref, ys_ref, *, tm, nblk, cchunk, gdn):
    i = pl.program_id(0)
    nw = nw_ref[...]
    curb = _rms(h_ref[...], nw).astype(BF16)
    z_ref[...] = _dot(curb, wz_ref[...]).astype(z_ref.dtype)
    raw = _dot(curb, ws_ref[...]) + sb_ref[...]
    if gdn:
        is_decay = lax.broadcasted_iota(jnp.int32, raw.shape, 1) < 2 * GDN_V_HEADS
        s_ref[...] = jnp.where(is_decay, -jnp.exp(sa_ref[...]) * jax.nn.softplus(raw), jax.nn.sigmoid(raw))
    else:
        s_ref[...] = jax.nn.softplus(raw)
    pad = CONV_TAPS // 2
    hd = GDN_HEAD_DIM
    lanes = 128
    rows = tm + 2 * HALO
    seg = rows // 8
    for s in range(D_MODEL // lanes):
        ls = slice(s * lanes, (s + 1) * lanes)
        hs_ref[s, 0:HALO, :] = hp_ref[:, ls]
        hs_ref[s, HALO:HALO + tm, :] = h_ref[:, ls]
        hs_ref[s, HALO + tm:rows, :] = hx_ref[:, ls]
    ext = jnp.concatenate(
        [jnp.concatenate([hs_ref[s, pl.ds(j, 8, stride=seg), :] for s in range(D_MODEL // lanes)], axis=1)
         for j in range(seg)], axis=0)
    r = lax.broadcasted_iota(jnp.int32, (rows, 1), 0)
    orig = (r % 8) * seg + r // 8
    inside = jnp.logical_and(jnp.logical_or(orig >= HALO, i > 0),
                             jnp.logical_or(orig < HALO + tm, i < nblk - 1))
    ext = jnp.where(inside, _rms(ext, nw), 0.0).astype(BF16)
    for c in range(CONV_CH // cchunk):
        sl = slice(c * cchunk, (c + 1) * cchunk)
        p = _dot(ext, wc_ref[:, sl])
        head = [pltpu.roll(p[(seg - pad + m) * 8:(seg - pad + m + 1) * 8], 1, 0) for m in range(pad)]
        tail = [pltpu.roll(p[m * 8:(m + 1) * 8], 7, 0) for m in range(pad)]
        p_ext = jnp.concatenate(head + [p] + tail, axis=0)
        acc = cb_ref[:, sl]
        for k in range(CONV_TAPS):
            acc = acc + cw_ref[k:k + 1, sl] * p_ext[8 * k:8 * k + rows]
        y = _silu(acc)
        if gdn and c * cchunk < 2 * GDN_KEY_DIM:
            parts = []
            for s in range(cchunk // hd):
                blk = y[:, s * hd:(s + 1) * hd]
                inv = lax.rsqrt(jnp.sum(blk * blk, axis=-1, keepdims=True) + NORM_EPS)
                if c * cchunk + s * hd < GDN_KEY_DIM:
                    inv = inv * (hd ** -0.5)
                parts.append(blk * inv)
            y = jnp.concatenate(parts, axis=1)
        for s in range(cchunk // lanes):
            for j in range(seg):
                ys_ref[s, pl.ds(j, 8, stride=seg), :] = y[8 * j:8 * j + 8, s * lanes:(s + 1) * lanes]
        for s in range(cchunk // lanes):
            act_ref[:, c * cchunk + s * lanes:c * cchunk + (s + 1) * lanes] = (
                ys_ref[s, HALO:HALO + tm, :].astype(act_ref.dtype))


def _inproj(h, nw, wc, wz, ws, conv_w, conv_b, s_bias, s_alog, gdn):
    L = h.shape[0]
    tm = min(ROW_TILE, L)
    nblk = L // tm
    per = tm // HALO
    last = L // HALO - 1
    kern = functools.partial(_inproj_kernel, tm=tm, nblk=nblk, cchunk=CONV_CHUNK, gdn=gdn)
    return pl.pallas_call(
        kern,
        grid=(nblk,),
        in_specs=[
            pl.BlockSpec((tm, D_MODEL), lambda i: (i, 0)),
            pl.BlockSpec((HALO, D_MODEL), lambda i: (jnp.maximum(i * per - 1, 0), 0)),
            pl.BlockSpec((HALO, D_MODEL), lambda i: (jnp.minimum((i + 1) * per, last), 0)),
            _resident((1, D_MODEL)),
            _resident((D_MODEL, CONV_CH)),
            _resident((D_MODEL, Z_CH)),
            _resident((D_MODEL, SMALL_CH)),
            _resident((8, CONV_CH)),
            _resident((1, CONV_CH)),
            _resident((1, SMALL_CH)),
            _resident((1, SMALL_CH)),
        ],
        out_specs=[
            pl.BlockSpec((tm, CONV_CH), lambda i: (i, 0)),
            pl.BlockSpec((tm, Z_CH), lambda i: (i, 0)),
            pl.BlockSpec((tm, SMALL_CH), lambda i: (i, 0)),
        ],
        out_shape=[
            jax.ShapeDtypeStruct((L, CONV_CH), BF16),
            jax.ShapeDtypeStruct((L, Z_CH), BF16),
            jax.ShapeDtypeStruct((L, SMALL_CH), F32),
        ],
        scratch_shapes=[pltpu.VMEM((D_MODEL // 128, tm + 2 * HALO, 128), F32),
                        pltpu.VMEM((CONV_CHUNK // 128, tm + 2 * HALO, 128), F32)],
        compiler_params=_params("arbitrary"),
        name="inproj_conv",
    )(h, h, h, nw, wc, wz, ws, conv_w, conv_b, s_bias, s_alog)


def _pad_small(t):
    t = t.reshape(1, -1)
    return jnp.pad(t, ((0, 0), (0, SMALL_CH - t.shape[1])))


def _split_in_weights(w_in, conv_w, conv_b, conv_first):
    if conv_first:
        wc, wz = w_in[:, :CONV_CH], w_in[:, CONV_CH:CONV_CH + Z_CH]
    else:
        wz, wc = w_in[:, :Z_CH], w_in[:, Z_CH:Z_CH + CONV_CH]
    ws = w_in[:, CONV_CH + Z_CH:]
    ws = jnp.pad(ws, ((0, 0), (0, SMALL_CH - ws.shape[1])))
    cw = jnp.pad(conv_w, ((0, 8 - CONV_TAPS), (0, 0)))
    return wc.astype(BF16), wz.astype(BF16), ws.astype(BF16), cw, conv_b.reshape(1, CONV_CH)


def _ssd_core_kernel(xf_ref, xb_ref, dtf_ref, dtb_ref, alog_ref, dskip_ref, exb_ref,
                     yf_ref, yb_ref, sf_ref, sb_ref, *, q):
    @pl.when(pl.program_id(0) == 0)
    def _():
        sf_ref[...] = jnp.zeros_like(sf_ref)
        sb_ref[...] = jnp.zeros_like(sb_ref)

    row = lax.broadcasted_iota(jnp.int32, (q, q), 0)
    col = lax.broadcasted_iota(jnp.int32, (q, q), 1)
    lower = row >= col
    upper = row <= col
    lower_f = lower.astype(F32)
    upper_f = upper.astype(F32)
    a = -jnp.exp(alog_ref[...])
    p = SSD_HEAD_DIM
    per_group = SSD_HEADS // SSD_GROUPS
    gw = per_group * p
    pw = 2 * p
    b_off = SSD_D_INNER
    c_off = SSD_D_INNER + SSD_GROUPS * SSD_STATE
    left = lax.broadcasted_iota(jnp.int32, (q, pw), 1) < p

    dirs = ((xf_ref, dtf_ref, yf_ref, sf_ref), (xb_ref, dtb_ref, yb_ref, sb_ref))
    pre = []
    for d, (x_ref, dt_ref, y_ref, s_ref) in enumerate(dirs):
        dt = dt_ref[...]
        da = dt * a
        tri_c, tri_r = (lower_f, upper_f) if d == 0 else (upper_f, lower_f)
        cs = jnp.dot(tri_c, da, precision=HIGHEST, preferred_element_type=F32)
        cs_t = jnp.dot(da.T, tri_r, precision=HIGHEST, preferred_element_type=F32)
        tot = cs[q - 1:q, :] if d == 0 else cs[0:1, :]
        exb = exb_ref[d]
        dt_x = _dot(dt.astype(BF16), exb)
        e_x = _dot(jnp.exp(cs).astype(BF16), exb)
        w_x = _dot((dt * jnp.exp(tot - cs)).astype(BF16), exb)
        t_hi = tot.astype(BF16)
        t_mid = (tot - t_hi.astype(F32)).astype(BF16)
        t_lo = (tot - t_hi.astype(F32) - t_mid.astype(F32)).astype(BF16)
        pieces = jnp.concatenate([jnp.broadcast_to(t, (8, SMALL_CH)) for t in (t_hi, t_mid, t_lo)], axis=0)
        tot_3 = _dot(pieces, exb)
        tot_x = tot_3[0:1] + tot_3[8:9] + tot_3[16:17]
        pre.append((cs, cs_t, dt_x, e_x, w_x, jnp.exp(tot_x)))

    def front(d, g):
        x_ref, s_ref = dirs[d][0], dirs[d][3]
        bmat = x_ref[:, b_off + g * SSD_STATE:b_off + (g + 1) * SSD_STATE]
        cmat = x_ref[:, c_off + g * SSD_STATE:c_off + (g + 1) * SSD_STATE]
        cb = lax.dot_general(cmat, bmat, NT_DIMS, preferred_element_type=F32)
        yoff = _dot(cmat, s_ref[g].astype(BF16))
        return bmat, cb, yoff

    jobs = [(d, g) for d in range(2) for g in range(SSD_GROUPS)]
    ahead = front(*jobs[0])
    for idx, (d, g) in enumerate(jobs):
        bmat, cb, yoff = ahead
        if idx + 1 < len(jobs):
            ahead = front(*jobs[idx + 1])
        xg = dirs[d][0][:, g * gw:(g + 1) * gw].astype(F32)
        cs, cs_t, dt_x, e_x, w_x, cdec_x = pre[d]
        mask = lower if d == 0 else upper
        y_ref = dirs[d][2]
        for pr in range(per_group // 2):
            h0 = per_group * g + 2 * pr
            lms = []
            for hh in (h0, h0 + 1):
                ch = hh + SSD_HEADS * d
                lms.append(cb * jnp.exp(jnp.where(mask, cs[:, ch:ch + 1] - cs_t[ch:ch + 1, :], -jnp.inf)))
            m_pair = jnp.concatenate(lms, axis=1).astype(BF16)
            sl = slice(h0 * p, h0 * p + pw)
            lo = slice(2 * pr * p, 2 * pr * p + pw)
            xp = xg[:, lo]
            xdt = xp * dt_x[:, sl]
            rhs = jnp.concatenate([jnp.where(left, xdt, 0.0), jnp.where(left, 0.0, xdt)], axis=0).astype(BF16)
            y = _dot(m_pair, rhs) + yoff[:, lo] * e_x[:, sl]
            if d == 0:
                y = y + xp * dskip_ref[:, sl]
            y_ref[:, sl] = y.astype(y_ref.dtype)
        s_ref = dirs[d][3]
        gs = slice(g * gw, (g + 1) * gw)
        new = lax.dot_general(bmat, (xg * w_x[:, gs]).astype(BF16), TN_DIMS, preferred_element_type=F32)
        s_ref[g] = s_ref[g] * cdec_x[:, gs] + new


def _ssd_core(act, small, a_log, d_skip):
    L = act.shape[0]
    q = SSD_CHUNK
    nc = L // q
    alog = _pad_small(a_log)
    dskip = jnp.repeat(d_skip, SSD_HEAD_DIM).reshape(1, SSD_D_INNER)
    head_of_channel = jnp.arange(SSD_D_INNER) // SSD_HEAD_DIM
    col_id = jnp.arange(SMALL_CH)
    expand = jnp.stack([col_id[:, None] == head_of_channel[None, :] + SSD_HEADS * d for d in range(2)])
    gw = SSD_D_INNER // SSD_GROUPS
    fwd = lambda c: (c, 0)
    bwd = lambda c: (nc - 1 - c, 0)
    return pl.pallas_call(
        functools.partial(_ssd_core_kernel, q=q),
        grid=(nc,),
        in_specs=[
            pl.BlockSpec((q, CONV_CH), fwd),
            pl.BlockSpec((q, CONV_CH), bwd),
            pl.BlockSpec((q, SMALL_CH), fwd),
            pl.BlockSpec((q, SMALL_CH), bwd),
            _resident((1, SMALL_CH)),
            _resident((1, SSD_D_INNER)),
            _resident((2, SMALL_CH, SSD_D_INNER)),
        ],
        out_specs=[pl.BlockSpec((q, SSD_D_INNER), fwd), pl.BlockSpec((q, SSD_D_INNER), bwd)],
        out_shape=[jax.ShapeDtypeStruct((L, SSD_D_INNER), BF16)] * 2,
        scratch_shapes=[pltpu.VMEM((SSD_GROUPS, SSD_STATE, gw), F32)] * 2,
        compiler_params=_params("arbitrary"),
        name="ssd_core",
    )(act, act, small, small, alog, dskip, expand.astype(BF16))


def _ssd_out_kernel(yf_ref, yb_ref, z_ref, h_ref, gnw_ref, wo_ref, pnw_ref, *ffn_refs):
    y = yf_ref[...].astype(F32) + yb_ref[...].astype(F32)
    yz = y * _silu(z_ref[...].astype(F32))
    gw = SSD_D_INNER // SSD_GROUPS
    parts = []
    for g in range(SSD_GROUPS):
        blk = yz[:, g * gw:(g + 1) * gw]
        parts.append(blk * lax.rsqrt(jnp.mean(blk * blk, axis=-1, keepdims=True) + NORM_EPS))
    yn = (jnp.concatenate(parts, axis=1) * gnw_ref[...]).astype(BF16)
    m = _dot(yn, wo_ref[...])
    _ffn_block(h_ref[...] + _rms(m, pnw_ref[...]), *ffn_refs)


def _ffn_block(h, nw_ref, wgu_ref, wd_ref, pnw_ref, o_ref, act_ref):
    hn = _rms(h, nw_ref[...]).astype(BF16)
    for c in range(D_FF // FFN_CHUNK):
        gate = _dot(hn, wgu_ref[:, c * FFN_CHUNK:(c + 1) * FFN_CHUNK])
        up = _dot(hn, wgu_ref[:, D_FF + c * FFN_CHUNK:D_FF + (c + 1) * FFN_CHUNK])
        act_ref[:, c * FFN_CHUNK:(c + 1) * FFN_CHUNK] = (_silu(gate) * up).astype(BF16)
    f = _dot(act_ref[...], wd_ref[...])
    o_ref[...] = h + _rms(f, pnw_ref[...])


def _mixer_out_call(kern, name, h, streams, consts, ffn):
    L = h.shape[0]
    tm = min(ROW_TILE, L)
    nw_pre, nw_post, w_gate_up, w_down = ffn
    consts = tuple(consts) + (nw_pre.reshape(1, -1), w_gate_up.astype(BF16), w_down.astype(BF16),
                              nw_post.reshape(1, -1))
    tiled = lambda a: pl.BlockSpec((tm, a.shape[1]), lambda i: (i, 0))
    return pl.pallas_call(
        kern,
        grid=(L // tm,),
        in_specs=[tiled(a) for a in streams] + [tiled(h)] + [_resident(c.shape) for c in consts],
        out_specs=tiled(h),
        out_shape=jax.ShapeDtypeStruct(h.shape, F32),
        scratch_shapes=[pltpu.VMEM((tm, D_FF), BF16)],
        compiler_params=_params("arbitrary"),
        name=name,
    )(*streams, h, *consts)


def _ssd_layer(h, nw_pre, nw_post, w_in, conv_w, conv_b, dt_bias, a_log, d_skip, norm_w, w_out, ffn):
    wc, wz, ws, cw, cb = _split_in_weights(w_in, conv_w, conv_b, conv_first=False)
    act, z, small = _inproj(h, nw_pre.reshape(1, -1), wc, wz, ws, cw, cb, _pad_small(dt_bias), _pad_small(a_log),
                            gdn=False)
    yf, yb = _ssd_core(act, small, a_log, d_skip)
    consts = (norm_w.reshape(1, -1), w_out.astype(BF16), nw_post.reshape(1, -1))
    return _mixer_out_call(_ssd_out_kernel, "ssd_out_ffn", h, (yf, yb, z), consts, ffn)


def _gdn_core_kernel(xf_ref, xb_ref, gcf_ref, gcb_ref, grf_ref, grb_ref, ex_ref, exh_ref,
                     of_ref, ob_ref, s_ref, u_ref, wq_ref, kd_ref, qkm_ref, et_ref, *, rb, ck):
    @pl.when(pl.program_id(0) == 0)
    def _():
        s_ref[...] = jnp.zeros_like(s_ref)

    n = rb // ck
    hd = GDN_HEAD_DIM
    nq, nv = GDN_QK_HEADS, GDN_V_HEADS
    k_off, v_off = nq * hd, 2 * nq * hd
    row = lax.broadcasted_iota(jnp.int32, (ck, ck), 0)
    col = lax.broadcasted_iota(jnp.int32, (ck, ck), 1)
    lower_f, upper_f = (row >= col).astype(F32), (row <= col).astype(F32)
    row2 = lax.broadcasted_iota(jnp.int32, (ck, 2 * ck), 0)
    lane2 = lax.broadcasted_iota(jnp.int32, (ck, 2 * ck), 1)
    left = lane2 < ck
    col2 = jnp.where(left, lane2, lane2 - ck)
    pair = lambda t: jnp.concatenate([t, t], axis=1)
    eye2 = (row2 == col2).astype(F32)

    def block_diag(p2):
        return jnp.concatenate([jnp.where(left, p2, 0.0), jnp.where(left, 0.0, p2)], axis=0).astype(BF16)

    def solve_chunk(ci, carry):
        pairs = []
        for d in range(2):
            x_ref, gc_ref, gr_ref = (xf_ref, gcf_ref, grf_ref) if d == 0 else (xb_ref, gcb_ref, grb_ref)
            cc = ci if d == 0 else n - 1 - ci
            rows = pl.ds(pl.multiple_of(cc * ck, ck), ck)
            tri_c, tri_r = (lower_f, upper_f) if d == 0 else (upper_f, lower_f)
            mask2, strict2 = (row2 >= col2, row2 > col2) if d == 0 else (row2 <= col2, row2 < col2)
            gb_c = gc_ref[rows, :]
            gb_r = gr_ref[cc]
            cs_c = jnp.dot(tri_c, gb_c, precision=HIGHEST, preferred_element_type=F32)
            cs_r = jnp.dot(gb_r, tri_r, precision=HIGHEST, preferred_element_type=F32)
            tot = cs_c[ck - 1:ck, :] if d == 0 else cs_c[0:1, :]
            et_ref[d, cc] = jnp.exp(tot)
            beta_al = pltpu.roll(gb_c, SMALL_CH - 2 * nv, axis=1)
            ex, ex_half = ex_ref[d], exh_ref[d]
            beta_b = beta_al.astype(BF16)
            beta_h = _dot(beta_b, ex_half)
            stacked = jnp.concatenate([beta_b, jnp.exp(cs_c).astype(BF16), jnp.exp(tot - cs_c).astype(BF16)], axis=0)
            stacked_x = _dot(stacked, ex)
            beta_x = stacked_x[:ck]
            e_x = stacked_x[ck:2 * ck]
            kf_x = stacked_x[2 * ck:]
            for hq in range(nq):
                qn = x_ref[rows, hq * hd:(hq + 1) * hd]
                kn = x_ref[rows, k_off + hq * hd:k_off + (hq + 1) * hd]
                gram = lax.dot_general(jnp.concatenate([kn, qn], axis=0), kn, NT_DIMS,
                                       preferred_element_type=F32)
                kk2, qk2 = pair(gram[:ck]), pair(gram[ck:])
                c0 = nv * d + 2 * hq
                gcs_c2 = jnp.where(left, cs_c[:, c0:c0 + 1], cs_c[:, c0 + 1:c0 + 2])
                gcs_r2 = jnp.concatenate([cs_r[c0:c0 + 1, :], cs_r[c0 + 1:c0 + 2, :]], axis=1)
                beta2 = beta_h[:, hq * 2 * ck:(hq + 1) * 2 * ck]
                decay2 = jnp.exp(jnp.where(mask2, gcs_c2 - gcs_r2, -jnp.inf))
                a2 = jnp.where(strict2, beta2 * kk2 * decay2, 0.0)
                qkm_ref[d, hq, cc] = (qk2 * decay2).astype(BF16)
                qf, kf = qn.astype(F32), kn.astype(F32)
                rhs = []
                for j in range(2):
                    vh = 2 * hq + j
                    hs = slice(vh * hd, (vh + 1) * hd)
                    beta_c, e_g = beta_x[:, hs], e_x[:, hs]
                    vx = x_ref[rows, v_off + vh * hd:v_off + (vh + 1) * hd].astype(F32)
                    rhs.append(jnp.concatenate([vx * beta_c, kf * (beta_c * e_g)], axis=1).astype(BF16))
                    wq_ref[d, vh, cc, ck:2 * ck, :] = (qf * e_g).astype(BF16)
                    kd_ref[d, vh, cc] = (kf * kf_x[:, hs]).astype(BF16)
                zero = jnp.zeros_like(rhs[0])
                rhs2 = jnp.concatenate([jnp.concatenate([rhs[0], zero], axis=1),
                                        jnp.concatenate([zero, rhs[1]], axis=1)], axis=0)
                pairs.append((d, hq, cc, a2, rhs2))
        pws = [_dot(a2.astype(BF16), block_diag(a2)) for (_, _, _, a2, _) in pairs]
        tis = [eye2 - a2 for (_, _, _, a2, _) in pairs]
        for level in range(5):
            if level < 4:
                res = [_dot(jnp.concatenate([t, p], axis=0).astype(BF16), block_diag(p)) for t, p in zip(tis, pws)]
                tis = [t + r[:ck] for t, r in zip(tis, res)]
                pws = [r[ck:] for r in res]
            else:
                tis = [t + _dot(t.astype(BF16), block_diag(p)) for t, p in zip(tis, pws)]
        for (d, hq, cc, _, rhs2), t in zip(pairs, tis):
            sol = _dot(t.astype(BF16), rhs2)
            for j in range(2):
                vh = 2 * hq + j
                u_ref[d, vh, cc] = sol[:, 2 * j * hd:(2 * j + 1) * hd]
                wq_ref[d, vh, cc, 0:ck, :] = sol[:, (2 * j + 1) * hd:(2 * j + 2) * hd].astype(BF16)
        return carry

    def scan_chunk(ci, carry):
        units = [(d, vh, ci if d == 0 else n - 1 - ci) for d in range(2) for vh in range(nv)]
        boths = [_dot(wq_ref[d, vh, cc], s_ref[d, vh].astype(BF16)) for d, vh, cc in units]
        v_news = [(u_ref[d, vh, cc] - both[:ck]).astype(BF16) for (d, vh, cc), both in zip(units, boths)]
        for i in range(0, len(units), 2):
            d, vh, cc = units[i]
            o_ref = of_ref if d == 0 else ob_ref
            rows = pl.ds(pl.multiple_of(cc * ck, ck), ck)
            zero = jnp.zeros_like(v_news[i])
            v2 = jnp.concatenate([jnp.concatenate([v_news[i], zero], axis=1),
                                  jnp.concatenate([zero, v_news[i + 1]], axis=1)], axis=0)
            o2 = jnp.concatenate([boths[i][ck:], boths[i + 1][ck:]], axis=1) + _dot(qkm_ref[d, vh // 2, cc], v2)
            o_ref[rows, vh * hd:(vh + 2) * hd] = o2.astype(o_ref.dtype)
        for (d, vh, cc), v_newb in zip(units, v_news):
            scale = et_ref[d, cc][:, nv * d + vh:nv * d + vh + 1]
            s_ref[d, vh] = s_ref[d, vh] * scale + lax.dot_general(
                kd_ref[d, vh, cc], v_newb, TN_DIMS, preferred_element_type=F32)
        return carry

    lax.fori_loop(0, n, solve_chunk, 0)
    lax.fori_loop(0, n, scan_chunk, 0)


def _gdn_core(act, small):
    L = act.shape[0]
    ck = GDN_CHUNK
    rb = min(GDN_ROWS, L)
    nb = L // rb
    nv = GDN_V_HEADS
    hd = GDN_HEAD_DIM
    n = rb // ck
    g_row = small[:, :4 * nv].reshape(L // ck, ck, 4 * nv).transpose(0, 2, 1)
    col_id = jnp.arange(SMALL_CH)[:, None]
    expand = lambda w: jnp.stack([col_id == jnp.arange(nv * w)[None, :] // w + nv * d
                                  for d in range(2)]).astype(BF16)
    ex, ex_half = expand(hd), expand(ck)
    fwd = lambda b: (b, 0)
    bwd = lambda b: (nb - 1 - b, 0)
    return pl.pallas_call(
        functools.partial(_gdn_core_kernel, rb=rb, ck=ck),
        grid=(nb,),
        in_specs=[
            pl.BlockSpec((rb, CONV_CH), fwd),
            pl.BlockSpec((rb, CONV_CH), bwd),
            pl.BlockSpec((rb, SMALL_CH), fwd),
            pl.BlockSpec((rb, SMALL_CH), bwd),
            pl.BlockSpec((n, 4 * nv, ck), lambda b: (b, 0, 0)),
            pl.BlockSpec((n, 4 * nv, ck), lambda b: (nb - 1 - b, 0, 0)),
            _resident(ex.shape),
            _resident(ex_half.shape),
        ],
        out_specs=[pl.BlockSpec((rb, GDN_VALUE_DIM), fwd), pl.BlockSpec((rb, GDN_VALUE_DIM), bwd)],
        out_shape=[jax.ShapeDtypeStruct((L, GDN_VALUE_DIM), BF16)] * 2,
        scratch_shapes=[
            pltpu.VMEM((2, nv, hd, hd), F32),
            pltpu.VMEM((2, nv, n, ck, hd), F32),
            pltpu.VMEM((2, nv, n, 2 * ck, hd), BF16),
            pltpu.VMEM((2, nv, n, ck, hd), BF16),
            pltpu.VMEM((2, nv // 2, n, ck, 2 * ck), BF16),
            pltpu.VMEM((2, n, 1, SMALL_CH), F32),
        ],
        compiler_params=_params("arbitrary"),
        name="gdn_core",
    )(act, act, small, small, g_row, g_row, ex, ex_half)


def _gdn_out_kernel(of_ref, ob_ref, z_ref, h_ref, hnw_ref, wo_ref, pnw_ref, *ffn_refs):
    o = of_ref[...].astype(F32) + ob_ref[...].astype(F32)
    z = z_ref[...].astype(F32)
    hd = GDN_HEAD_DIM
    hnw = hnw_ref[...]
    parts = []
    for v in range(GDN_VALUE_DIM // hd):
        blk = o[:, v * hd:(v + 1) * hd]
        parts.append(_rms(blk, hnw) * _silu(z[:, v * hd:(v + 1) * hd]))
    m = _dot(jnp.concatenate(parts, axis=1).astype(BF16), wo_ref[...])
    _ffn_block(h_ref[...] + _rms(m, pnw_ref[...]), *ffn_refs)


def _gdn_layer(h, nw_pre, nw_post, w_in, conv_w, conv_b, dt_bias, a_log, norm_w, w_out, ffn):
    wc, wz, ws, cw, cb = _split_in_weights(w_in, conv_w, conv_b, conv_first=True)
    act, z, small = _inproj(h, nw_pre.reshape(1, -1), wc, wz, ws, cw, cb, _pad_small(dt_bias), _pad_small(a_log),
                            gdn=True)
    of, ob = _gdn_core(act, small)
    consts = (norm_w.reshape(1, -1), w_out.astype(BF16), nw_post.reshape(1, -1))
    return _mixer_out_call(_gdn_out_kernel, "gdn_out_ffn", h, (of, ob, z), consts, ffn)


def _s5_weights_kernel(lre_ref, lim_ref, lst_ref, bre_ref, bim_ref, cre_ref, cim_ref, win_ref, wout_ref):
    lre, lim = lre_ref[0, 0], lim_ref[0, 0]
    step = jnp.exp(lst_ref[0, 0])
    mag, ang = jnp.exp(lre * step), lim * step
    lbr, lbi = mag * jnp.cos(ang), mag * jnp.sin(ang)
    inv_den = 1.0 / (lre * lre + lim * lim)
    ar, ai, br, bi = lbr - 1.0, lbi, lre * inv_den, -lim * inv_den
    zr, zi = ar * br - ai * bi, ar * bi + ai * br
    b_re, b_im = bre_ref[0], bim_ref[0]
    rows, n = lre.shape
    w = (rows // S5_GROUP) * n
    assert n & (n - 1) == 0 and S5_GROUP & (S5_GROUP - 1) == 0
    spread = (lax.broadcasted_iota(jnp.int32, (n, w), 1) & (n - 1)) == lax.broadcasted_iota(jnp.int32, (n, w), 0)
    own = ((lax.broadcasted_iota(jnp.int32, (rows, w), 0) >> (S5_GROUP.bit_length() - 1))
           == (lax.broadcasted_iota(jnp.int32, (rows, w), 1) >> (n.bit_length() - 1)))

    def block_diag(t):
        tiled = _dot(t.astype(BF16), jnp.where(spread, 1.0, 0.0).astype(BF16))
        return jnp.where(own, tiled, 0.0)

    win_ref[0, 0, :, 0:w] = block_diag(zr * b_re - zi * b_im).astype(BF16)
    win_ref[0, 0, :, w:2 * w] = block_diag(zr * b_im + zi * b_re).astype(BF16)
    wout_ref[0, 0, 0:w, :] = block_diag(cre_ref[0, 0]).T.astype(BF16)
    wout_ref[0, 0, w:2 * w, :] = block_diag(-cim_ref[0, 0]).T.astype(BF16)


def _s5_weights(lam_re, lam_im, log_step, b_re, b_im, c_re, c_im):
    g, n, c = S5_GROUPS, S5_STATE, S5_GROUP
    per = g // S5_KB
    rows, w = per * c, per * n
    by_row = lambda t: jnp.broadcast_to(t.reshape(2, S5_KB, per, 1, n), (2, S5_KB, per, c, n)).reshape(
        2, S5_KB, rows, n)
    lst = jnp.broadcast_to(log_step[:, :, None], (2, g, n))
    b_rows = lambda t: t.transpose(0, 2, 1).reshape(S5_KB, rows, n)
    c_rows = lambda t: t.reshape(2, S5_KB, rows, n)
    per_dir = pl.BlockSpec((1, 1, rows, n), lambda d, k: (d, k, 0, 0))
    shared = pl.BlockSpec((1, rows, n), lambda d, k: (k, 0, 0))
    return pl.pallas_call(
        _s5_weights_kernel,
        grid=(2, S5_KB),
        in_specs=[per_dir, per_dir, per_dir, shared, shared, per_dir, per_dir],
        out_specs=[pl.BlockSpec((1, 1, rows, 2 * w), lambda d, k: (d, k, 0, 0)),
                   pl.BlockSpec((1, 1, 2 * w, rows), lambda d, k: (d, k, 0, 0))],
        out_shape=[jax.ShapeDtypeStruct((2, S5_KB, rows, 2 * w), BF16),
                   jax.ShapeDtypeStruct((2, S5_KB, 2 * w, rows), BF16)],
        compiler_params=_params("arbitrary", "arbitrary"),
        name="s5_weights",
    )(by_row(lam_re), by_row(lam_im), by_row(lst), b_rows(b_re), b_rows(b_im), c_rows(c_re), c_rows(c_im))


def _s5_prep_kernel(lre4_ref, lim4_ref, lst4_ref, pw_ref):
    r = lax.broadcasted_iota(jnp.int32, (2 * S5_TILE, S5_LANES), 0)
    coarse = r < S5_TILE
    rr = jnp.where(coarse, r, r - S5_TILE).astype(F32)
    mult = jnp.where(coarse, float(S5_TILE), 1.0)
    for d in range(2):
        k = mult * (rr + 1.0 if d == 0 else float(S5_TILE) - rr)
        st = jnp.exp(lst4_ref[d])
        m, a = jnp.exp(k * (lre4_ref[d] * st)), k * (lim4_ref[d] * st)
        pw_ref[d, 0] = m * jnp.cos(a)
        pw_ref[d, 1] = m * jnp.sin(a)


def _s5_prep(lam_re, lam_im, log_step):
    g, n = S5_GROUPS, S5_STATE
    lst = jnp.broadcast_to(log_step[:, :, None], (2, g, n))
    flat = lambda t: t.reshape(2, 1, g * n)
    full = lambda s: pl.BlockSpec(s, lambda: (0,) * len(s))
    return pl.pallas_call(
        _s5_prep_kernel,
        in_specs=[full((2, 1, g * n))] * 3,
        out_specs=full((2, 2, 2 * S5_TILE, g * n)),
        out_shape=jax.ShapeDtypeStruct((2, 2, 2 * S5_TILE, g * n), F32),
        name="s5_prep",
    )(flat(lam_re), flat(lam_im), flat(lst))


def _s5_core_kernel(hf_ref, hb_ref, nw_ref, win_ref, wout_ref, pw_ref, yf_ref, yb_ref,
                    car_ref, hsc_ref, ysc_ref, *, rt):
    @pl.when(pl.program_id(0) == 0)
    def _():
        car_ref[...] = jnp.zeros_like(car_ref)

    nw = nw_ref[...]
    ts = S5_TILE
    nt = rt // ts
    w = S5_LANES // S5_KB
    cin = D_MODEL // S5_KB
    rowi = lax.broadcasted_iota(jnp.int32, (ts, w), 0)
    tile = lambda v: jnp.broadcast_to(v, (ts, w))

    def cmul_add(a_r, a_i, s_r, s_i, x_r, x_i):
        return a_r * s_r - a_i * s_i + x_r, a_r * s_i + a_i * s_r + x_i

    jobs = [(d, kb) for d in range(2) for kb in range(S5_KB)]
    refs = ((hf_ref, yf_ref), (hb_ref, yb_ref))
    lanes = 128
    nl = D_MODEL // lanes
    hn = []
    for d in range(2):
        for c in range(nl):
            hsc_ref[c] = refs[d][0][:, c * lanes:(c + 1) * lanes]
        hn.append([_rms(jnp.concatenate([hsc_ref[c, pl.ds(t, rt, stride=ts), :] for c in range(nl)], axis=1),
                        nw).astype(BF16) for t in range(ts)])

    def drive(d, kb):
        lhs = jnp.concatenate([hn[d][t][:, kb * cin:(kb + 1) * cin] for t in range(ts)], axis=0)
        return _dot(lhs, win_ref[d, kb])

    x_next = drive(*jobs[0])
    for idx, (d, kb) in enumerate(jobs):
        x = x_next
        if idx + 1 < len(jobs):
            x_next = drive(*jobs[idx + 1])
        y_ref = refs[d][1]
        xr = [x[t * rt:(t + 1) * rt, :w] for t in range(ts)]
        xi = [x[t * rt:(t + 1) * rt, w:] for t in range(ts)]
        p_r = pw_ref[d, 0, 0:ts, kb * w:(kb + 1) * w]
        p_i = pw_ref[d, 1, 0:ts, kb * w:(kb + 1) * w]
        one = ts if d == 0 else 2 * ts - 1
        a_r = pw_ref[d, 0, one:one + 1, kb * w:(kb + 1) * w]
        a_i = pw_ref[d, 1, one:one + 1, kb * w:(kb + 1) * w]
        order = list(range(ts)) if d == 0 else list(range(ts - 1, -1, -1))

        e_r, e_i = xr[order[0]], xi[order[0]]
        for t in order[1:]:
            e_r, e_i = cmul_add(a_r, a_i, e_r, e_i, xr[t], xi[t])

        consts = []
        for k in (1, 2, 4):
            src = k - 1 if d == 0 else ts - k
            valid = rowi >= k if d == 0 else rowi <= ts - 1 - k
            consts.append((jnp.where(valid, tile(p_r[src:src + 1]), 0.0),
                           jnp.where(valid, tile(p_i[src:src + 1]), 0.0)))
        last, edge, back = (ts - 1, 0, 1) if d == 0 else (0, ts - 1, ts - 1)
        c_r, c_i = car_ref[d, kb, 0], car_ref[d, kb, 1]
        in_r, in_i = [None] * nt, [None] * nt
        for jj in range(nt):
            j = jj if d == 0 else nt - 1 - jj
            s_r, s_i = e_r[j * ts:(j + 1) * ts], e_i[j * ts:(j + 1) * ts]
            for k, (k_r, k_i) in zip((1, 2, 4), consts):
                sh = k if d == 0 else ts - k
                t_r, t_i = pltpu.roll(s_r, sh, 0), pltpu.roll(s_i, sh, 0)
                s_r, s_i = s_r + k_r * t_r - k_i * t_i, s_i + k_r * t_i + k_i * t_r
            s_r, s_i = s_r + p_r * c_r - p_i * c_i, s_i + p_r * c_i + p_i * c_r
            in_r[j] = jnp.where(rowi == edge, c_r, pltpu.roll(s_r, back, 0))
            in_i[j] = jnp.where(rowi == edge, c_i, pltpu.roll(s_i, back, 0))
            c_r, c_i = tile(s_r[last:last + 1]), tile(s_i[last:last + 1])
        car_ref[d, kb, 0] = c_r
        car_ref[d, kb, 1] = c_i

        s_r, s_i = jnp.concatenate(in_r, axis=0), jnp.concatenate(in_i, axis=0)
        states = [None] * ts
        for t in order:
            s_r, s_i = cmul_add(a_r, a_i, s_r, s_i, xr[t], xi[t])
            states[t] = jnp.concatenate([s_r, s_i], axis=1).astype(BF16)
        yk = _dot(jnp.concatenate(states, axis=0), wout_ref[d, kb])
        for t in range(ts):
            for c in range(cin // lanes):
                ysc_ref[kb * (cin // lanes) + c, pl.ds(t, rt, stride=ts), :] = (
                    yk[t * rt:(t + 1) * rt, c * lanes:(c + 1) * lanes])
        if kb == S5_KB - 1:
            for c in range(nl):
                y_ref[:, c * lanes:(c + 1) * lanes] = ysc_ref[c].astype(y_ref.dtype)


def _s5_core(h, nw, w_in, w_out, pw):
    L = h.shape[0]
    rb = min(S5_ROWS, L)
    nb = L // rb
    w = S5_LANES // S5_KB
    ts = S5_TILE
    rt = rb // ts
    fwd = lambda b: (b, 0)
    bwd = lambda b: (nb - 1 - b, 0)
    return pl.pallas_call(
        functools.partial(_s5_core_kernel, rt=rt),
        grid=(nb,),
        in_specs=[
            pl.BlockSpec((rb, D_MODEL), fwd),
            pl.BlockSpec((rb, D_MODEL), bwd),
            _resident((1, D_MODEL)),
            _resident(w_in.shape),
            _resident(w_out.shape),
            _resident(pw.shape),
        ],
        out_specs=[pl.BlockSpec((rb, D_MODEL), fwd), pl.BlockSpec((rb, D_MODEL), bwd)],
        out_shape=[jax.ShapeDtypeStruct((L, D_MODEL), BF16)] * 2,
        scratch_shapes=[pltpu.VMEM((2, S5_KB, 2, ts, w), F32),
                        pltpu.VMEM((D_MODEL // 128, rb, 128), F32),
                        pltpu.VMEM((D_MODEL // 128, rb, 128), F32)],
        compiler_params=_params("arbitrary"),
        name="s5_core",
    )(h, h, nw, w_in, w_out, pw)


def _s5_out_kernel(yf_ref, yb_ref, h_ref, nw_ref, dsk_ref, wg_ref, bg_ref, pnw_ref, *ffn_refs):
    h = h_ref[...]
    hn = _rms(h, nw_ref[...])
    y = yf_ref[...].astype(F32) + yb_ref[...].astype(F32) + dsk_ref[...] * hn
    y = jax.nn.gelu(y)
    vg = _dot(y.astype(BF16), wg_ref[...]) + bg_ref[...]
    m = vg[:, :D_MODEL] * jax.nn.sigmoid(vg[:, D_MODEL:])
    _ffn_block(h + _rms(m, pnw_ref[...]), *ffn_refs)


def _s5_layer(h, nw_pre, nw_post, lam_re, lam_im, log_step, b_re, b_im, c_re, c_im, d_skip, w_glu, b_glu,
              ffn):
    nw = nw_pre.reshape(1, -1)
    pw = _s5_prep(lam_re, lam_im, log_step)
    w_in, w_out = _s5_weights(lam_re, lam_im, log_step, b_re, b_im, c_re, c_im)
    yf, yb = _s5_core(h, nw, w_in, w_out, pw)
    consts = (nw, d_skip.reshape(1, -1), w_glu.astype(BF16), b_glu.reshape(1, -1), nw_post.reshape(1, -1))
    return _mixer_out_call(_s5_out_kernel, "s5_out_ffn", h, (yf, yb), consts, ffn)


def kernel(x, norm_w, ssd_w_in, ssd_conv_w, ssd_conv_b, ssd_dt_bias, ssd_a_log, ssd_d, ssd_norm_w, ssd_w_out, gdn_w_in, gdn_conv_w, gdn_conv_b, gdn_dt_bias, gdn_a_log, gdn_norm_w, gdn_w_out, s5_lam_re, s5_lam_im, s5_log_step, s5_b_re, s5_b_im, s5_c_re, s5_c_im, s5_d, s5_w_glu, s5_b_glu, ffn_w_gate_up, ffn_w_down):
    bt, L, d = x.shape
    assert bt == 1 and d == D_MODEL
    h = x.reshape(L, d)
    for i in range(norm_w.shape[0]):
        kind, j = i % 3, i // 3
        ffn = (norm_w[i, 2], norm_w[i, 3], ffn_w_gate_up[i], ffn_w_down[i])
        if kind == 0:
            h = _ssd_layer(h, norm_w[i, 0], norm_w[i, 1], ssd_w_in[j], ssd_conv_w[j], ssd_conv_b[j],
                           ssd_dt_bias[j], ssd_a_log[j], ssd_d[j], ssd_norm_w[j], ssd_w_out[j], ffn)
        elif kind == 1:
            h = _gdn_layer(h, norm_w[i, 0], norm_w[i, 1], gdn_w_in[j], gdn_conv_w[j], gdn_conv_b[j],
                           gdn_dt_bias[j], gdn_a_log[j], gdn_norm_w[j], gdn_w_out[j], ffn)
        else:
            h = _s5_layer(h, norm_w[i, 0], norm_w[i, 1], s5_lam_re[j], s5_lam_im[j], s5_log_step[j],
                          s5_b_re[j], s5_b_im[j], s5_c_re[j], s5_c_im[j], s5_d[j], s5_w_glu[j], s5_b_glu[j], ffn)
    return h.reshape(bt, L, d)
```
